```python
import math
import jax, jax.numpy as jnp
from jax import lax
import numpy as np

D_MODEL = 1024
BATCH = 4
SEQ = 4096
DEPTH = 1

CONV_WIDTH = D_MODEL // 2
CONV_KERNEL = 31
SSM_WIDTH = D_MODEL // 2
SSM_GROUP = 16
SSM_GROUPS = SSM_WIDTH // SSM_GROUP
SSM_STATE = 64
FFN_HIDDEN = ((8 * D_MODEL + 3 * 256 - 1) // (3 * 256)) * 256
IN_COLS = 2 * CONV_WIDTH + SSM_WIDTH + 2 * D_MODEL
N_MOD = 6
EPS = 1e-6
DT_MIN = 1e-3
DT_MAX = 1e-1

kernel_name = "hybrid_conv_s5_gated_block"


def rmsnorm(x, g):
    x32 = x.astype(jnp.float32)
    y = x32 * lax.rsqrt(jnp.mean(x32 * x32, axis=-1, keepdims=True) + EPS)
    return (y * g.astype(jnp.float32)).astype(x.dtype)


def layernorm(x, g, b):
    x32 = x.astype(jnp.float32)
    mu = jnp.mean(x32, axis=-1, keepdims=True)
    var = jnp.mean(jnp.square(x32 - mu), axis=-1, keepdims=True)
    y = (x32 - mu) * lax.rsqrt(var + EPS)
    return (y * g.astype(jnp.float32) + b.astype(jnp.float32)).astype(x.dtype)


def conformer_conv(u_glu, conv_w, conv_b, ln_g, ln_b, w_proj):
    a, gate = jnp.split(u_glu, 2, axis=-1)
    u = a * jax.nn.sigmoid(gate)
    y = lax.conv_general_dilated(
        u, conv_w[:, None, :].astype(u.dtype), window_strides=(1,),
        padding=[(CONV_KERNEL - 1, 0)],
        dimension_numbers=("NWC", "WIO", "NWC"),
        feature_group_count=CONV_WIDTH) + conv_b
    y = jax.nn.silu(layernorm(y, ln_g, ln_b))
    return y @ w_proj


def _scan_combine(e1, e2):
    a1r, a1i, b1r, b1i = e1
    a2r, a2i, b2r, b2i = e2
    ar = a2r * a1r - a2i * a1i
    ai = a2r * a1i + a2i * a1r
    br = a2r * b1r - a2i * b1i + b2r
    bi = a2r * b1i + a2i * b1r + b2i
    return (ar, ai, br, bi)


def s5_mixer(u, a_re, a_im, b_re, b_im, c_re, c_im, d, log_dt, w_glu):
    bsz, seq, _ = u.shape
    f32 = jnp.float32
    u32 = u.astype(f32).reshape(bsz, seq, SSM_GROUPS, SSM_GROUP)
    ar, ai = a_re.astype(f32), a_im.astype(f32)
    dt = jnp.exp(log_dt.astype(f32))[:, None]
    mag = jnp.exp(dt * ar)
    e_re, e_im = mag * jnp.cos(dt * ai), mag * jnp.sin(dt * ai)
    n_re, n_im = e_re - 1.0, e_im
    den = ar * ar + ai * ai
    q_re = (n_re * ar + n_im * ai) / den
    q_im = (n_im * ar - n_re * ai) / den
    br32, bi32 = b_re.astype(f32), b_im.astype(f32)
    bb_re = q_re[..., None] * br32 - q_im[..., None] * bi32
    bb_im = q_re[..., None] * bi32 + q_im[..., None] * br32
    bu_re = jnp.einsum("bsgh,gph->bsgp", u32, bb_re)
    bu_im = jnp.einsum("bsgh,gph->bsgp", u32, bb_im)
    abar_re = jnp.broadcast_to(e_re, bu_re.shape)
    abar_im = jnp.broadcast_to(e_im, bu_re.shape)
    _, _, x_re, x_im = lax.associative_scan(
        _scan_combine, (abar_re, abar_im, bu_re, bu_im), axis=1)
    y = (jnp.einsum("bsgp,ghp->bsgh", x_re, c_re.astype(f32))
         - jnp.einsum("bsgp,ghp->bsgh", x_im, c_im.astype(f32)))
    y = y.reshape(bsz, seq, SSM_WIDTH) + d.astype(f32) * u.astype(f32)
    y = jax.nn.gelu(y).astype(u.dtype)
    za, zb = jnp.split(y @ w_glu, 2, axis=-1)
    return za * jax.nn.sigmoid(zb)


def setup_inputs(seed: int = 0) -> dict:
    key = jax.random.key(seed)
    ks = jax.random.split(key, 32)
    f32 = jnp.float32
    L, D, G, P, H = DEPTH, D_MODEL, SSM_GROUPS, SSM_STATE, SSM_GROUP

    def nrm(k, shape, fan_in):
        return jax.random.normal(k, shape, f32) * fan_in ** -0.5

    def gain(k, shape):
        return 1.0 + 0.05 * jax.random.normal(k, shape, f32)

    n_idx = jnp.arange(P, dtype=f32)
    a_re = -0.5 + 0.01 * jax.random.normal(ks[8], (L, G, P), f32)
    a_im = math.pi * n_idx[None, None, :] + 0.01 * jax.random.normal(ks[9], (L, G, P), f32)
    log_dt = jax.random.uniform(ks[10], (L, G), f32, math.log(DT_MIN), math.log(DT_MAX))
    return {
        "x": jax.random.normal(ks[0], (BATCH, SEQ, D), f32),
        "c": jax.random.normal(ks[1], (BATCH, D), f32),
        "w_ada": nrm(ks[2], (L, D, N_MOD * D), D) * 0.5,
        "b_ada": 0.02 * jax.random.normal(ks[3], (L, N_MOD * D), f32),
        "norm1_g": gain(ks[4], (L, D)),
        "w_in": nrm(ks[5], (L, D, IN_COLS), D),
        "conv_w": nrm(ks[6], (L, CONV_KERNEL, CONV_WIDTH), CONV_KERNEL),
        "conv_b": 0.02 * jax.random.normal(ks[7], (L, CONV_WIDTH), f32),
        "conv_ln_g": gain(ks[11], (L, CONV_WIDTH)),
        "conv_ln_b": 0.02 * jax.random.normal(ks[12], (L, CONV_WIDTH), f32),
        "conv_proj": nrm(ks[13], (L, CONV_WIDTH, D), CONV_WIDTH),
        "ssm_a_re": a_re,
        "ssm_a_im": a_im,
        "ssm_b_re": nrm(ks[14], (L, G, P, H), 2 * H),
        "ssm_b_im": nrm(ks[15], (L, G, P, H), 2 * H),
        "ssm_c_re": nrm(ks[16], (L, G, H, P), P),
        "ssm_c_im": nrm(ks[17], (L, G, H, P), P),
        "ssm_d": jax.random.normal(ks[18], (L, SSM_WIDTH), f32),
        "ssm_log_dt": log_dt,
        "ssm_glu": nrm(ks[19], (L, SSM_WIDTH, 2 * D), SSM_WIDTH),
        "w_out": nrm(ks[20], (L, D, D), D),
        "norm2_g": gain(ks[21], (L, D)),
        "w_ffn_in": nrm(ks[22], (L, D, 2 * FFN_HIDDEN), D),
        "w_ffn_out": nrm(ks[23], (L, FFN_HIDDEN, D), FFN_HIDDEN),
        "final_g": gain(ks[24], (D,)),
    }


def reference(x, c, w_ada, b_ada, norm1_g, w_in, conv_w, conv_b, conv_ln_g, conv_ln_b,
              conv_proj, ssm_a_re, ssm_a_im, ssm_b_re, ssm_b_im, ssm_c_re, ssm_c_im,
              ssm_d, ssm_log_dt, ssm_glu, w_out, norm2_g, w_ffn_in, w_ffn_out, final_g):
    c_act = jax.nn.silu(c)
    split_pts = [2 * CONV_WIDTH, 2 * CONV_WIDTH + SSM_WIDTH, 2 * CONV_WIDTH + SSM_WIDTH + D_MODEL]
    for l in range(DEPTH):
        mod = (c_act @ w_ada[l] + b_ada[l])[:, None, :]
        sh1, sc1, g1, sh2, sc2, g2 = jnp.split(mod, N_MOD, axis=-1)

        h = rmsnorm(x, norm1_g[l]) * (1.0 + sc1) + sh1
        z = h @ w_in[l]
        u_conv, u_ssm, gl_conv, gl_ssm = jnp.split(z, split_pts, axis=-1)
        y_conv = conformer_conv(u_conv, conv_w[l], conv_b[l], conv_ln_g[l],
                                conv_ln_b[l], conv_proj[l])
        y_ssm = s5_mixer(u_ssm, ssm_a_re[l], ssm_a_im[l], ssm_b_re[l], ssm_b_im[l],
                         ssm_c_re[l], ssm_c_im[l], ssm_d[l], ssm_log_dt[l], ssm_glu[l])
        merged = jax.nn.sigmoid(gl_conv) * y_conv + jax.nn.sigmoid(gl_ssm) * y_ssm
        x = x + g1 * (merged @ w_out[l])

        h = rmsnorm(x, norm2_g[l]) * (1.0 + sc2) + sh2
        f_gate, f_up = jnp.split(h @ w_ffn_in[l], 2, axis=-1)
        x = x + g2 * ((jax.nn.silu(f_gate) * f_up) @ w_ffn_out[l])
    return rmsnorm(x, final_g)
```

```python
import functools
import math

import jax
import jax.numpy as jnp
from jax import lax
from jax.experimental import pallas as pl
from jax.experimental.pallas import tpu as pltpu

F32 = jnp.float32
BF16 = jnp.bfloat16

D_MODEL = 1024
CONV_WIDTH = 512
CONV_KERNEL = 31
SSM_WIDTH = 512
SSM_GROUP = 16
SSM_GROUPS = 32
SSM_STATE = 64
FFN_HIDDEN = 2816
EPS = 1e-6

CHUNK = 16
PAIR_CH = 2 * SSM_GROUP
N_PAIRS = SSM_GROUPS // 2
PAIRS_PER_STEP = 4
CONV_HALO = 32

VMEM_LIMIT = 56 * 1024 * 1024


def _rms_mod(x, g, scale, shift):
    ms = jnp.mean(x * x, axis=-1, keepdims=True)
    y = x * lax.rsqrt(ms + EPS)
    return (y * g) * (1.0 + scale) + shift


def _sigmoid(v):
    return jax.nn.sigmoid(v)


def _mod_kernel(c_ref, w_ref, b_ref, o_ref):
    c = c_ref[...]
    ca = c * _sigmoid(c)
    o_ref[0] = jnp.dot(ca, w_ref[...], preferred_element_type=F32,
                       precision=lax.Precision.HIGHEST) + b_ref[...]


def _mod_call(c, w_ada, b_ada):
    bsz = c.shape[0]
    n_mod = w_ada.shape[1] // D_MODEL
    return pl.pallas_call(
        _mod_kernel,
        grid=(n_mod,),
        in_specs=[
            pl.BlockSpec((bsz, D_MODEL), lambda j: (0, 0)),
            pl.BlockSpec((D_MODEL, D_MODEL), lambda j: (0, j)),
            pl.BlockSpec((1, D_MODEL), lambda j: (0, j)),
        ],
        out_specs=pl.BlockSpec((1, bsz, D_MODEL), lambda j: (j, 0, 0)),
        out_shape=jax.ShapeDtypeStruct((n_mod, bsz, D_MODEL), F32),
        name="mod",
    )(c, w_ada, b_ada.reshape(1, -1))


def _ssm_in_kernel(x_ref, mod_ref, g_ref, w_ref, o_ref):
    h = _rms_mod(x_ref[0], g_ref[...], mod_ref[0, 1:2, :], mod_ref[0, 0:1, :])
    o_ref[0] = jnp.dot(h.astype(BF16), w_ref[...], preferred_element_type=F32)


def _ssm_in_call(x, mod, g, w_ssm, ts):
    bsz, seq, _ = x.shape
    return pl.pallas_call(
        _ssm_in_kernel,
        grid=(bsz, seq // ts),
        in_specs=[
            pl.BlockSpec((1, ts, D_MODEL), lambda b, s: (b, s, 0)),
            pl.BlockSpec((1, 6, D_MODEL), lambda b, s: (b, 0, 0)),
            pl.BlockSpec((1, D_MODEL), lambda b, s: (0, 0)),
            pl.BlockSpec((D_MODEL, SSM_WIDTH), lambda b, s: (0, 0)),
        ],
        out_specs=pl.BlockSpec((1, ts, SSM_WIDTH), lambda b, s: (b, s, 0)),
        out_shape=jax.ShapeDtypeStruct((bsz, seq, SSM_WIDTH), F32),
        compiler_params=pltpu.CompilerParams(
            dimension_semantics=("arbitrary", "arbitrary"),
            vmem_limit_bytes=VMEM_LIMIT),
        name="ssm_in",
    )(x, mod, g, w_ssm)


def _ssm_kernel(u_ref, kt_ref, bst_ref, cst_ref, lre_ref, lim_ref, d_ref, o_ref,
                sre, sim, xre, xim):
    n_chunks = u_ref.shape[-1]
    pw = 2 * SSM_STATE
    rows = CHUNK * PAIR_CH

    def load_pair(j):
        r = u_ref[0, :, j * PAIR_CH:(j + 1) * PAIR_CH, :]
        return r.reshape(rows, n_chunks)

    for j in range(PAIRS_PER_STEP):
        rb = load_pair(j).astype(BF16)
        s = lax.dot_general(rb, bst_ref[j], (((0,), (0,)), ((), ())),
                            preferred_element_type=F32)
        sre[:, j * pw:(j + 1) * pw] = s[:, :pw]
        sim[:, j * pw:(j + 1) * pw] = s[:, pw:]

    lre = lre_ref[0]
    lim = lim_ref[0]

    def body(c, carry):
        xr, xi = carry
        xre[pl.ds(c, 1), :] = xr
        xim[pl.ds(c, 1), :] = xi
        nr = lre * xr - lim * xi + sre[pl.ds(c, 1), :]
        ni = lre * xi + lim * xr + sim[pl.ds(c, 1), :]
        return nr, ni

    zero = jnp.zeros((1, PAIRS_PER_STEP * pw), F32)
    lax.fori_loop(0, n_chunks, body, (zero, zero), unroll=8)

    for j in range(PAIRS_PER_STEP):
        r = load_pair(j)
        y_intra = jnp.dot(kt_ref[j], r.astype(BF16), preferred_element_type=F32)
        xp = jnp.concatenate([xre[:, j * pw:(j + 1) * pw],
                              xim[:, j * pw:(j + 1) * pw]], axis=1).astype(BF16)
        y_state = lax.dot_general(cst_ref[j], xp, (((1,), (1,)), ((), ())),
                                  preferred_element_type=F32)
        y = y_intra + y_state + d_ref[j] * r
        y = jax.nn.gelu(y)
        o_ref[0, :, j * PAIR_CH:(j + 1) * PAIR_CH, :] = (
            y.reshape(CHUNK, PAIR_CH, n_chunks).astype(o_ref.dtype))


def _ssm_call(u_t, kt, bst, cst, lam_re, lam_im, d_col):
    bsz, _, _, n_chunks = u_t.shape
    pps = PAIRS_PER_STEP
    ch = pps * PAIR_CH
    state_w = pps * 2 * SSM_STATE
    rows = CHUNK * PAIR_CH
    return pl.pallas_call(
        _ssm_kernel,
        grid=(N_PAIRS // pps, bsz),
        in_specs=[
            pl.BlockSpec((1, CHUNK, ch, n_chunks), lambda p, b: (b, 0, p, 0)),
            pl.BlockSpec((pps, rows, rows), lambda p, b: (p, 0, 0)),
            pl.BlockSpec((pps, rows, 4 * SSM_STATE), lambda p, b: (p, 0, 0)),
            pl.BlockSpec((pps, rows, 4 * SSM_STATE), lambda p, b: (p, 0, 0)),
            pl.BlockSpec((1, 1, state_w), lambda p, b: (p, 0, 0)),
            pl.BlockSpec((1, 1, state_w), lambda p, b: (p, 0, 0)),
            pl.BlockSpec((pps, rows, 1), lambda p, b: (p, 0, 0)),
        ],
        out_specs=pl.BlockSpec((1, CHUNK, ch, n_chunks), lambda p, b: (b, 0, p, 0)),
        out_shape=jax.ShapeDtypeStruct(u_t.shape, BF16),
        scratch_shapes=[pltpu.VMEM((n_chunks, state_w), F32)] * 4,
        compiler_params=pltpu.CompilerParams(
            dimension_semantics=("arbitrary", "arbitrary"),
            vmem_limit_bytes=VMEM_LIMIT),
        name="ssm",
    )(u_t, kt, bst, cst, lam_re, lam_im, d_col)


def _ssm_params(a_re, a_im, b_re, b_im, c_re, c_im, d, log_dt):
    g, p, h, t = SSM_GROUPS, SSM_STATE, SSM_GROUP, CHUNK
    hi = lax.Precision.HIGHEST
    dt = jnp.exp(log_dt)[:, None]
    mag = jnp.exp(dt * a_re)
    e_re, e_im = mag * jnp.cos(dt * a_im), mag * jnp.sin(dt * a_im)
    n_re, n_im = e_re - 1.0, e_im
    den = a_re * a_re + a_im * a_im
    q_re = (n_re * a_re + n_im * a_im) / den
    q_im = (n_im * a_re - n_re * a_im) / den
    bb_re = q_re[..., None] * b_re - q_im[..., None] * b_im
    bb_im = q_re[..., None] * b_im + q_im[..., None] * b_re
    k = jnp.arange(t + 1, dtype=a_re.dtype)[:, None, None]
    pmag = jnp.exp(k * (dt * a_re)[None])
    pw_re = pmag * jnp.cos(k * (dt * a_im)[None])
    pw_im = pmag * jnp.sin(k * (dt * a_im)[None])
    eye2 = jnp.eye(2, dtype=a_re.dtype)

    cl_re = c_re[None] * pw_re[:, :, None, :] - c_im[None] * pw_im[:, :, None, :]
    cl_im = c_re[None] * pw_im[:, :, None, :] + c_im[None] * pw_re[:, :, None, :]

    kk = (jnp.einsum("kghp,gpj->kghj", cl_re[:t], bb_re, precision=hi)
          - jnp.einsum("kghp,gpj->kghj", cl_im[:t], bb_im, precision=hi))
    ti = jnp.arange(t)
    lag = ti[:, None] - ti[None, :]
    kt = jnp.where((lag >= 0)[:, :, None, None, None], kk[jnp.maximum(lag, 0)], 0.0)
    kt = kt.reshape(t, t, N_PAIRS, 2, h, h)
    kt = jnp.einsum("tsrahj,ab->rtahsbj", kt, eye2)
    kt = kt.reshape(N_PAIRS, t * 2 * h, t * 2 * h)

    rev_re, rev_im = pw_re[t - 1::-1][:t], pw_im[t - 1::-1][:t]
    sb_re = rev_re[..., None] * bb_re[None] - rev_im[..., None] * bb_im[None]
    sb_im = rev_re[..., None] * bb_im[None] + rev_im[..., None] * bb_re[None]
    sb = jnp.stack([sb_re, sb_im], 0).reshape(2, t, N_PAIRS, 2, p, h)
    bst = jnp.einsum("israpj,ab->rsajibp", sb, eye2)
    bst = bst.reshape(N_PAIRS, t * 2 * h, 2 * 2 * p)

    cs = jnp.stack([cl_re[1:], -cl_im[1:]], 0).reshape(2, t, N_PAIRS, 2, h, p)
    cst = jnp.einsum("itrahp,ab->rtahibp", cs, eye2)
    cst = cst.reshape(N_PAIRS, t * 2 * h, 2 * 2 * p)

    steps = N_PAIRS // PAIRS_PER_STEP
    lam_re = pw_re[t].reshape(steps, 1, -1)
    lam_im = pw_im[t].reshape(steps, 1, -1)
    d_col = jnp.broadcast_to(d.reshape(N_PAIRS, 1, 2 * h), (N_PAIRS, t, 2 * h))
    d_col = d_col.reshape(N_PAIRS, t * 2 * h, 1)
    return kt.astype(BF16), bst.astype(BF16), cst.astype(BF16), lam_re, lam_im, d_col


def _mixer_kernel(x_ref, y_ref, mod_ref, g_ref, wc_ref, wg_ref, cw_ref, cb_ref,
                  lng_ref, lnb_ref, proj_ref, glu_ref, wout_ref, o_ref, ubuf):
    ts = x_ref.shape[1]
    halo = CONV_HALO

    @pl.when(pl.program_id(1) == 0)
    def _():
        ubuf[0:halo, :] = jnp.zeros((halo, CONV_WIDTH), F32)

    x = x_ref[0]
    h = _rms_mod(x, g_ref[...], mod_ref[0, 1:2, :], mod_ref[0, 0:1, :]).astype(BF16)

    zc = jnp.dot(h, wc_ref[...], preferred_element_type=F32)
    u = zc[:, :CONV_WIDTH] * _sigmoid(zc[:, CONV_WIDTH:])
    ubuf[halo:halo + ts, :] = u
    acc = jnp.zeros((ts, CONV_WIDTH), F32)
    first = halo - (CONV_KERNEL - 1)
    for j in range(CONV_KERNEL):
        acc = acc + cw_ref[j:j + 1, :] * ubuf[first + j:first + j + ts, :]
    ubuf[0:halo, :] = ubuf[ts:ts + halo, :]
    acc = acc + cb_ref[...]
    mu = jnp.mean(acc, axis=-1, keepdims=True)
    cen = acc - mu
    var = jnp.mean(cen * cen, axis=-1, keepdims=True)
    yn = cen * lax.rsqrt(var + EPS) * lng_ref[...] + lnb_ref[...]
    yn = yn * _sigmoid(yn)
    y_conv = jnp.dot(yn.astype(BF16), proj_ref[...], preferred_element_type=F32)

    zs = jnp.dot(y_ref[0], glu_ref[...], preferred_element_type=F32)
    y_ssm = zs[:, :D_MODEL] * _sigmoid(zs[:, D_MODEL:])

    zg = jnp.dot(h, wg_ref[...], preferred_element_type=F32)
    merged = _sigmoid(zg[:, :D_MODEL]) * y_conv + _sigmoid(zg[:, D_MODEL:]) * y_ssm
    upd = jnp.dot(merged.astype(BF16), wout_ref[...], preferred_element_type=F32)
    o_ref[0] = x + mod_ref[0, 2:3, :] * upd


def _const_spec(shape):
    return pl.BlockSpec(shape, lambda b, s: (0,) * len(shape),
                        pipeline_mode=pl.Buffered(1))


def _mixer_call(x, y, mod, g, wc, wg, cw, cb, lng, lnb, proj, glu, wout, ts):
    bsz, seq, _ = x.shape
    return pl.pallas_call(
        _mixer_kernel,
        grid=(bsz, seq // ts),
        in_specs=[
            pl.BlockSpec((1, ts, D_MODEL), lambda b, s: (b, s, 0)),
            pl.BlockSpec((1, ts, SSM_WIDTH), lambda b, s: (b, s, 0)),
            pl.BlockSpec((1, 6, D_MODEL), lambda b, s: (b, 0, 0)),
            _const_spec((1, D_MODEL)),
            _const_spec((D_MODEL, 2 * CONV_WIDTH)),
            _const_spec((D_MODEL, 2 * D_MODEL)),
            _const_spec((CONV_KERNEL, CONV_WIDTH)),
            _const_spec((1, CONV_WIDTH)),
            _const_spec((1, CONV_WIDTH)),
            _const_spec((1, CONV_WIDTH)),
            _const_spec((CONV_WIDTH, D_MODEL)),
            _const_spec((SSM_WIDTH, 2 * D_MODEL)),
            _const_spec((D_MODEL, D_MODEL)),
        ],
        out_specs=pl.BlockSpec((1, ts, D_MODEL), lambda b, s: (b, s, 0)),
        out_shape=jax.ShapeDtypeStruct(x.shape, F32),
        scratch_shapes=[pltpu.VMEM((ts + CONV_HALO, CONV_WIDTH), F32)],
        compiler_params=pltpu.CompilerParams(
            dimension_semantics=("arbitrary", "arbitrary"),
            vmem_limit_bytes=VMEM_LIMIT),
        name="mixer",
    )(x, y, mod, g, wc, wg, cw, cb, lng, lnb, proj, glu, wout)


def _ffn_kernel(x_ref, mod_ref, g_ref, wg_ref, wu_ref, wo_ref, fg_ref, o_ref):
    x = x_ref[0]
    h = _rms_mod(x, g_ref[...], mod_ref[0, 4:5, :], mod_ref[0, 3:4, :]).astype(BF16)
    fg = jnp.dot(h, wg_ref[...], preferred_element_type=F32)
    fu = jnp.dot(h, wu_ref[...], preferred_element_type=F32)
    act = (fg * _sigmoid(fg) * fu).astype(BF16)
    upd = jnp.dot(act, wo_ref[...], preferred_element_type=F32)
    x2 = x + mod_ref[0, 5:6, :] * upd
    ms = jnp.mean(x2 * x2, axis=-1, keepdims=True)
    o_ref[0] = x2 * lax.rsqrt(ms + EPS) * fg_ref[...]


def _ffn_call(x, mod, g, wg, wu, wo, final_g, ts):
    bsz, seq, _ = x.shape
    return pl.pallas_call(
        _ffn_kernel,
        grid=(bsz, seq // ts),
        in_specs=[
            pl.BlockSpec((1, ts, D_MODEL), lambda b, s: (b, s, 0)),
            pl.BlockSpec((1, 6, D_MODEL), lambda b, s: (b, 0, 0)),
            _const_spec((1, D_MODEL)),
            _const_spec((D_MODEL, FFN_HIDDEN)),
            _const_spec((D_MODEL, FFN_HIDDEN)),
            _const_spec((FFN_HIDDEN, D_MODEL)),
            _const_spec((1, D_MODEL)),
        ],
        out_specs=pl.BlockSpec((1, ts, D_MODEL), lambda b, s: (b, s, 0)),
        out_shape=jax.ShapeDtypeStruct(x.shape, F32),
        compiler_params=pltpu.CompilerParams(
            dimension_semantics=("arbitrary", "arbitrary"),
            vmem_limit_bytes=VMEM_LIMIT),
        name="ffn",
    )(x, mod, g, wg, wu, wo, final_g)


def kernel(x, c, w_ada, b_ada, norm1_g, w_in, conv_w, conv_b, conv_ln_g, conv_ln_b,
           conv_proj, ssm_a_re, ssm_a_im, ssm_b_re, ssm_b_im, ssm_c_re, ssm_c_im,
           ssm_d, ssm_log_dt, ssm_glu, w_out, norm2_g, w_ffn_in, w_ffn_out, final_g):
    bsz, seq, _ = x.shape
    assert w_ada.shape[0] == 1, "single layer only"
    n_chunks = seq // CHUNK

    mod = _mod_call(c, w_ada[0], b_ada[0])
    mod = jnp.transpose(mod, (1, 0, 2))

    w_in0 = w_in[0]
    c0, c1 = 2 * CONV_WIDTH, 2 * CONV_WIDTH + SSM_WIDTH
    w_conv_in = w_in0[:, :c0].astype(BF16)
    w_ssm = w_in0[:, c0:c1].astype(BF16)
    w_gates = w_in0[:, c1:].astype(BF16)
    g1 = norm1_g[0].reshape(1, -1)

    u = _ssm_in_call(x, mod, g1, w_ssm, 1024)
    u_t = jnp.transpose(u.reshape(bsz, n_chunks, CHUNK, SSM_WIDTH), (0, 2, 3, 1))
    ops = _ssm_params(ssm_a_re[0], ssm_a_im[0], ssm_b_re[0], ssm_b_im[0],
                      ssm_c_re[0], ssm_c_im[0], ssm_d[0], ssm_log_dt[0])
    y_t = _ssm_call(u_t, *ops)
    y = jnp.transpose(y_t, (0, 3, 1, 2)).reshape(bsz, seq, SSM_WIDTH)

    x1 = _mixer_call(
        x, y, mod, g1, w_conv_in, w_gates, conv_w[0], conv_b[0].reshape(1, -1),
        conv_ln_g[0].reshape(1, -1), conv_ln_b[0].reshape(1, -1),
        conv_proj[0].astype(BF16), ssm_glu[0].astype(BF16), w_out[0].astype(BF16), 512)

    wf = w_ffn_in[0]
    return _ffn_call(x1, mod, norm2_g[0].reshape(1, -1),
                     wf[:, :FFN_HIDDEN].astype(BF16), wf[:, FFN_HIDDEN:].astype(BF16),
                     w_ffn_out[0].astype(BF16), final_g.reshape(1, -1), 512)
```

```python
import jax
import jax.numpy as jnp
from jax import lax
from jax.experimental import pallas as pl
from jax.experimental.pallas import tpu as pltpu

F32 = jnp.float32
BF16 = jnp.bfloat16

D_MODEL = 1024
CONV_WIDTH = 512
CONV_KERNEL = 31
SSM_WIDTH = 512
SSM_GROUP = 16
SSM_GROUPS = 32
SSM_STATE = 64
FFN_HIDDEN = 2816
EPS = 1e-6

CHUNK = 16
PAIR_CH = 2 * SSM_GROUP
PAIR_ST = 2 * SSM_STATE
PAIR_ROWS = CHUNK * PAIR_CH
N_PAIRS = SSM_GROUPS // 2
PAIRS_PER_STEP = 4
CONV_HALO = 32
GATE_BLOCK = 512
LANES = 128
SUBLANES = 8

VMEM_LIMIT = 56 * 1024 * 1024


def _rms_mod(x, g, scale, shift):
    ms = jnp.mean(x * x, axis=-1, keepdims=True)
    y = x * lax.rsqrt(ms + EPS)
    return (y * g) * (1.0 + scale) + shift


def _sigmoid(v):
    return jax.nn.sigmoid(v)


def _mod_kernel(c_ref, w_ref, b_ref, o_ref):
    c = c_ref[...]
    ca = c * _sigmoid(c)
    o_ref[0] = jnp.dot(ca, w_ref[...], preferred_element_type=F32,
                       precision=lax.Precision.HIGHEST) + b_ref[...]


def _mod_call(c, w_ada, b_ada):
    bsz = c.shape[0]
    n_mod = w_ada.shape[1] // D_MODEL
    return pl.pallas_call(
        _mod_kernel,
        grid=(n_mod,),
        in_specs=[
            pl.BlockSpec((bsz, D_MODEL), lambda j: (0, 0)),
            pl.BlockSpec((D_MODEL, D_MODEL), lambda j: (0, j)),
            pl.BlockSpec((1, D_MODEL), lambda j: (0, j)),
        ],
        out_specs=pl.BlockSpec((1, bsz, D_MODEL), lambda j: (j, 0, 0)),
        out_shape=jax.ShapeDtypeStruct((n_mod, bsz, D_MODEL), F32),
        name="mod",
    )(c, w_ada, b_ada.reshape(1, -1))


def _ssm_prep_kernel(ar_ref, ai_ref, ldt_ref, cre_ref, cim_ref, bre_ref, bim_ref,
                     dl_ref, kt_ref, bst_ref, cst_ref, lre_ref, lim_ref, cl_s, lb_s):
    ar, ai = ar_ref[0], ai_ref[0]
    dt = jnp.exp(ldt_ref[0])
    dar, dai = dt * ar, dt * ai
    mag = jnp.exp(dar)
    n_re, n_im = mag * jnp.cos(dai) - 1.0, mag * jnp.sin(dai)
    den = ar * ar + ai * ai
    q_re = (n_re * ar + n_im * ai) / den
    q_im = (n_im * ar - n_re * ai) / den

    row = lax.broadcasted_iota(jnp.int32, (PAIR_CH, PAIR_ST), 0)
    lane = lax.broadcasted_iota(jnp.int32, (PAIR_CH, PAIR_ST), 1)
    same_group = (row // SSM_GROUP) == (lane // SSM_STATE)

    def block_diag(ref):
        v = ref[0]
        return jnp.where(same_group, jnp.concatenate([v, v], axis=1), 0.0)

    c_re, c_im = block_diag(cre_ref), block_diag(cim_ref)
    b_re, b_im = block_diag(bre_ref), block_diag(bim_ref)
    bb_re = q_re * b_re - q_im * b_im
    bb_im = q_re * b_im + q_im * b_re

    powers = {}

    def power(k):
        if k not in powers:
            m = jnp.exp(float(k) * dar)
            powers[k] = (m * jnp.cos(float(k) * dai), m * jnp.sin(float(k) * dai))
        return powers[k]

    def cmul_cat(x_re, x_im, p, conj_out):
        p_re, p_im = p
        o_re = x_re * p_re - x_im * p_im
        o_im = x_re * p_im + x_im * p_re
        return jnp.concatenate([o_re, -o_im if conj_out else o_im], axis=1)

    for t in range(CHUNK):
        rs = slice(t * PAIR_CH, (t + 1) * PAIR_CH)
        cl_s[rs, :] = cmul_cat(c_re, c_im, power(t), True)
        lb_s[rs, :] = cmul_cat(bb_re, bb_im, power(-t), False)
        cst_ref[0, rs, :] = cmul_cat(c_re, c_im, power(t + 1), True).astype(BF16)
        bst_ref[0, rs, :] = cmul_cat(bb_re, bb_im, power(CHUNK - 1 - t),
                                     False).astype(BF16)
    lre_ref[0], lim_ref[0] = power(CHUNK)

    k = lax.dot_general(cl_s[...], lb_s[...], (((1,), (1,)), ((), ())),
                        preferred_element_type=F32, precision=lax.Precision.HIGHEST)
    ri = lax.broadcasted_iota(jnp.int32, (PAIR_ROWS, PAIR_ROWS), 0)
    ci = lax.broadcasted_iota(jnp.int32, (PAIR_ROWS, PAIR_ROWS), 1)
    k = jnp.where(ri // PAIR_CH >= ci // PAIR_CH, k, 0.0)
    k = k + jnp.where(ri == ci, dl_ref[0], 0.0)
    kt_ref[0] = k.astype(BF16)


def _ssm_prep_call(a_re, a_im, b_re, b_im, c_re, c_im, d, log_dt):
    np_, pc, ps, st = N_PAIRS, PAIR_CH, PAIR_ST, SSM_STATE
    row = lambda a: a.reshape(np_, 1, ps)
    ldt = jnp.broadcast_to(log_dt[:, None], (SSM_GROUPS, st))
    d_lane = jnp.tile(d.reshape(np_, 1, pc), (1, 1, CHUNK))
    bt = lambda a: jnp.transpose(a, (0, 2, 1)).reshape(np_, pc, st)
    spec = lambda *shape: pl.BlockSpec((1,) + shape, lambda p: (p, 0, 0))
    return pl.pallas_call(
        _ssm_prep_kernel,
        grid=(np_,),
        in_specs=[spec(1, ps), spec(1, ps), spec(1, ps), spec(pc, st), spec(pc, st),
                  spec(pc, st), spec(pc, st), spec(1, PAIR_ROWS)],
        out_specs=[spec(PAIR_ROWS, PAIR_ROWS), spec(PAIR_ROWS, 2 * ps),
                   spec(PAIR_ROWS, 2 * ps), spec(1, ps), spec(1, ps)],
        out_shape=[jax.ShapeDtypeStruct((np_, PAIR_ROWS, PAIR_ROWS), BF16),
                   jax.ShapeDtypeStruct((np_, PAIR_ROWS, 2 * ps), BF16),
                   jax.ShapeDtypeStruct((np_, PAIR_ROWS, 2 * ps), BF16),
                   jax.ShapeDtypeStruct((np_, 1, ps), F32),
                   jax.ShapeDtypeStruct((np_, 1, ps), F32)],
        scratch_shapes=[pltpu.VMEM((PAIR_ROWS, 2 * ps), F32)] * 2,
        name="ssm_prep",
    )(row(a_re), row(a_im), row(ldt), c_re.reshape(np_, pc, st),
      c_im.reshape(np_, pc, st), bt(b_re), bt(b_im), d_lane)


def _ssm_in_kernel(x_ref, mod_ref, g_ref, w_ref, o_ref, us):
    n_chunks = o_ref.shape[-1]
    h = _rms_mod(x_ref[0], g_ref[...], mod_ref[0, 1:2, :], mod_ref[0, 0:1, :])
    u = jnp.dot(h.astype(BF16), w_ref[...], preferred_element_type=F32)
    for k in range(SSM_WIDTH // LANES):
        us[k] = u[:, k * LANES:(k + 1) * LANES]
    for tau in range(CHUNK):
        for k in range(SSM_WIDTH // LANES):
            piece = us[k, pl.ds(tau, n_chunks, stride=CHUNK), :]
            o_ref[0, tau, k * LANES:(k + 1) * LANES, :] = piece.T.astype(o_ref.dtype)


def _ssm_in_call(x, mod, g, w_in_b, chunks_per_step):
    bsz, seq, _ = x.shape
    ts = chunks_per_step * CHUNK
    ssm_col_block = 2 * CONV_WIDTH // SSM_WIDTH
    return pl.pallas_call(
        _ssm_in_kernel,
        grid=(bsz, seq // ts),
        in_specs=[
            pl.BlockSpec((1, ts, D_MODEL), lambda b, s: (b, s, 0)),
            pl.BlockSpec((1, 6, D_MODEL), lambda b, s: (b, 0, 0)),
            pl.BlockSpec((1, D_MODEL), lambda b, s: (0, 0)),
            pl.BlockSpec((D_MODEL, SSM_WIDTH), lambda b, s: (0, ssm_col_block)),
        ],
        out_specs=pl.BlockSpec((1, CHUNK, SSM_WIDTH, chunks_per_step),
                               lambda b, s: (b, 0, 0, s)),
        out_shape=jax.ShapeDtypeStruct((bsz, CHUNK, SSM_WIDTH, seq // CHUNK), BF16),
        scratch_shapes=[pltpu.VMEM((SSM_WIDTH // LANES, ts, LANES), F32)],
        compiler_params=pltpu.CompilerParams(
            dimension_semantics=("arbitrary", "arbitrary"),
            vmem_limit_bytes=VMEM_LIMIT),
        name="ssm_in",
    )(x, mod, g, w_in_b)


def _ssm_kernel(u_ref, kt_ref, bst_ref, cst_ref, lre_ref, lim_ref, o_ref,
                sre, sim, xre, xim, ys, yn):
    n_chunks = u_ref.shape[-1]
    pw = PAIR_ST

    def load_pair(j):
        r = u_ref[0, :, j * PAIR_CH:(j + 1) * PAIR_CH, :]
        return r.reshape(PAIR_ROWS, n_chunks)

    for j in range(PAIRS_PER_STEP):
        s = lax.dot_general(load_pair(j), bst_ref[j], (((0,), (0,)), ((), ())),
                            preferred_element_type=F32)
        sre[:, j * pw:(j + 1) * pw] = s[:, :pw]
        sim[:, j * pw:(j + 1) * pw] = s[:, pw:]

    lre = lre_ref[0]
    lim = lim_ref[0]

    def body(c, carry):
        xr, xi = carry
        xre[pl.ds(c, 1), :] = xr
        xim[pl.ds(c, 1), :] = xi
        nr = lre * xr - lim * xi + sre[pl.ds(c, 1), :]
        ni = lre * xi + lim * xr + sim[pl.ds(c, 1), :]
        return nr, ni

    zero = jnp.zeros((1, PAIRS_PER_STEP * pw), F32)
    lax.fori_loop(0, n_chunks, body, (zero, zero), unroll=8)

    for j in range(PAIRS_PER_STEP):
        y_intra = jnp.dot(kt_ref[j], load_pair(j), preferred_element_type=F32)
        xp = jnp.concatenate([xre[:, j * pw:(j + 1) * pw],
                              xim[:, j * pw:(j + 1) * pw]], axis=1).astype(BF16)
        y_state = lax.dot_general(cst_ref[j], xp, (((1,), (1,)), ((), ())),
                                  preferred_element_type=F32)
        y = jax.nn.gelu(y_intra + y_state)
        ys[:, j * PAIR_CH:(j + 1) * PAIR_CH, :] = y.reshape(CHUNK, PAIR_CH, n_chunks)

    for t in range(CHUNK):
        yn[pl.ds(t, n_chunks, stride=CHUNK), :] = ys[t].T
    o_ref[0] = yn[...].astype(o_ref.dtype)


def _ssm_call(u_t, kt, bst, cst, lam_re, lam_im):
    bsz, _, _, n_chunks = u_t.shape
    seq = n_chunks * CHUNK
    pps = PAIRS_PER_STEP
    ch = pps * PAIR_CH
    state_w = pps * PAIR_ST
    steps = N_PAIRS // pps
    lam_re = lam_re.reshape(steps, 1, state_w)
    lam_im = lam_im.reshape(steps, 1, state_w)
    return pl.pallas_call(
        _ssm_kernel,
        grid=(steps, bsz),
        in_specs=[
            pl.BlockSpec((1, CHUNK, ch, n_chunks), lambda p, b: (b, 0, p, 0)),
            pl.BlockSpec((pps, PAIR_ROWS, PAIR_ROWS), lambda p, b: (p, 0, 0)),
            pl.BlockSpec((pps, PAIR_ROWS, 2 * PAIR_ST), lambda p, b: (p, 0, 0)),
            pl.BlockSpec((pps, PAIR_ROWS, 2 * PAIR_ST), lambda p, b: (p, 0, 0)),
            pl.BlockSpec((1, 1, state_w), lambda p, b: (p, 0, 0)),
            pl.BlockSpec((1, 1, state_w), lambda p, b: (p, 0, 0)),
        ],
        out_specs=pl.BlockSpec((1, seq, ch), lambda p, b: (b, 0, p)),
        out_shape=jax.ShapeDtypeStruct((bsz, seq, SSM_WIDTH), BF16),
        scratch_shapes=[pltpu.VMEM((n_chunks, state_w), F32)] * 4 + [
            pltpu.VMEM((CHUNK, ch, n_chunks), F32), pltpu.VMEM((seq, ch), F32)],
        compiler_params=pltpu.CompilerParams(
            dimension_semantics=("arbitrary", "arbitrary"),
            vmem_limit_bytes=VMEM_LIMIT),
        name="ssm",
    )(u_t, kt, bst, cst, lam_re, lam_im)


def _mixer_kernel(x_ref, y_ref, mod_ref, g_ref, wc_ref, wgc0_ref, wgc1_ref, wgs0_ref,
                  wgs1_ref, cw_ref, cb_ref, lng_ref, lnb_ref, proj_ref, glu_ref,
                  wout_ref, o_ref, ubuf):
    ts = x_ref.shape[1]
    halo = CONV_HALO
    half = D_MODEL // 2

    @pl.when(pl.program_id(1) == 0)
    def _():
        ubuf[0:halo, :] = jnp.zeros((halo, CONV_WIDTH), F32)
        ubuf[halo + ts:, :] = jnp.zeros((SUBLANES, CONV_WIDTH), F32)

    x = x_ref[0]
    h = _rms_mod(x, g_ref[...], mod_ref[0, 1:2, :], mod_ref[0, 0:1, :]).astype(BF16)

    zc = jnp.dot(h, wc_ref[...], preferred_element_type=F32)
    u = zc[:, :CONV_WIDTH] * _sigmoid(zc[:, CONV_WIDTH:])
    ubuf[halo:halo + ts, :] = u
    first = halo - (CONV_KERNEL - 1)
    acc = None
    for r in range(SUBLANES):
        part = None
        for j in range(CONV_KERNEL):
            if (first + j) % SUBLANES != r:
                continue
            base = first + j - r
            term = cw_ref[j:j + 1, :] * ubuf[base:base + ts + SUBLANES, :]
            part = term if part is None else part + term
        part = part[r:r + ts, :]
        acc = part if acc is None else acc + part
    ubuf[0:halo, :] = ubuf[ts:ts + halo, :]
    acc = acc + cb_ref[...]
    mu = jnp.mean(acc, axis=-1, keepdims=True)
    cen = acc - mu
    var = jnp.mean(cen * cen, axis=-1, keepdims=True)
    yn = cen * lax.rsqrt(var + EPS) * lng_ref[...] + lnb_ref[...]
    yn = yn * _sigmoid(yn)
    y_conv = jnp.dot(yn.astype(BF16), proj_ref[...], preferred_element_type=F32)

    zs = jnp.dot(y_ref[0], glu_ref[...], preferred_element_type=F32)
    y_ssm = zs[:, :D_MODEL] * _sigmoid(zs[:, D_MODEL:])

    def gate(w_ref):
        return _sigmoid(jnp.dot(h, w_ref[...], preferred_element_type=F32))

    merged = jnp.concatenate(
        [gate(wgc0_ref) * y_conv[:, :half] + gate(wgs0_ref) * y_ssm[:, :half],
         gate(wgc1_ref) * y_conv[:, half:] + gate(wgs1_ref) * y_ssm[:, half:]], axis=1)
    upd = jnp.dot(merged.astype(BF16), wout_ref[...], preferred_element_type=F32)
    o_ref[0] = x + mod_ref[0, 2:3, :] * upd


def _const_spec(shape, index=None):
    index = (0,) * len(shape) if index is None else index
    return pl.BlockSpec(shape, lambda b, s: index, pipeline_mode=pl.Buffered(1))


def _mixer_call(x, y, mod, g, w_in_b, cw, cb, lng, lnb, proj, glu, wout, ts):
    bsz, seq, _ = x.shape
    gate0 = (2 * CONV_WIDTH + SSM_WIDTH) // GATE_BLOCK
    gate_spec = lambda i: _const_spec((D_MODEL, GATE_BLOCK), (0, gate0 + i))
    return pl.pallas_call(
        _mixer_kernel,
        grid=(bsz, seq // ts),
        in_specs=[
            pl.BlockSpec((1, ts, D_MODEL), lambda b, s: (b, s, 0)),
            pl.BlockSpec((1, ts, SSM_WIDTH), lambda b, s: (b, s, 0)),
            pl.BlockSpec((1, 6, D_MODEL), lambda b, s: (b, 0, 0)),
            _const_spec((1, D_MODEL)),
            _const_spec((D_MODEL, 2 * CONV_WIDTH)),
            gate_spec(0), gate_spec(1), gate_spec(2), gate_spec(3),
            _const_spec((CONV_KERNEL, CONV_WIDTH)),
            _const_spec((1, CONV_WIDTH)),
            _const_spec((1, CONV_WIDTH)),
            _const_spec((1, CONV_WIDTH)),
            _const_spec((CONV_WIDTH, D_MODEL)),
            _const_spec((SSM_WIDTH, 2 * D_MODEL)),
            _const_spec((D_MODEL, D_MODEL)),
        ],
        out_specs=pl.BlockSpec((1, ts, D_MODEL), lambda b, s: (b, s, 0)),
        out_shape=jax.ShapeDtypeStruct(x.shape, F32),
        scratch_shapes=[pltpu.VMEM((ts + CONV_HALO + SUBLANES, CONV_WIDTH), F32)],
        compiler_params=pltpu.CompilerParams(
            dimension_semantics=("arbitrary", "arbitrary"),
            vmem_limit_bytes=VMEM_LIMIT),
        name="mixer",
    )(x, y, mod, g, w_in_b, w_in_b, w_in_b, w_in_b, w_in_b, cw, cb, lng, lnb, proj, glu,
      wout)


def _ffn_kernel(x_ref, mod_ref, g_ref, wg_ref, wu_ref, wo_ref, fg_ref, o_ref):
    x = x_ref[0]
    h = _rms_mod(x, g_ref[...], mod_ref[0, 4:5, :], mod_ref[0, 3:4, :]).astype(BF16)
    fg = jnp.dot(h, wg_ref[...], preferred_element_type=F32)
    fu = jnp.dot(h, wu_ref[...], preferred_element_type=F32)
    act = (fg * _sigmoid(fg) * fu).astype(BF16)
    upd = jnp.dot(act, wo_ref[...], preferred_element_type=F32)
    x2 = x + mod_ref[0, 5:6, :] * upd
    ms = jnp.mean(x2 * x2, axis=-1, keepdims=True)
    o_ref[0] = x2 * lax.rsqrt(ms + EPS) * fg_ref[...]


def _ffn_call(x, mod, g, w_ffn_in_b, wo, final_g, ts):
    bsz, seq, _ = x.shape
    return pl.pallas_call(
        _ffn_kernel,
        grid=(bsz, seq // ts),
        in_specs=[
            pl.BlockSpec((1, ts, D_MODEL), lambda b, s: (b, s, 0)),
            pl.BlockSpec((1, 6, D_MODEL), lambda b, s: (b, 0, 0)),
            _const_spec((1, D_MODEL)),
            _const_spec((D_MODEL, FFN_HIDDEN), (0, 0)),
            _const_spec((D_MODEL, FFN_HIDDEN), (0, 1)),
            _const_spec((FFN_HIDDEN, D_MODEL)),
            _const_spec((1, D_MODEL)),
        ],
        out_specs=pl.BlockSpec((1, ts, D_MODEL), lambda b, s: (b, s, 0)),
        out_shape=jax.ShapeDtypeStruct(x.shape, F32),
        compiler_params=pltpu.CompilerParams(
            dimension_semantics=("arbitrary", "arbitrary"),
            vmem_limit_bytes=VMEM_LIMIT),
        name="ffn",
    )(x, mod, g, w_ffn_in_b, w_ffn_in_b, wo, final_g)


def kernel(x, c, w_ada, b_ada, norm1_g, w_in, conv_w, conv_b, conv_ln_g, conv_ln_b,
           conv_proj, ssm_a_re, ssm_a_im, ssm_b_re, ssm_b_im, ssm_c_re, ssm_c_im,
           ssm_d, ssm_log_dt, ssm_glu, w_out, norm2_g, w_ffn_in, w_ffn_out, final_g):
    bsz, seq, _ = x.shape
    assert w_ada.shape[0] == 1, "single layer only"

    mod = _mod_call(c, w_ada[0], b_ada[0])
    mod = jnp.transpose(mod, (1, 0, 2))

    w_in_b = w_in[0].astype(BF16)
    g1 = norm1_g[0].reshape(1, -1)

    u_t = _ssm_in_call(x, mod, g1, w_in_b, 128)
    ops = _ssm_prep_call(ssm_a_re[0], ssm_a_im[0], ssm_b_re[0], ssm_b_im[0],
                         ssm_c_re[0], ssm_c_im[0], ssm_d[0], ssm_log_dt[0])
    y = _ssm_call(u_t, *ops)

    x1 = _mixer_call(
        x, y, mod, g1, w_in_b, conv_w[0], conv_b[0].reshape(1, -1),
        conv_ln_g[0].reshape(1, -1), conv_ln_b[0].reshape(1, -1),
        conv_proj[0].astype(BF16), ssm_glu[0].astype(BF16), w_out[0].astype(BF16), 512)

    return _ffn_call(x1, mod, norm2_g[0].reshape(1, -1), w_ffn_in[0].astype(BF16),
                     w_ffn_out[0].astype(BF16), final_g.reshape(1, -1), 512)
```

```python
import jax
import jax.numpy as jnp
from jax import lax
from jax.experimental import pallas as pl
from jax.experimental.pallas import tpu as pltpu

F32 = jnp.float32
BF16 = jnp.bfloat16

D_MODEL = 1024
CONV_WIDTH = 512
CONV_KERNEL = 31
SSM_WIDTH = 512
SSM_GROUP = 16
SSM_GROUPS = 32
SSM_STATE = 64
FFN_HIDDEN = 2816
EPS = 1e-6

CHUNK = 16
PAIR_CH = 2 * SSM_GROUP
PAIR_ST = 2 * SSM_STATE
PAIR_ROWS = CHUNK * PAIR_CH
N_PAIRS = SSM_GROUPS // 2
PAIRS_PER_STEP = 4
CONV_HALO = 32
GATE_BLOCK = 512
LANES = 128
SUBLANES = 8
CONV_ROWS = 64
N_SIDE = 8

VMEM_LIMIT = 56 * 1024 * 1024


def _rms_mod(x, g, scale, shift):
    ms = jnp.mean(x * x, axis=-1, keepdims=True)
    y = x * lax.rsqrt(ms + EPS)
    return (y * g) * (1.0 + scale) + shift


def _sigmoid(v):
    return jax.nn.sigmoid(v)


def _mod_kernel(c_ref, w_ref, b_ref, o_ref):
    c = c_ref[...]
    ca = c * _sigmoid(c)
    o_ref[0] = jnp.dot(ca, w_ref[...], preferred_element_type=F32,
                       precision=lax.Precision.HIGHEST) + b_ref[...]


def _mod_call(c, w_ada, b_ada):
    bsz = c.shape[0]
    n_mod = w_ada.shape[1] // D_MODEL
    return pl.pallas_call(
        _mod_kernel,
        grid=(n_mod,),
        in_specs=[
            pl.BlockSpec((bsz, D_MODEL), lambda j: (0, 0)),
            pl.BlockSpec((D_MODEL, D_MODEL), lambda j: (0, j)),
            pl.BlockSpec((1, D_MODEL), lambda j: (0, j)),
        ],
        out_specs=pl.BlockSpec((1, bsz, D_MODEL), lambda j: (j, 0, 0)),
        out_shape=jax.ShapeDtypeStruct((n_mod, bsz, D_MODEL), F32),
        name="mod",
    )(c, w_ada, b_ada.reshape(1, -1))


def _ssm_prep_kernel(ar_ref, ai_ref, ldt_ref, cre_ref, cim_ref, bre_ref, bim_ref,
                     dl_ref, kt_ref, bst_ref, cst_ref, lre_ref, lim_ref, cl_s, lb_s):
    ar, ai = ar_ref[0], ai_ref[0]
    dt = jnp.exp(ldt_ref[0])
    dar, dai = dt * ar, dt * ai
    mag, cos1, sin1 = jnp.exp(dar), jnp.cos(dai), jnp.sin(dai)
    n_re, n_im = mag * cos1 - 1.0, mag * sin1
    den = ar * ar + ai * ai
    q_re = (n_re * ar + n_im * ai) / den
    q_im = (n_im * ar - n_re * ai) / den

    row = lax.broadcasted_iota(jnp.int32, (PAIR_CH, PAIR_ST), 0)
    lane = lax.broadcasted_iota(jnp.int32, (PAIR_CH, PAIR_ST), 1)
    same_group = (row // SSM_GROUP) == (lane // SSM_STATE)

    def block_diag(ref):
        v = ref[0]
        return jnp.where(same_group, jnp.concatenate([v, v], axis=1), 0.0)

    c_re, c_im = block_diag(cre_ref), block_diag(cim_ref)
    b_re, b_im = block_diag(bre_ref), block_diag(bim_ref)
    bb_re = q_re * b_re - q_im * b_im
    bb_im = q_re * b_im + q_im * b_re

    inv_mag = jnp.exp(-dar)
    one = jnp.ones_like(dar)
    powers = {0: (one, jnp.zeros_like(dar)), 1: (mag * cos1, mag * sin1),
              -1: (inv_mag * cos1, -(inv_mag * sin1))}

    def power(k):
        if k not in powers:
            step = 1 if k > 0 else -1
            (p_re, p_im), (s_re, s_im) = power(k - step), powers[step]
            powers[k] = (p_re * s_re - p_im * s_im, p_re * s_im + p_im * s_re)
        return powers[k]

    def cmul_cat(x_re, x_im, p, conj_out):
        p_re, p_im = p
        o_re = x_re * p_re - x_im * p_im
        o_im = x_re * p_im + x_im * p_re
        return jnp.concatenate([o_re, -o_im if conj_out else o_im], axis=1)

    for t in range(CHUNK):
        rs = slice(t * PAIR_CH, (t + 1) * PAIR_CH)
        cl_s[rs, :] = cmul_cat(c_re, c_im, power(t), True)
        lb_s[rs, :] = cmul_cat(bb_re, bb_im, power(-t), False)
        cst_ref[0, rs, :] = cmul_cat(c_re, c_im, power(t + 1), True).astype(BF16)
        bst_ref[0, rs, :] = cmul_cat(bb_re, bb_im, power(CHUNK - 1 - t),
                                     False).astype(BF16)
    lre_ref[0], lim_ref[0] = power(CHUNK)

    k = lax.dot_general(cl_s[...], lb_s[...], (((1,), (1,)), ((), ())),
                        preferred_element_type=F32, precision=lax.Precision.HIGHEST)
    ri = lax.broadcasted_iota(jnp.int32, (PAIR_ROWS, PAIR_ROWS), 0)
    ci = lax.broadcasted_iota(jnp.int32, (PAIR_ROWS, PAIR_ROWS), 1)
    k = jnp.where(ri // PAIR_CH >= ci // PAIR_CH, k, 0.0)
    k = k + jnp.where(ri == ci, dl_ref[0], 0.0)
    kt_ref[0] = k.astype(BF16)


def _ssm_prep_call(a_re, a_im, b_re, b_im, c_re, c_im, d, log_dt):
    np_, pc, ps, st = N_PAIRS, PAIR_CH, PAIR_ST, SSM_STATE
    row = lambda a: a.reshape(np_, 1, ps)
    ldt = jnp.broadcast_to(log_dt[:, None], (SSM_GROUPS, st))
    d_lane = jnp.tile(d.reshape(np_, 1, pc), (1, 1, CHUNK))
    bt = lambda a: jnp.transpose(a, (0, 2, 1)).reshape(np_, pc, st)
    spec = lambda *shape: pl.BlockSpec((1,) + shape, lambda p: (p, 0, 0))
    return pl.pallas_call(
        _ssm_prep_kernel,
        grid=(np_,),
        in_specs=[spec(1, ps), spec(1, ps), spec(1, ps), spec(pc, st), spec(pc, st),
                  spec(pc, st), spec(pc, st), spec(1, PAIR_ROWS)],
        out_specs=[spec(PAIR_ROWS, PAIR_ROWS), spec(PAIR_ROWS, 2 * ps),
                   spec(PAIR_ROWS, 2 * ps), spec(1, ps), spec(1, ps)],
        out_shape=[jax.ShapeDtypeStruct((np_, PAIR_ROWS, PAIR_ROWS), BF16),
                   jax.ShapeDtypeStruct((np_, PAIR_ROWS, 2 * ps), BF16),
                   jax.ShapeDtypeStruct((np_, PAIR_ROWS, 2 * ps), BF16),
                   jax.ShapeDtypeStruct((np_, 1, ps), F32),
                   jax.ShapeDtypeStruct((np_, 1, ps), F32)],
        scratch_shapes=[pltpu.VMEM((PAIR_ROWS, 2 * ps), F32)] * 2,
        name="ssm_prep",
    )(row(a_re), row(a_im), row(ldt), c_re.reshape(np_, pc, st),
      c_im.reshape(np_, pc, st), bt(b_re), bt(b_im), d_lane)


def _ssm_in_kernel(x_ref, mod_ref, g_ref, w_ref, o_ref, us):
    n_chunks = o_ref.shape[-1]
    h = _rms_mod(x_ref[0], g_ref[...], mod_ref[0, 1:2, :], mod_ref[0, 0:1, :])
    u = jnp.dot(h.astype(BF16), w_ref[...], preferred_element_type=F32)
    for k in range(SSM_WIDTH // LANES):
        us[k] = u[:, k * LANES:(k + 1) * LANES]
    for tau in range(CHUNK):
        for k in range(SSM_WIDTH // LANES):
            piece = us[k, pl.ds(tau, n_chunks, stride=CHUNK), :]
            o_ref[0, tau, k * LANES:(k + 1) * LANES, :] = piece.T.astype(o_ref.dtype)


def _ssm_in_call(x, mod, g, w_in_b, chunks_per_step):
    bsz, seq, _ = x.shape
    ts = chunks_per_step * CHUNK
    ssm_col_block = 2 * CONV_WIDTH // SSM_WIDTH
    return pl.pallas_call(
        _ssm_in_kernel,
        grid=(bsz, seq // ts),
        in_specs=[
            pl.BlockSpec((1, ts, D_MODEL), lambda b, s: (b, s, 0)),
            pl.BlockSpec((1, 6, D_MODEL), lambda b, s: (b, 0, 0)),
            pl.BlockSpec((1, D_MODEL), lambda b, s: (0, 0)),
            pl.BlockSpec((D_MODEL, SSM_WIDTH), lambda b, s: (0, ssm_col_block)),
        ],
        out_specs=pl.BlockSpec((1, CHUNK, SSM_WIDTH, chunks_per_step),
                               lambda b, s: (b, 0, 0, s)),
        out_shape=jax.ShapeDtypeStruct((bsz, CHUNK, SSM_WIDTH, seq // CHUNK), BF16),
        scratch_shapes=[pltpu.VMEM((SSM_WIDTH // LANES, ts, LANES), F32)],
        compiler_params=pltpu.CompilerParams(
            dimension_semantics=("arbitrary", "arbitrary"),
            vmem_limit_bytes=VMEM_LIMIT),
        name="ssm_in",
    )(x, mod, g, w_in_b)


def _ssm_kernel(u_ref, kt_ref, bst_ref, cst_ref, lre_ref, lim_ref, o_ref,
                sre, sim, xre, xim, ys, yn):
    n_chunks = u_ref.shape[-1]
    pw = PAIR_ST

    def load_pair(j):
        r = u_ref[0, :, j * PAIR_CH:(j + 1) * PAIR_CH, :]
        return r.reshape(PAIR_ROWS, n_chunks)

    for j in range(PAIRS_PER_STEP):
        s = lax.dot_general(load_pair(j), bst_ref[j], (((0,), (0,)), ((), ())),
                            preferred_element_type=F32)
        sre[:, j * pw:(j + 1) * pw] = s[:, :pw]
        sim[:, j * pw:(j + 1) * pw] = s[:, pw:]

    lre = lre_ref[0]
    lim = lim_ref[0]

    def body(c, carry):
        xr, xi = carry
        xre[pl.ds(c, 1), :] = xr
        xim[pl.ds(c, 1), :] = xi
        nr = lre * xr - lim * xi + sre[pl.ds(c, 1), :]
        ni = lre * xi + lim * xr + sim[pl.ds(c, 1), :]
        return nr, ni

    zero = jnp.zeros((1, PAIRS_PER_STEP * pw), F32)
    lax.fori_loop(0, n_chunks, body, (zero, zero), unroll=8)

    for j in range(PAIRS_PER_STEP):
        y_intra = jnp.dot(kt_ref[j], load_pair(j), preferred_element_type=F32)
        xp = jnp.concatenate([xre[:, j * pw:(j + 1) * pw],
                              xim[:, j * pw:(j + 1) * pw]], axis=1).astype(BF16)
        y_state = lax.dot_general(cst_ref[j], xp, (((1,), (1,)), ((), ())),
                                  preferred_element_type=F32)
        y = jax.nn.gelu(y_intra + y_state)
        ys[:, j * PAIR_CH:(j + 1) * PAIR_CH, :] = y.reshape(CHUNK, PAIR_CH, n_chunks)

    for t in range(CHUNK):
        yn[pl.ds(t, n_chunks, stride=CHUNK), :] = ys[t].T
    o_ref[0] = yn[...].astype(o_ref.dtype)


def _ssm_call(u_t, kt, bst, cst, lam_re, lam_im):
    bsz, _, _, n_chunks = u_t.shape
    seq = n_chunks * CHUNK
    pps = PAIRS_PER_STEP
    ch = pps * PAIR_CH
    state_w = pps * PAIR_ST
    steps = N_PAIRS // pps
    lam_re = lam_re.reshape(steps, 1, state_w)
    lam_im = lam_im.reshape(steps, 1, state_w)
    return pl.pallas_call(
        _ssm_kernel,
        grid=(steps, bsz),
        in_specs=[
            pl.BlockSpec((1, CHUNK, ch, n_chunks), lambda p, b: (b, 0, p, 0)),
            pl.BlockSpec((pps, PAIR_ROWS, PAIR_ROWS), lambda p, b: (p, 0, 0)),
            pl.BlockSpec((pps, PAIR_ROWS, 2 * PAIR_ST), lambda p, b: (p, 0, 0)),
            pl.BlockSpec((pps, PAIR_ROWS, 2 * PAIR_ST), lambda p, b: (p, 0, 0)),
            pl.BlockSpec((1, 1, state_w), lambda p, b: (p, 0, 0)),
            pl.BlockSpec((1, 1, state_w), lambda p, b: (p, 0, 0)),
        ],
        out_specs=pl.BlockSpec((1, seq, ch), lambda p, b: (b, 0, p)),
        out_shape=jax.ShapeDtypeStruct((bsz, seq, SSM_WIDTH), BF16),
        scratch_shapes=[pltpu.VMEM((n_chunks, state_w), F32)] * 4 + [
            pltpu.VMEM((CHUNK, ch, n_chunks), F32), pltpu.VMEM((seq, ch), F32)],
        compiler_params=pltpu.CompilerParams(
            dimension_semantics=("arbitrary", "arbitrary"),
            vmem_limit_bytes=VMEM_LIMIT),
        name="ssm",
    )(u_t, kt, bst, cst, lam_re, lam_im)


def _mixer_kernel(x_ref, y_ref, mod_ref, g_ref, wc_ref, wgc0_ref, wgc1_ref, wgs0_ref,
                  wgs1_ref, cw_ref, cb_ref, lng_ref, lnb_ref, proj_ref, glu_ref,
                  wout_ref, o_ref, ubuf, cbuf, sbuf):
    ts = x_ref.shape[1]
    halo = CONV_HALO
    half = D_MODEL // 2
    n_lt = CONV_WIDTH // LANES

    @pl.when(pl.program_id(1) == 0)
    def _():
        ubuf[:, 0:halo, :] = jnp.zeros((n_lt, halo, LANES), F32)

    x = x_ref[0]
    h = _rms_mod(x, g_ref[...], mod_ref[0, 1:2, :], mod_ref[0, 0:1, :]).astype(BF16)

    zc = jnp.dot(h, wc_ref[...], preferred_element_type=F32)
    u = zc[:, :CONV_WIDTH] * _sigmoid(zc[:, CONV_WIDTH:])
    for k in range(n_lt):
        ubuf[k, halo:halo + ts, :] = u[:, k * LANES:(k + 1) * LANES]

    y_in = y_ref[0]

    def dot_h(w_ref):
        return jnp.dot(h, w_ref[...], preferred_element_type=F32)

    def dot_y(k):
        return jnp.dot(y_in, glu_ref[:, k * half:(k + 1) * half],
                       preferred_element_type=F32)

    gc0, gs0, za0, zb0, gc1, gs1, za1, zb1 = range(N_SIDE)

    def put(k, v):
        sbuf[k] = v

    dyn_zero = jnp.minimum(pl.program_id(1), 0)
    never = lax.broadcasted_iota(jnp.int32, (SUBLANES, LANES), 0) < dyn_zero

    def anchored(term, k):
        fence = sbuf[dyn_zero, k * SUBLANES:(k + 1) * SUBLANES, 0:LANES]
        head = jnp.where(never, fence, term[0:SUBLANES, :])
        return jnp.concatenate([head, term[SUBLANES:, :]], axis=0)

    side_jobs = [
        lambda: put(gc0, dot_h(wgc0_ref)), lambda: put(gs0, dot_h(wgs0_ref)),
        lambda: put(za0, dot_y(0)), lambda: put(zb0, dot_y(2)),
        lambda: put(gc1, dot_h(wgc1_ref)), lambda: put(gs1, dot_h(wgs1_ref)),
        lambda: put(za1, dot_y(1)), lambda: put(zb1, dot_y(3)),
    ]
    first = halo - (CONV_KERNEL - 1)
    n_rt = ts // CONV_ROWS
    for i in range(n_rt):
        for k in range(n_lt):
            lanes = slice(k * LANES, (k + 1) * LANES)
            acc = None
            for j in range(CONV_KERNEL):
                row0 = first + j + i * CONV_ROWS
                term = cw_ref[j:j + 1, lanes] * ubuf[k, row0:row0 + CONV_ROWS, :]
                if j == 0 and i > 0:
                    term = anchored(term, k)
                acc = term if acc is None else acc + term
            cbuf[i * CONV_ROWS:(i + 1) * CONV_ROWS, lanes] = acc + cb_ref[:, lanes]
        if (i + 1) * len(side_jobs) % n_rt == 0:
            side_jobs[(i + 1) * len(side_jobs) // n_rt - 1]()
    for k in range(n_lt):
        ubuf[k, 0:halo, :] = ubuf[k, ts:ts + halo, :]
    acc = cbuf[...]
    mu = jnp.mean(acc, axis=-1, keepdims=True)
    cen = acc - mu
    var = jnp.mean(cen * cen, axis=-1, keepdims=True)
    yn = cen * lax.rsqrt(var + EPS) * lng_ref[...] + lnb_ref[...]
    yn = yn * _sigmoid(yn)
    y_conv = jnp.dot(yn.astype(BF16), proj_ref[...], preferred_element_type=F32)

    def merge(conv_half, g_conv, g_ssm, z_a, z_b):
        y_ssm = sbuf[z_a] * _sigmoid(sbuf[z_b])
        return _sigmoid(sbuf[g_conv]) * conv_half + _sigmoid(sbuf[g_ssm]) * y_ssm

    merged = jnp.concatenate(
        [merge(y_conv[:, :half], gc0, gs0, za0, zb0),
         merge(y_conv[:, half:], gc1, gs1, za1, zb1)], axis=1)
    upd = jnp.dot(merged.astype(BF16), wout_ref[...], preferred_element_type=F32)
    o_ref[0] = x + mod_ref[0, 2:3, :] * upd


def _const_spec(shape, index=None):
    index = (0,) * len(shape) if index is None else index
    return pl.BlockSpec(shape, lambda b, s: index, pipeline_mode=pl.Buffered(1))


def _mixer_call(x, y, mod, g, w_in_b, cw, cb, lng, lnb, proj, glu, wout, ts):
    bsz, seq, _ = x.shape
    gate0 = (2 * CONV_WIDTH + SSM_WIDTH) // GATE_BLOCK
    gate_spec = lambda i: _const_spec((D_MODEL, GATE_BLOCK), (0, gate0 + i))
    return pl.pallas_call(
        _mixer_kernel,
        grid=(bsz, seq // ts),
        in_specs=[
            pl.BlockSpec((1, ts, D_MODEL), lambda b, s: (b, s, 0)),
            pl.BlockSpec((1, ts, SSM_WIDTH), lambda b, s: (b, s, 0)),
            pl.BlockSpec((1, 6, D_MODEL), lambda b, s: (b, 0, 0)),
            _const_spec((1, D_MODEL)),
            _const_spec((D_MODEL, 2 * CONV_WIDTH)),
            gate_spec(0), gate_spec(1), gate_spec(2), gate_spec(3),
            _const_spec((CONV_KERNEL, CONV_WIDTH)),
            _const_spec((1, CONV_WIDTH)),
            _const_spec((1, CONV_WIDTH)),
            _const_spec((1, CONV_WIDTH)),
            _const_spec((CONV_WIDTH, D_MODEL)),
            _const_spec((SSM_WIDTH, 2 * D_MODEL)),
            _const_spec((D_MODEL, D_MODEL)),
        ],
        out_specs=pl.BlockSpec((1, ts, D_MODEL), lambda b, s: (b, s, 0)),
        out_shape=jax.ShapeDtypeStruct(x.shape, F32),
        scratch_shapes=[pltpu.VMEM((CONV_WIDTH // LANES, ts + CONV_HALO, LANES), F32),
                        pltpu.VMEM((ts, CONV_WIDTH), F32),
                        pltpu.VMEM((N_SIDE, ts, D_MODEL // 2), F32)],
        compiler_params=pltpu.CompilerParams(
            dimension_semantics=("arbitrary", "arbitrary"),
            vmem_limit_bytes=VMEM_LIMIT),
        name="mixer",
    )(x, y, mod, g, w_in_b, w_in_b, w_in_b, w_in_b, w_in_b, cw, cb, lng, lnb, proj, glu,
      wout)


def _ffn_kernel(x_ref, mod_ref, g_ref, wg_ref, wu_ref, wo_ref, fg_ref, o_ref):
    x = x_ref[0]
    h = _rms_mod(x, g_ref[...], mod_ref[0, 4:5, :], mod_ref[0, 3:4, :]).astype(BF16)
    fg = jnp.dot(h, wg_ref[...], preferred_element_type=F32)
    fu = jnp.dot(h, wu_ref[...], preferred_element_type=F32)
    act = (fg * _sigmoid(fg) * fu).astype(BF16)
    upd = jnp.dot(act, wo_ref[...], preferred_element_type=F32)
    x2 = x + mod_ref[0, 5:6, :] * upd
    ms = jnp.mean(x2 * x2, axis=-1, keepdims=True)
    o_ref[0] = x2 * lax.rsqrt(ms + EPS) * fg_ref[...]


def _ffn_call(x, mod, g, w_ffn_in_b, wo, final_g, ts):
    bsz, seq, _ = x.shape
    return pl.pallas_call(
        _ffn_kernel,
        grid=(bsz, seq // ts),
        in_specs=[
            pl.BlockSpec((1, ts, D_MODEL), lambda b, s: (b, s, 0)),
            pl.BlockSpec((1, 6, D_MODEL), lambda b, s: (b, 0, 0)),
            _const_spec((1, D_MODEL)),
            _const_spec((D_MODEL, FFN_HIDDEN), (0, 0)),
            _const_spec((D_MODEL, FFN_HIDDEN), (0, 1)),
            _const_spec((FFN_HIDDEN, D_MODEL)),
            _const_spec((1, D_MODEL)),
        ],
        out_specs=pl.BlockSpec((1, ts, D_MODEL), lambda b, s: (b, s, 0)),
        out_shape=jax.ShapeDtypeStruct(x.shape, F32),
        compiler_params=pltpu.CompilerParams(
            dimension_semantics=("arbitrary", "arbitrary"),
            vmem_limit_bytes=VMEM_LIMIT),
        name="ffn",
    )(x, mod, g, w_ffn_in_b, w_ffn_in_b, wo, final_g)


def kernel(x, c, w_ada, b_ada, norm1_g, w_in, conv_w, conv_b, conv_ln_g, conv_ln_b,
           conv_proj, ssm_a_re, ssm_a_im, ssm_b_re, ssm_b_im, ssm_c_re, ssm_c_im,
           ssm_d, ssm_log_dt, ssm_glu, w_out, norm2_g, w_ffn_in, w_ffn_out, final_g):
    bsz, seq, _ = x.shape
    assert w_ada.shape[0] == 1, "single layer only"

    mod = _mod_call(c, w_ada[0], b_ada[0])
    mod = jnp.transpose(mod, (1, 0, 2))

    w_in_b = w_in[0].astype(BF16)
    g1 = norm1_g[0].reshape(1, -1)

    u_t = _ssm_in_call(x, mod, g1, w_in_b, 128)
    ops = _ssm_prep_call(ssm_a_re[0], ssm_a_im[0], ssm_b_re[0], ssm_b_im[0],
                         ssm_c_re[0], ssm_c_im[0], ssm_d[0], ssm_log_dt[0])
    y = _ssm_call(u_t, *ops)

    x1 = _mixer_call(
        x, y, mod, g1, w_in_b, conv_w[0], conv_b[0].reshape(1, -1),
        conv_ln_g[0].reshape(1, -1), conv_ln_b[0].reshape(1, -1),
        conv_proj[0].astype(BF16), ssm_glu[0].astype(BF16), w_out[0].astype(BF16), 512)

    return _ffn_call(x1, mod, norm2_g[0].reshape(1, -1), w_ffn_in[0].astype(BF16),
                     w_ffn_out[0].astype(BF16), final_g.reshape(1, -1), 512)
```

```python
import jax
import jax.numpy as jnp
from jax import lax
from jax.experimental import pallas as pl
from jax.experimental.pallas import tpu as pltpu

F32 = jnp.float32
BF16 = jnp.bfloat16

D_MODEL = 1024
CONV_WIDTH = 512
CONV_KERNEL = 31
SSM_WIDTH = 512
SSM_GROUP = 16
SSM_GROUPS = 32
SSM_STATE = 64
FFN_HIDDEN = 2816
EPS = 1e-6

CHUNK = 16
PAIR_CH = 2 * SSM_GROUP
PAIR_ST = 2 * SSM_STATE
PAIR_ROWS = CHUNK * PAIR_CH
N_PAIRS = SSM_GROUPS // 2
PAIRS_PER_STEP = 4
CONV_HALO = 32
GATE_BLOCK = 512
LANES = 128
SUBLANES = 8
CONV_ROWS = 64
MIX_TRIPS = 2

VMEM_LIMIT = 56 * 1024 * 1024


def _rms_mod(x, g, scale, shift):
    ms = jnp.mean(x * x, axis=-1, keepdims=True)
    y = x * lax.rsqrt(ms + EPS)
    return (y * g) * (1.0 + scale) + shift


def _sigmoid(v):
    return 0.5 * jnp.tanh(0.5 * v) + 0.5


def _mod_kernel(c_ref, w_ref, b_ref, o_ref):
    c = c_ref[...]
    ca = c * _sigmoid(c)
    o_ref[0] = jnp.dot(ca, w_ref[...], preferred_element_type=F32,
                       precision=lax.Precision.HIGHEST) + b_ref[...]


def _mod_call(c, w_ada, b_ada):
    bsz = c.shape[0]
    n_mod = w_ada.shape[1] // D_MODEL
    return pl.pallas_call(
        _mod_kernel,
        grid=(n_mod,),
        in_specs=[
            pl.BlockSpec((bsz, D_MODEL), lambda j: (0, 0)),
            pl.BlockSpec((D_MODEL, D_MODEL), lambda j: (0, j)),
            pl.BlockSpec((1, D_MODEL), lambda j: (0, j)),
        ],
        out_specs=pl.BlockSpec((1, bsz, D_MODEL), lambda j: (j, 0, 0)),
        out_shape=jax.ShapeDtypeStruct((n_mod, bsz, D_MODEL), F32),
        name="mod",
    )(c, w_ada, b_ada.reshape(1, -1))


def _ssm_prep_kernel(ar_ref, ai_ref, ldt_ref, cre_ref, cim_ref, bre_ref, bim_ref,
                     dl_ref, kt_ref, bst_ref, cst_ref, lre_ref, lim_ref, cl_s, lb_s):
    ar, ai = ar_ref[0], ai_ref[0]
    dt = jnp.exp(ldt_ref[0])
    dar, dai = dt * ar, dt * ai
    mag, cos1, sin1 = jnp.exp(dar), jnp.cos(dai), jnp.sin(dai)
    n_re, n_im = mag * cos1 - 1.0, mag * sin1
    den = ar * ar + ai * ai
    q_re = (n_re * ar + n_im * ai) / den
    q_im = (n_im * ar - n_re * ai) / den

    row = lax.broadcasted_iota(jnp.int32, (PAIR_CH, PAIR_ST), 0)
    lane = lax.broadcasted_iota(jnp.int32, (PAIR_CH, PAIR_ST), 1)
    same_group = (row // SSM_GROUP) == (lane // SSM_STATE)

    def block_diag(ref):
        v = ref[0]
        return jnp.where(same_group, jnp.concatenate([v, v], axis=1), 0.0)

    c_re, c_im = block_diag(cre_ref), block_diag(cim_ref)
    b_re, b_im = block_diag(bre_ref), block_diag(bim_ref)
    bb_re = q_re * b_re - q_im * b_im
    bb_im = q_re * b_im + q_im * b_re

    inv_mag = jnp.exp(-dar)
    one = jnp.ones_like(dar)
    powers = {0: (one, jnp.zeros_like(dar)), 1: (mag * cos1, mag * sin1),
              -1: (inv_mag * cos1, -(inv_mag * sin1))}

    def power(k):
        if k not in powers:
            step = 1 if k > 0 else -1
            (p_re, p_im), (s_re, s_im) = power(k - step), powers[step]
            powers[k] = (p_re * s_re - p_im * s_im, p_re * s_im + p_im * s_re)
        return powers[k]

    def cmul_cat(x_re, x_im, p, conj_out):
        p_re, p_im = p
        o_re = x_re * p_re - x_im * p_im
        o_im = x_re * p_im + x_im * p_re
        return jnp.concatenate([o_re, -o_im if conj_out else o_im], axis=1)

    for t in range(CHUNK):
        rs = slice(t * PAIR_CH, (t + 1) * PAIR_CH)
        cl_s[rs, :] = cmul_cat(c_re, c_im, power(t), True)
        lb_s[rs, :] = cmul_cat(bb_re, bb_im, power(-t), False)
        cst_ref[0, rs, :] = cmul_cat(c_re, c_im, power(t + 1), True).astype(BF16)
        bst_ref[0, rs, :] = cmul_cat(bb_re, bb_im, power(CHUNK - 1 - t),
                                     False).astype(BF16)
    lre_ref[0], lim_ref[0] = power(CHUNK)

    k = lax.dot_general(cl_s[...], lb_s[...], (((1,), (1,)), ((), ())),
                        preferred_element_type=F32, precision=lax.Precision.HIGHEST)
    ri = lax.broadcasted_iota(jnp.int32, (PAIR_ROWS, PAIR_ROWS), 0)
    ci = lax.broadcasted_iota(jnp.int32, (PAIR_ROWS, PAIR_ROWS), 1)
    k = jnp.where(ri // PAIR_CH >= ci // PAIR_CH, k, 0.0)
    k = k + jnp.where(ri == ci, dl_ref[0], 0.0)
    kt_ref[0] = k.astype(BF16)


def _ssm_prep_call(a_re, a_im, b_re, b_im, c_re, c_im, d, log_dt):
    np_, pc, ps, st = N_PAIRS, PAIR_CH, PAIR_ST, SSM_STATE
    row = lambda a: a.reshape(np_, 1, ps)
    ldt = jnp.broadcast_to(log_dt[:, None], (SSM_GROUPS, st))
    d_lane = jnp.tile(d.reshape(np_, 1, pc), (1, 1, CHUNK))
    bt = lambda a: jnp.transpose(a, (0, 2, 1)).reshape(np_, pc, st)
    spec = lambda *shape: pl.BlockSpec((1,) + shape, lambda p: (p, 0, 0))
    return pl.pallas_call(
        _ssm_prep_kernel,
        grid=(np_,),
        in_specs=[spec(1, ps), spec(1, ps), spec(1, ps), spec(pc, st), spec(pc, st),
                  spec(pc, st), spec(pc, st), spec(1, PAIR_ROWS)],
        out_specs=[spec(PAIR_ROWS, PAIR_ROWS), spec(PAIR_ROWS, 2 * ps),
                   spec(PAIR_ROWS, 2 * ps), spec(1, ps), spec(1, ps)],
        out_shape=[jax.ShapeDtypeStruct((np_, PAIR_ROWS, PAIR_ROWS), BF16),
                   jax.ShapeDtypeStruct((np_, PAIR_ROWS, 2 * ps), BF16),
                   jax.ShapeDtypeStruct((np_, PAIR_ROWS, 2 * ps), BF16),
                   jax.ShapeDtypeStruct((np_, 1, ps), F32),
                   jax.ShapeDtypeStruct((np_, 1, ps), F32)],
        scratch_shapes=[pltpu.VMEM((PAIR_ROWS, 2 * ps), F32)] * 2,
        name="ssm_prep",
    )(row(a_re), row(a_im), row(ldt), c_re.reshape(np_, pc, st),
      c_im.reshape(np_, pc, st), bt(b_re), bt(b_im), d_lane)


def _ssm_in_kernel(x_ref, mod_ref, g_ref, w_ref, o_ref, us):
    n_chunks = o_ref.shape[-1]
    h = _rms_mod(x_ref[0], g_ref[...], mod_ref[0, 1:2, :], mod_ref[0, 0:1, :])
    u = jnp.dot(h.astype(BF16), w_ref[...], preferred_element_type=F32)
    for k in range(SSM_WIDTH // LANES):
        us[k] = u[:, k * LANES:(k + 1) * LANES]
    for tau in range(CHUNK):
        for k in range(SSM_WIDTH // LANES):
            piece = us[k, pl.ds(tau, n_chunks, stride=CHUNK), :]
            o_ref[0, tau, k * LANES:(k + 1) * LANES, :] = piece.T.astype(o_ref.dtype)


def _ssm_in_call(x, mod, g, w_in_b, chunks_per_step):
    bsz, seq, _ = x.shape
    ts = chunks_per_step * CHUNK
    ssm_col_block = 2 * CONV_WIDTH // SSM_WIDTH
    return pl.pallas_call(
        _ssm_in_kernel,
        grid=(bsz, seq // ts),
        in_specs=[
            pl.BlockSpec((1, ts, D_MODEL), lambda b, s: (b, s, 0)),
            pl.BlockSpec((1, 6, D_MODEL), lambda b, s: (b, 0, 0)),
            pl.BlockSpec((1, D_MODEL), lambda b, s: (0, 0)),
            pl.BlockSpec((D_MODEL, SSM_WIDTH), lambda b, s: (0, ssm_col_block)),
        ],
        out_specs=pl.BlockSpec((1, CHUNK, SSM_WIDTH, chunks_per_step),
                               lambda b, s: (b, 0, 0, s)),
        out_shape=jax.ShapeDtypeStruct((bsz, CHUNK, SSM_WIDTH, seq // CHUNK), BF16),
        scratch_shapes=[pltpu.VMEM((SSM_WIDTH // LANES, ts, LANES), F32)],
        compiler_params=pltpu.CompilerParams(
            dimension_semantics=("arbitrary", "arbitrary"),
            vmem_limit_bytes=VMEM_LIMIT),
        name="ssm_in",
    )(x, mod, g, w_in_b)


def _ssm_kernel(u_ref, kt_ref, bst_ref, cst_ref, lre_ref, lim_ref, o_ref,
                sre, sim, xre, xim, ys, yn):
    n_chunks = u_ref.shape[-1]
    pw = PAIR_ST

    def load_pair(j):
        r = u_ref[0, :, j * PAIR_CH:(j + 1) * PAIR_CH, :]
        return r.reshape(PAIR_ROWS, n_chunks)

    for j in range(PAIRS_PER_STEP):
        s = lax.dot_general(load_pair(j), bst_ref[j], (((0,), (0,)), ((), ())),
                            preferred_element_type=F32)
        sre[:, j * pw:(j + 1) * pw] = s[:, :pw]
        sim[:, j * pw:(j + 1) * pw] = s[:, pw:]

    lre = lre_ref[0]
    lim = lim_ref[0]

    def body(c, carry):
        xr, xi = carry
        xre[pl.ds(c, 1), :] = xr
        xim[pl.ds(c, 1), :] = xi
        nr = lre * xr - lim * xi + sre[pl.ds(c, 1), :]
        ni = lre * xi + lim * xr + sim[pl.ds(c, 1), :]
        return nr, ni

    zero = jnp.zeros((1, PAIRS_PER_STEP * pw), F32)
    lax.fori_loop(0, n_chunks, body, (zero, zero), unroll=8)

    for j in range(PAIRS_PER_STEP):
        y_intra = jnp.dot(kt_ref[j], load_pair(j), preferred_element_type=F32)
        xp = jnp.concatenate([xre[:, j * pw:(j + 1) * pw],
                              xim[:, j * pw:(j + 1) * pw]], axis=1).astype(BF16)
        y_state = lax.dot_general(cst_ref[j], xp, (((1,), (1,)), ((), ())),
                                  preferred_element_type=F32)
        y = jax.nn.gelu(y_intra + y_state)
        ys[:, j * PAIR_CH:(j + 1) * PAIR_CH, :] = y.reshape(CHUNK, PAIR_CH, n_chunks)

    for t in range(CHUNK):
        yn[pl.ds(t, n_chunks, stride=CHUNK), :] = ys[t].T
    o_ref[0] = yn[...].astype(o_ref.dtype)


def _ssm_call(u_t, kt, bst, cst, lam_re, lam_im):
    bsz, _, _, n_chunks = u_t.shape
    seq = n_chunks * CHUNK
    pps = PAIRS_PER_STEP
    ch = pps * PAIR_CH
    state_w = pps * PAIR_ST
    steps = N_PAIRS // pps
    lam_re = lam_re.reshape(steps, 1, state_w)
    lam_im = lam_im.reshape(steps, 1, state_w)
    return pl.pallas_call(
        _ssm_kernel,
        grid=(steps, bsz),
        in_specs=[
            pl.BlockSpec((1, CHUNK, ch, n_chunks), lambda p, b: (b, 0, p, 0)),
            pl.BlockSpec((pps, PAIR_ROWS, PAIR_ROWS), lambda p, b: (p, 0, 0)),
            pl.BlockSpec((pps, PAIR_ROWS, 2 * PAIR_ST), lambda p, b: (p, 0, 0)),
            pl.BlockSpec((pps, PAIR_ROWS, 2 * PAIR_ST), lambda p, b: (p, 0, 0)),
            pl.BlockSpec((1, 1, state_w), lambda p, b: (p, 0, 0)),
            pl.BlockSpec((1, 1, state_w), lambda p, b: (p, 0, 0)),
        ],
        out_specs=pl.BlockSpec((1, seq, ch), lambda p, b: (b, 0, p)),
        out_shape=jax.ShapeDtypeStruct((bsz, seq, SSM_WIDTH), BF16),
        scratch_shapes=[pltpu.VMEM((n_chunks, state_w), F32)] * 4 + [
            pltpu.VMEM((CHUNK, ch, n_chunks), F32), pltpu.VMEM((seq, ch), F32)],
        compiler_params=pltpu.CompilerParams(
            dimension_semantics=("arbitrary", "arbitrary"),
            vmem_limit_bytes=VMEM_LIMIT),
        name="ssm",
    )(u_t, kt, bst, cst, lam_re, lam_im)


def _mixer_kernel(x_ref, y_ref, mod_ref, g_ref, wc_ref, wg_ref, cw_ref, cb_ref,
                  lng_ref, lnb_ref, proj_ref, glu_ref, wout_ref, o_ref,
                  ubuf, cbuf, hbuf, gbuf, sbuf):
    ts = x_ref.shape[1]
    halo = CONV_HALO
    n_lt = CONV_WIDTH // LANES

    @pl.when(pl.program_id(1) == 0)
    def _():
        ubuf[:, 0:halo, :] = jnp.zeros((n_lt, halo, LANES), F32)

    yb = y_ref[0]
    z_a = jnp.dot(yb, glu_ref[:, :D_MODEL], preferred_element_type=F32)
    z_b = jnp.dot(yb, glu_ref[:, D_MODEL:], preferred_element_type=F32)

    x = x_ref[0]
    h = _rms_mod(x, g_ref[...], mod_ref[0, 1:2, :], mod_ref[0, 0:1, :]).astype(BF16)
    hbuf[...] = h

    zc = jnp.dot(h, wc_ref[...], preferred_element_type=F32)
    sbuf[...] = z_a * _sigmoid(z_b)
    u = zc[:, :CONV_WIDTH] * _sigmoid(zc[:, CONV_WIDTH:])
    for k in range(n_lt):
        ubuf[k, halo:halo + ts, :] = u[:, k * LANES:(k + 1) * LANES]

    first = halo - (CONV_KERNEL - 1)
    tiles_per_trip = ts // CONV_ROWS // MIX_TRIPS

    def trip(it, carry):
        for q in range(tiles_per_trip):
            r0 = pl.multiple_of((it * tiles_per_trip + q) * CONV_ROWS, CONV_ROWS)
            for k in range(n_lt):
                lanes = slice(k * LANES, (k + 1) * LANES)
                acc = None
                for j in range(CONV_KERNEL):
                    term = (cw_ref[j:j + 1, lanes]
                            * ubuf[k, pl.ds(r0 + (first + j), CONV_ROWS), :])
                    acc = term if acc is None else acc + term
                cbuf[pl.ds(r0, CONV_ROWS), lanes] = acc + cb_ref[:, lanes]
        hb = hbuf[...]
        gbuf[it] = jnp.dot(hb, wg_ref[it], preferred_element_type=F32)
        gbuf[MIX_TRIPS + it] = jnp.dot(hb, wg_ref[MIX_TRIPS + it],
                                       preferred_element_type=F32)
        return carry

    lax.fori_loop(0, MIX_TRIPS, trip, 0)
    for k in range(n_lt):
        ubuf[k, 0:halo, :] = ubuf[k, ts:ts + halo, :]
    acc = cbuf[...]
    mu = jnp.mean(acc, axis=-1, keepdims=True)
    cen = acc - mu
    var = jnp.mean(cen * cen, axis=-1, keepdims=True)
    yn = cen * lax.rsqrt(var + EPS) * lng_ref[...] + lnb_ref[...]
    yn = yn * _sigmoid(yn)
    y_conv = jnp.dot(yn.astype(BF16), proj_ref[...], preferred_element_type=F32)

    part_w = D_MODEL // MIX_TRIPS
    merged = jnp.concatenate(
        [_sigmoid(gbuf[it]) * y_conv[:, it * part_w:(it + 1) * part_w]
         + _sigmoid(gbuf[MIX_TRIPS + it]) * sbuf[:, it * part_w:(it + 1) * part_w]
         for it in range(MIX_TRIPS)], axis=1)
    upd = jnp.dot(merged.astype(BF16), wout_ref[...], preferred_element_type=F32)
    o_ref[0] = x + mod_ref[0, 2:3, :] * upd


def _const_spec(shape, index=None):
    index = (0,) * len(shape) if index is None else index
    return pl.BlockSpec(shape, lambda b, s: index, pipeline_mode=pl.Buffered(1))


def _column_chunks(w, n):
    k, cols = w.shape
    return jnp.transpose(w.reshape(k, n, cols // n), (1, 0, 2))


def _mixer_call(x, y, mod, g, w_in_b, cw, cb, lng, lnb, proj, glu, wout, ts):
    bsz, seq, _ = x.shape
    part_w = D_MODEL // MIX_TRIPS
    w_gates = _column_chunks(w_in_b[:, 2 * CONV_WIDTH + SSM_WIDTH:], 2 * MIX_TRIPS)
    return pl.pallas_call(
        _mixer_kernel,
        grid=(bsz, seq // ts),
        in_specs=[
            pl.BlockSpec((1, ts, D_MODEL), lambda b, s: (b, s, 0)),
            pl.BlockSpec((1, ts, SSM_WIDTH), lambda b, s: (b, s, 0)),
            pl.BlockSpec((1, 6, D_MODEL), lambda b, s: (b, 0, 0)),
            _const_spec((1, D_MODEL)),
            _const_spec((D_MODEL, 2 * CONV_WIDTH)),
            _const_spec((2 * MIX_TRIPS, D_MODEL, part_w)),
            _const_spec((CONV_KERNEL, CONV_WIDTH)),
            _const_spec((1, CONV_WIDTH)),
            _const_spec((1, CONV_WIDTH)),
            _const_spec((1, CONV_WIDTH)),
            _const_spec((CONV_WIDTH, D_MODEL)),
            _const_spec((SSM_WIDTH, 2 * D_MODEL)),
            _const_spec((D_MODEL, D_MODEL)),
        ],
        out_specs=pl.BlockSpec((1, ts, D_MODEL), lambda b, s: (b, s, 0)),
        out_shape=jax.ShapeDtypeStruct(x.shape, F32),
        scratch_shapes=[pltpu.VMEM((CONV_WIDTH // LANES, ts + CONV_HALO, LANES), F32),
                        pltpu.VMEM((ts, CONV_WIDTH), F32),
                        pltpu.VMEM((ts, D_MODEL), BF16),
                        pltpu.VMEM((2 * MIX_TRIPS, ts, part_w), F32),
                        pltpu.VMEM((ts, D_MODEL), F32)],
        compiler_params=pltpu.CompilerParams(
            dimension_semantics=("arbitrary", "arbitrary"),
            vmem_limit_bytes=VMEM_LIMIT),
        name="mixer",
    )(x, y, mod, g, w_in_b, w_gates, cw, cb, lng, lnb, proj, glu, wout)


def _ffn_kernel(x_ref, mod_ref, g_ref, wg_ref, wu_ref, wo_ref, fg_ref, o_ref):
    x = x_ref[0]
    h = _rms_mod(x, g_ref[...], mod_ref[0, 4:5, :], mod_ref[0, 3:4, :]).astype(BF16)
    fg = jnp.dot(h, wg_ref[...], preferred_element_type=F32)
    fu = jnp.dot(h, wu_ref[...], preferred_element_type=F32)
    act = (fg * _sigmoid(fg) * fu).astype(BF16)
    upd = jnp.dot(act, wo_ref[...], preferred_element_type=F32)
    x2 = x + mod_ref[0, 5:6, :] * upd
    ms = jnp.mean(x2 * x2, axis=-1, keepdims=True)
    o_ref[0] = x2 * lax.rsqrt(ms + EPS) * fg_ref[...]


def _ffn_call(x, mod, g, w_ffn_in_b, wo, final_g, ts):
    bsz, seq, _ = x.shape
    return pl.pallas_call(
        _ffn_kernel,
        grid=(bsz, seq // ts),
        in_specs=[
            pl.BlockSpec((1, ts, D_MODEL), lambda b, s: (b, s, 0)),
            pl.BlockSpec((1, 6, D_MODEL), lambda b, s: (b, 0, 0)),
            _const_spec((1, D_MODEL)),
            _const_spec((D_MODEL, FFN_HIDDEN), (0, 0)),
            _const_spec((D_MODEL, FFN_HIDDEN), (0, 1)),
            _const_spec((FFN_HIDDEN, D_MODEL)),
            _const_spec((1, D_MODEL)),
        ],
        out_specs=pl.BlockSpec((1, ts, D_MODEL), lambda b, s: (b, s, 0)),
        out_shape=jax.ShapeDtypeStruct(x.shape, F32),
        compiler_params=pltpu.CompilerParams(
            dimension_semantics=("arbitrary", "arbitrary"),
            vmem_limit_bytes=VMEM_LIMIT),
        name="ffn",
    )(x, mod, g, w_ffn_in_b, w_ffn_in_b, wo, final_g)


def kernel(x, c, w_ada, b_ada, norm1_g, w_in, conv_w, conv_b, conv_ln_g, conv_ln_b,
           conv_proj, ssm_a_re, ssm_a_im, ssm_b_re, ssm_b_im, ssm_c_re, ssm_c_im,
           ssm_d, ssm_log_dt, ssm_glu, w_out, norm2_g, w_ffn_in, w_ffn_out, final_g):
    bsz, seq, _ = x.shape
    assert w_ada.shape[0] == 1, "single layer only"

    mod = _mod_call(c, w_ada[0], b_ada[0])
    mod = jnp.transpose(mod, (1, 0, 2))

    w_in_b = w_in[0].astype(BF16)
    g1 = norm1_g[0].reshape(1, -1)

    u_t = _ssm_in_call(x, mod, g1, w_in_b, 128)
    ops = _ssm_prep_call(ssm_a_re[0], ssm_a_im[0], ssm_b_re[0], ssm_b_im[0],
                         ssm_c_re[0], ssm_c_im[0], ssm_d[0], ssm_log_dt[0])
    y = _ssm_call(u_t, *ops)

    x1 = _mixer_call(
        x, y, mod, g1, w_in_b, conv_w[0], conv_b[0].reshape(1, -1),
        conv_ln_g[0].reshape(1, -1), conv_ln_b[0].reshape(1, -1),
        conv_proj[0].astype(BF16), ssm_glu[0].astype(BF16), w_out[0].astype(BF16), 512)

    return _ffn_call(x1, mod, norm2_g[0].reshape(1, -1), w_ffn_in[0].astype(BF16),
                     w_ffn_out[0].astype(BF16), final_g.reshape(1, -1), 512)
```

```python
import jax
import jax.numpy as jnp
from jax import lax
from jax.experimental import pallas as pl
from jax.experimental.pallas import tpu as pltpu

F32 = jnp.float32
BF16 = jnp.bfloat16

D_MODEL = 1024
CONV_WIDTH = 512
CONV_KERNEL = 31
SSM_WIDTH = 512
SSM_GROUP = 16
SSM_GROUPS = 32
SSM_STATE = 64
FFN_HIDDEN = 2816
EPS = 1e-6

CHUNK = 16
PAIR_CH = 2 * SSM_GROUP
PAIR_ST = 2 * SSM_STATE
PAIR_ROWS = CHUNK * PAIR_CH
N_PAIRS = SSM_GROUPS // 2
PAIRS_PER_STEP = 4
CONV_HALO = 32
GATE_BLOCK = 512
LANES = 128
SUBLANES = 8
CONV_ROWS = 64
N_SIDE = 8

VMEM_LIMIT = 56 * 1024 * 1024


def _rms_mod(x, g, scale, shift):
    ms = jnp.mean(x * x, axis=-1, keepdims=True)
    y = x * lax.rsqrt(ms + EPS)
    return (y * g) * (1.0 + scale) + shift


def _sigmoid(v):
    return jax.nn.sigmoid(v)


def _mod_kernel(c_ref, w_ref, b_ref, o_ref):
    c = c_ref[...]
    ca = c * _sigmoid(c)
    o_ref[0] = jnp.dot(ca, w_ref[...], preferred_element_type=F32,
                       precision=lax.Precision.HIGHEST) + b_ref[...]


def _mod_call(c, w_ada, b_ada):
    bsz = c.shape[0]
    n_mod = w_ada.shape[1] // D_MODEL
    return pl.pallas_call(
        _mod_kernel,
        grid=(n_mod,),
        in_specs=[
            pl.BlockSpec((bsz, D_MODEL), lambda j: (0, 0)),
            pl.BlockSpec((D_MODEL, D_MODEL), lambda j: (0, j)),
            pl.BlockSpec((1, D_MODEL), lambda j: (0, j)),
        ],
        out_specs=pl.BlockSpec((1, bsz, D_MODEL), lambda j: (j, 0, 0)),
        out_shape=jax.ShapeDtypeStruct((n_mod, bsz, D_MODEL), F32),
        name="mod",
    )(c, w_ada, b_ada.reshape(1, -1))


def _ssm_prep_kernel(ar_ref, ai_ref, ldt_ref, cre_ref, cim_ref, bre_ref, bim_ref,
                     dl_ref, kt_ref, bst_ref, cst_ref, lre_ref, lim_ref, cl_s, lb_s):
    ar, ai = ar_ref[0], ai_ref[0]
    dt = jnp.exp(ldt_ref[0])
    dar, dai = dt * ar, dt * ai
    mag, cos1, sin1 = jnp.exp(dar), jnp.cos(dai), jnp.sin(dai)
    n_re, n_im = mag * cos1 - 1.0, mag * sin1
    den = ar * ar + ai * ai
    q_re = (n_re * ar + n_im * ai) / den
    q_im = (n_im * ar - n_re * ai) / den

    row = lax.broadcasted_iota(jnp.int32, (PAIR_CH, PAIR_ST), 0)
    lane = lax.broadcasted_iota(jnp.int32, (PAIR_CH, PAIR_ST), 1)
    same_group = (row // SSM_GROUP) == (lane // SSM_STATE)

    def block_diag(ref):
        v = ref[0]
        return jnp.where(same_group, jnp.concatenate([v, v], axis=1), 0.0)

    c_re, c_im = block_diag(cre_ref), block_diag(cim_ref)
    b_re, b_im = block_diag(bre_ref), block_diag(bim_ref)
    bb_re = q_re * b_re - q_im * b_im
    bb_im = q_re * b_im + q_im * b_re

    inv_mag = jnp.exp(-dar)
    one = jnp.ones_like(dar)
    powers = {0: (one, jnp.zeros_like(dar)), 1: (mag * cos1, mag * sin1),
              -1: (inv_mag * cos1, -(inv_mag * sin1))}

    def power(k):
        if k not in powers:
            step = 1 if k > 0 else -1
            (p_re, p_im), (s_re, s_im) = power(k - step), powers[step]
            powers[k] = (p_re * s_re - p_im * s_im, p_re * s_im + p_im * s_re)
        return powers[k]

    def cmul_cat(x_re, x_im, p, conj_out):
        p_re, p_im = p
        o_re = x_re * p_re - x_im * p_im
        o_im = x_re * p_im + x_im * p_re
        return jnp.concatenate([o_re, -o_im if conj_out else o_im], axis=1)

    for t in range(CHUNK):
        rs = slice(t * PAIR_CH, (t + 1) * PAIR_CH)
        cl_s[rs, :] = cmul_cat(c_re, c_im, power(t), True)
        lb_s[rs, :] = cmul_cat(bb_re, bb_im, power(-t), False)
        cst_ref[0, rs, :] = cmul_cat(c_re, c_im, power(t + 1), True).astype(BF16)
        bst_ref[0, rs, :] = cmul_cat(bb_re, bb_im, power(CHUNK - 1 - t),
                                     False).astype(BF16)
    lre_ref[0], lim_ref[0] = power(CHUNK)

    k = lax.dot_general(cl_s[...], lb_s[...], (((1,), (1,)), ((), ())),
                        preferred_element_type=F32, precision=lax.Precision.HIGHEST)
    ri = lax.broadcasted_iota(jnp.int32, (PAIR_ROWS, PAIR_ROWS), 0)
    ci = lax.broadcasted_iota(jnp.int32, (PAIR_ROWS, PAIR_ROWS), 1)
    k = jnp.where(ri // PAIR_CH >= ci // PAIR_CH, k, 0.0)
    k = k + jnp.where(ri == ci, dl_ref[0], 0.0)
    kt_ref[0] = k.astype(BF16)


def _ssm_prep_call(a_re, a_im, b_re, b_im, c_re, c_im, d, log_dt):
    np_, pc, ps, st = N_PAIRS, PAIR_CH, PAIR_ST, SSM_STATE
    row = lambda a: a.reshape(np_, 1, ps)
    ldt = jnp.broadcast_to(log_dt[:, None], (SSM_GROUPS, st))
    d_lane = jnp.tile(d.reshape(np_, 1, pc), (1, 1, CHUNK))
    bt = lambda a: jnp.transpose(a, (0, 2, 1)).reshape(np_, pc, st)
    spec = lambda *shape: pl.BlockSpec((1,) + shape, lambda p: (p, 0, 0))
    return pl.pallas_call(
        _ssm_prep_kernel,
        grid=(np_,),
        in_specs=[spec(1, ps), spec(1, ps), spec(1, ps), spec(pc, st), spec(pc, st),
                  spec(pc, st), spec(pc, st), spec(1, PAIR_ROWS)],
        out_specs=[spec(PAIR_ROWS, PAIR_ROWS), spec(PAIR_ROWS, 2 * ps),
                   spec(PAIR_ROWS, 2 * ps), spec(1, ps), spec(1, ps)],
        out_shape=[jax.ShapeDtypeStruct((np_, PAIR_ROWS, PAIR_ROWS), BF16),
                   jax.ShapeDtypeStruct((np_, PAIR_ROWS, 2 * ps), BF16),
                   jax.ShapeDtypeStruct((np_, PAIR_ROWS, 2 * ps), BF16),
                   jax.ShapeDtypeStruct((np_, 1, ps), F32),
                   jax.ShapeDtypeStruct((np_, 1, ps), F32)],
        scratch_shapes=[pltpu.VMEM((PAIR_ROWS, 2 * ps), F32)] * 2,
        name="ssm_prep",
    )(row(a_re), row(a_im), row(ldt), c_re.reshape(np_, pc, st),
      c_im.reshape(np_, pc, st), bt(b_re), bt(b_im), d_lane)


def _ssm_in_kernel(x_ref, mod_ref, g_ref, w_ref, o_ref, h_ref, us):
    n_chunks = o_ref.shape[-1]
    h = _rms_mod(x_ref[0], g_ref[...], mod_ref[0, 1:2, :], mod_ref[0, 0:1, :])
    h = h.astype(BF16)
    h_ref[0] = h
    u = jnp.dot(h, w_ref[...].astype(BF16), preferred_element_type=F32)
    for k in range(SSM_WIDTH // LANES):
        us[k] = u[:, k * LANES:(k + 1) * LANES]
    for tau in range(CHUNK):
        for k in range(SSM_WIDTH // LANES):
            piece = us[k, pl.ds(tau, n_chunks, stride=CHUNK), :]
            o_ref[0, tau, k * LANES:(k + 1) * LANES, :] = piece.T.astype(o_ref.dtype)


def _ssm_in_call(x, mod, g, w_in, chunks_per_step):
    bsz, seq, _ = x.shape
    ts = chunks_per_step * CHUNK
    ssm_col_block = 2 * CONV_WIDTH // SSM_WIDTH
    return pl.pallas_call(
        _ssm_in_kernel,
        grid=(bsz, seq // ts),
        in_specs=[
            pl.BlockSpec((1, ts, D_MODEL), lambda b, s: (b, s, 0)),
            pl.BlockSpec((1, 6, D_MODEL), lambda b, s: (b, 0, 0)),
            pl.BlockSpec((1, D_MODEL), lambda b, s: (0, 0)),
            pl.BlockSpec((D_MODEL, SSM_WIDTH), lambda b, s: (0, ssm_col_block)),
        ],
        out_specs=[pl.BlockSpec((1, CHUNK, SSM_WIDTH, chunks_per_step),
                                lambda b, s: (b, 0, 0, s)),
                   pl.BlockSpec((1, ts, D_MODEL), lambda b, s: (b, s, 0))],
        out_shape=[jax.ShapeDtypeStruct((bsz, CHUNK, SSM_WIDTH, seq // CHUNK), BF16),
                   jax.ShapeDtypeStruct((bsz, seq, D_MODEL), BF16)],
        scratch_shapes=[pltpu.VMEM((SSM_WIDTH // LANES, ts, LANES), F32)],
        compiler_params=pltpu.CompilerParams(
            dimension_semantics=("arbitrary", "arbitrary"),
            vmem_limit_bytes=VMEM_LIMIT),
        name="ssm_in",
    )(x, mod, g, w_in)


def _ssm_kernel(u_ref, kt_ref, bst_ref, cst_ref, lre_ref, lim_ref, *rest):
    n_w = (len(rest) - 7) // 2
    w_in_refs, o_ref, w_out_refs = rest[:n_w], rest[n_w], rest[n_w + 1:2 * n_w + 1]
    sre, sim, xre, xim, ys, yn = rest[2 * n_w + 1:]
    n_chunks = u_ref.shape[-1]
    pw = PAIR_ST

    for src, dst in zip(w_in_refs, w_out_refs):
        dst[...] = src[...].astype(BF16)

    def load_pair(j):
        r = u_ref[0, :, j * PAIR_CH:(j + 1) * PAIR_CH, :]
        return r.reshape(PAIR_ROWS, n_chunks)

    for j in range(PAIRS_PER_STEP):
        s = lax.dot_general(load_pair(j), bst_ref[j], (((0,), (0,)), ((), ())),
                            preferred_element_type=F32)
        sre[:, j * pw:(j + 1) * pw] = s[:, :pw]
        sim[:, j * pw:(j + 1) * pw] = s[:, pw:]

    lre = lre_ref[0]
    lim = lim_ref[0]

    def body(c, carry):
        xr, xi = carry
        xre[pl.ds(c, 1), :] = xr
        xim[pl.ds(c, 1), :] = xi
        nr = lre * xr - lim * xi + sre[pl.ds(c, 1), :]
        ni = lre * xi + lim * xr + sim[pl.ds(c, 1), :]
        return nr, ni

    zero = jnp.zeros((1, PAIRS_PER_STEP * pw), F32)
    lax.fori_loop(0, n_chunks, body, (zero, zero), unroll=8)

    for j in range(PAIRS_PER_STEP):
        y_intra = jnp.dot(kt_ref[j], load_pair(j), preferred_element_type=F32)
        xp = jnp.concatenate([xre[:, j * pw:(j + 1) * pw],
                              xim[:, j * pw:(j + 1) * pw]], axis=1).astype(BF16)
        y_state = lax.dot_general(cst_ref[j], xp, (((1,), (1,)), ((), ())),
                                  preferred_element_type=F32)
        y = jax.nn.gelu(y_intra + y_state)
        ys[:, j * PAIR_CH:(j + 1) * PAIR_CH, :] = y.reshape(CHUNK, PAIR_CH, n_chunks)

    for t in range(CHUNK):
        yn[pl.ds(t, n_chunks, stride=CHUNK), :] = ys[t].T
    o_ref[0] = yn[...].astype(o_ref.dtype)


def _ssm_call(u_t, kt, bst, cst, lam_re, lam_im, weights):
    bsz, _, _, n_chunks = u_t.shape
    seq = n_chunks * CHUNK
    pps = PAIRS_PER_STEP
    ch = pps * PAIR_CH
    state_w = pps * PAIR_ST
    steps = N_PAIRS // pps
    lam_re = lam_re.reshape(steps, 1, state_w)
    lam_im = lam_im.reshape(steps, 1, state_w)
    n_steps = steps * bsz
    w_specs = [pl.BlockSpec((w.shape[0] // n_steps, w.shape[1]),
                            lambda p, b: (p * bsz + b, 0)) for w in weights]
    outs = pl.pallas_call(
        _ssm_kernel,
        grid=(steps, bsz),
        in_specs=[
            pl.BlockSpec((1, CHUNK, ch, n_chunks), lambda p, b: (b, 0, p, 0)),
            pl.BlockSpec((pps, PAIR_ROWS, PAIR_ROWS), lambda p, b: (p, 0, 0)),
            pl.BlockSpec((pps, PAIR_ROWS, 2 * PAIR_ST), lambda p, b: (p, 0, 0)),
            pl.BlockSpec((pps, PAIR_ROWS, 2 * PAIR_ST), lambda p, b: (p, 0, 0)),
            pl.BlockSpec((1, 1, state_w), lambda p, b: (p, 0, 0)),
            pl.BlockSpec((1, 1, state_w), lambda p, b: (p, 0, 0)),
        ] + w_specs,
        out_specs=[pl.BlockSpec((1, seq, ch), lambda p, b: (b, 0, p))] + w_specs,
        out_shape=[jax.ShapeDtypeStruct((bsz, seq, SSM_WIDTH), BF16)] + [
            jax.ShapeDtypeStruct(w.shape, BF16) for w in weights],
        scratch_shapes=[pltpu.VMEM((n_chunks, state_w), F32)] * 4 + [
            pltpu.VMEM((CHUNK, ch, n_chunks), F32), pltpu.VMEM((seq, ch), F32)],
        compiler_params=pltpu.CompilerParams(
            dimension_semantics=("arbitrary", "arbitrary"),
            vmem_limit_bytes=VMEM_LIMIT),
        name="ssm",
    )(u_t, kt, bst, cst, lam_re, lam_im, *weights)
    return outs[0], outs[1:]


def _mixer_kernel(x_ref, h_ref, y_ref, mod_ref, wc_ref, wgc0_ref, wgc1_ref, wgs0_ref,
                  wgs1_ref, cw_ref, cb_ref, lng_ref, lnb_ref, proj_ref, glu_ref,
                  wout_ref, wf_ref, o_ref, wf_o_ref, ubuf, cbuf, sbuf):
    ts = x_ref.shape[1]
    halo = CONV_HALO
    half = D_MODEL // 2
    n_lt = CONV_WIDTH // LANES

    wf_o_ref[...] = wf_ref[...].astype(BF16)

    @pl.when(pl.program_id(1) == 0)
    def _():
        ubuf[:, 0:halo, :] = jnp.zeros((n_lt, halo, LANES), F32)

    h = h_ref[0]

    zc = jnp.dot(h, wc_ref[...], preferred_element_type=F32)
    u = zc[:, :CONV_WIDTH] * _sigmoid(zc[:, CONV_WIDTH:])
    for k in range(n_lt):
        ubuf[k, halo:halo + ts, :] = u[:, k * LANES:(k + 1) * LANES]

    y_in = y_ref[0]

    def dot_h(w_ref):
        return jnp.dot(h, w_ref[...], preferred_element_type=F32)

    def dot_y(k):
        return jnp.dot(y_in, glu_ref[:, k * half:(k + 1) * half],
                       preferred_element_type=F32)

    gc0, gs0, za0, zb0, gc1, gs1, za1, zb1 = range(N_SIDE)

    def put(k, v):
        sbuf[k] = v

    dyn_zero = jnp.minimum(pl.program_id(1), 0)
    never = lax.broadcasted_iota(jnp.int32, (SUBLANES, LANES), 0) < dyn_zero

    def anchored(term, k):
        fence = sbuf[dyn_zero, k * SUBLANES:(k + 1) * SUBLANES, 0:LANES]
        head = jnp.where(never, fence, term[0:SUBLANES, :])
        return jnp.concatenate([head, term[SUBLANES:, :]], axis=0)

    side_jobs = [
        lambda: put(gc0, dot_h(wgc0_ref)), lambda: put(gs0, dot_h(wgs0_ref)),
        lambda: put(za0, dot_y(0)), lambda: put(zb0, dot_y(2)),
        lambda: put(gc1, dot_h(wgc1_ref)), lambda: put(gs1, dot_h(wgs1_ref)),
        lambda: put(za1, dot_y(1)), lambda: put(zb1, dot_y(3)),
    ]
    first = halo - (CONV_KERNEL - 1)
    n_rt = ts // CONV_ROWS
    for i in range(n_rt):
        for k in range(n_lt):
            lanes = slice(k * LANES, (k + 1) * LANES)
            acc = None
            for j in range(CONV_KERNEL):
                row0 = first + j + i * CONV_ROWS
                term = cw_ref[j:j + 1, lanes] * ubuf[k, row0:row0 + CONV_ROWS, :]
                if j == 0 and i > 0:
                    term = anchored(term, k)
                acc = term if acc is None else acc + term
            cbuf[i * CONV_ROWS:(i + 1) * CONV_ROWS, lanes] = acc + cb_ref[:, lanes]
        if (i + 1) * len(side_jobs) % n_rt == 0:
            side_jobs[(i + 1) * len(side_jobs) // n_rt - 1]()
    for k in range(n_lt):
        ubuf[k, 0:halo, :] = ubuf[k, ts:ts + halo, :]
    acc = cbuf[...]
    mu = jnp.mean(acc, axis=-1, keepdims=True)
    cen = acc - mu
    var = jnp.mean(cen * cen, axis=-1, keepdims=True)
    yn = cen * lax.rsqrt(var + EPS) * lng_ref[...] + lnb_ref[...]
    yn = yn * _sigmoid(yn)
    y_conv = jnp.dot(yn.astype(BF16), proj_ref[...], preferred_element_type=F32)

    def merge(conv_half, g_conv, g_ssm, z_a, z_b):
        y_ssm = sbuf[z_a] * _sigmoid(sbuf[z_b])
        return _sigmoid(sbuf[g_conv]) * conv_half + _sigmoid(sbuf[g_ssm]) * y_ssm

    merged = jnp.concatenate(
        [merge(y_conv[:, :half], gc0, gs0, za0, zb0),
         merge(y_conv[:, half:], gc1, gs1, za1, zb1)], axis=1)
    upd = jnp.dot(merged.astype(BF16), wout_ref[...], preferred_element_type=F32)
    o_ref[0] = x_ref[0] + mod_ref[0, 2:3, :] * upd


def _const_spec(shape, index=None):
    index = (0,) * len(shape) if index is None else index
    return pl.BlockSpec(shape, lambda b, s: index, pipeline_mode=pl.Buffered(1))


def _mixer_call(x, h, y, mod, w_in_b, cw, cb, lng, lnb, proj, glu, wout, w_ffn_in, ts):
    bsz, seq, _ = x.shape
    n_s = seq // ts
    gate0 = (2 * CONV_WIDTH + SSM_WIDTH) // GATE_BLOCK
    gate_spec = lambda i: _const_spec((D_MODEL, GATE_BLOCK), (0, gate0 + i))
    wf_rows, wf_cols = w_ffn_in.shape
    wf_spec = pl.BlockSpec((wf_rows // (bsz * n_s), wf_cols), lambda b, s: (b * n_s + s, 0))
    return pl.pallas_call(
        _mixer_kernel,
        grid=(bsz, seq // ts),
        in_specs=[
            pl.BlockSpec((1, ts, D_MODEL), lambda b, s: (b, s, 0)),
            pl.BlockSpec((1, ts, D_MODEL), lambda b, s: (b, s, 0)),
            pl.BlockSpec((1, ts, SSM_WIDTH), lambda b, s: (b, s, 0)),
            pl.BlockSpec((1, 6, D_MODEL), lambda b, s: (b, 0, 0)),
            _const_spec((D_MODEL, 2 * CONV_WIDTH)),
            gate_spec(0), gate_spec(1), gate_spec(2), gate_spec(3),
            _const_spec((CONV_KERNEL, CONV_WIDTH)),
            _const_spec((1, CONV_WIDTH)),
            _const_spec((1, CONV_WIDTH)),
            _const_spec((1, CONV_WIDTH)),
            _const_spec((CONV_WIDTH, D_MODEL)),
            _const_spec((SSM_WIDTH, 2 * D_MODEL)),
            _const_spec((D_MODEL, D_MODEL)),
            wf_spec,
        ],
        out_specs=[pl.BlockSpec((1, ts, D_MODEL), lambda b, s: (b, s, 0)), wf_spec],
        out_shape=[jax.ShapeDtypeStruct(x.shape, F32),
                   jax.ShapeDtypeStruct(w_ffn_in.shape, BF16)],
        scratch_shapes=[pltpu.VMEM((CONV_WIDTH // LANES, ts + CONV_HALO, LANES), F32),
                        pltpu.VMEM((ts, CONV_WIDTH), F32),
                        pltpu.VMEM((N_SIDE, ts, D_MODEL // 2), F32)],
        compiler_params=pltpu.CompilerParams(
            dimension_semantics=("arbitrary", "arbitrary"),
            vmem_limit_bytes=VMEM_LIMIT),
        name="mixer",
    )(x, h, y, mod, w_in_b, w_in_b, w_in_b, w_in_b, w_in_b, cw, cb, lng, lnb, proj, glu,
      wout, w_ffn_in)


def _ffn_kernel(x_ref, mod_ref, g_ref, wg_ref, wu_ref, wo_ref, fg_ref, o_ref):
    x = x_ref[0]
    h = _rms_mod(x, g_ref[...], mod_ref[0, 4:5, :], mod_ref[0, 3:4, :]).astype(BF16)
    fg = jnp.dot(h, wg_ref[...], preferred_element_type=F32)
    fu = jnp.dot(h, wu_ref[...], preferred_element_type=F32)
    act = (fg * _sigmoid(fg) * fu).astype(BF16)
    upd = jnp.dot(act, wo_ref[...], preferred_element_type=F32)
    x2 = x + mod_ref[0, 5:6, :] * upd
    ms = jnp.mean(x2 * x2, axis=-1, keepdims=True)
    o_ref[0] = x2 * lax.rsqrt(ms + EPS) * fg_ref[...]


def _ffn_call(x, mod, g, w_ffn_in_b, wo, final_g, ts):
    bsz, seq, _ = x.shape
    return pl.pallas_call(
        _ffn_kernel,
        grid=(bsz, seq // ts),
        in_specs=[
            pl.BlockSpec((1, ts, D_MODEL), lambda b, s: (b, s, 0)),
            pl.BlockSpec((1, 6, D_MODEL), lambda b, s: (b, 0, 0)),
            _const_spec((1, D_MODEL)),
            _const_spec((D_MODEL, FFN_HIDDEN), (0, 0)),
            _const_spec((D_MODEL, FFN_HIDDEN), (0, 1)),
            _const_spec((FFN_HIDDEN, D_MODEL)),
            _const_spec((1, D_MODEL)),
        ],
        out_specs=pl.BlockSpec((1, ts, D_MODEL), lambda b, s: (b, s, 0)),
        out_shape=jax.ShapeDtypeStruct(x.shape, F32),
        compiler_params=pltpu.CompilerParams(
            dimension_semantics=("arbitrary", "arbitrary"),
            vmem_limit_bytes=VMEM_LIMIT),
        name="ffn",
    )(x, mod, g, w_ffn_in_b, w_ffn_in_b, wo, final_g)


def kernel(x, c, w_ada, b_ada, norm1_g, w_in, conv_w, conv_b, conv_ln_g, conv_ln_b,
           conv_proj, ssm_a_re, ssm_a_im, ssm_b_re, ssm_b_im, ssm_c_re, ssm_c_im,
           ssm_d, ssm_log_dt, ssm_glu, w_out, norm2_g, w_ffn_in, w_ffn_out, final_g):
    bsz, seq, _ = x.shape
    assert w_ada.shape[0] == 1, "single layer only"

    mod = _mod_call(c, w_ada[0], b_ada[0])
    mod = jnp.transpose(mod, (1, 0, 2))

    g1 = norm1_g[0].reshape(1, -1)

    u_t, h1 = _ssm_in_call(x, mod, g1, w_in[0], 128)
    ops = _ssm_prep_call(ssm_a_re[0], ssm_a_im[0], ssm_b_re[0], ssm_b_im[0],
                         ssm_c_re[0], ssm_c_im[0], ssm_d[0], ssm_log_dt[0])
    y, (w_in_b, proj_b, glu_b, w_out_b, w_ffn_out_b) = _ssm_call(
        u_t, *ops, (w_in[0], conv_proj[0], ssm_glu[0], w_out[0], w_ffn_out[0]))

    x1, w_ffn_in_b = _mixer_call(
        x, h1, y, mod, w_in_b, conv_w[0], conv_b[0].reshape(1, -1),
        conv_ln_g[0].reshape(1, -1), conv_ln_b[0].reshape(1, -1),
        proj_b, glu_b, w_out_b, w_ffn_in[0], 512)

    return _ffn_call(x1, mod, norm2_g[0].reshape(1, -1), w_ffn_in_b, w_ffn_out_b,
                     final_g.reshape(1, -1), 512)
```

```python
import jax
import jax.numpy as jnp
from jax import lax
from jax.experimental import pallas as pl
from jax.experimental.pallas import tpu as pltpu

F32 = jnp.float32
BF16 = jnp.bfloat16

D_MODEL = 1024
CONV_WIDTH = 512
CONV_KERNEL = 31
SSM_WIDTH = 512
SSM_GROUP = 16
SSM_GROUPS = 32
SSM_STATE = 64
FFN_HIDDEN = 2816
EPS = 1e-6

CHUNK = 16
PAIR_CH = 2 * SSM_GROUP
PAIR_ST = 2 * SSM_STATE
PAIR_ROWS = CHUNK * PAIR_CH
N_PAIRS = SSM_GROUPS // 2
PAIRS_PER_STEP = 4
CONV_HALO = 32
GATE_BLOCK = 512
LANES = 128
SUBLANES = 8
CONV_ROWS = 64
N_SIDE = 8

VMEM_LIMIT = 56 * 1024 * 1024


def _rms_mod(x, g, scale, shift):
    ms = jnp.mean(x * x, axis=-1, keepdims=True)
    y = x * lax.rsqrt(ms + EPS)
    return (y * g) * (1.0 + scale) + shift


def _sigmoid(v):
    return jax.nn.sigmoid(v)


def _mod_kernel(c_ref, w_ref, b_ref, o_ref):
    c = c_ref[...]
    ca = c * _sigmoid(c)
    o_ref[0] = jnp.dot(ca.astype(BF16), w_ref[...].astype(BF16),
                       preferred_element_type=F32) + b_ref[...]


def _mod_call(c, w_ada, b_ada):
    bsz = c.shape[0]
    n_mod = w_ada.shape[1] // D_MODEL
    return pl.pallas_call(
        _mod_kernel,
        grid=(n_mod,),
        in_specs=[
            pl.BlockSpec((bsz, D_MODEL), lambda j: (0, 0)),
            pl.BlockSpec((D_MODEL, D_MODEL), lambda j: (0, j)),
            pl.BlockSpec((1, D_MODEL), lambda j: (0, j)),
        ],
        out_specs=pl.BlockSpec((1, bsz, D_MODEL), lambda j: (j, 0, 0)),
        out_shape=jax.ShapeDtypeStruct((n_mod, bsz, D_MODEL), F32),
        name="mod",
    )(c, w_ada, b_ada.reshape(1, -1))


def _ssm_prep_kernel(ar_ref, ai_ref, ldt_ref, cre_ref, cim_ref, bre_ref, bim_ref,
                     dl_ref, kt_ref, bst_ref, cst_ref, lre_ref, lim_ref, cl_s, lb_s):
    ar, ai = ar_ref[0], ai_ref[0]
    dt = jnp.exp(ldt_ref[0])
    dar, dai = dt * ar, dt * ai
    mag, cos1, sin1 = jnp.exp(dar), jnp.cos(dai), jnp.sin(dai)
    n_re, n_im = mag * cos1 - 1.0, mag * sin1
    den = ar * ar + ai * ai
    q_re = (n_re * ar + n_im * ai) / den
    q_im = (n_im * ar - n_re * ai) / den

    row = lax.broadcasted_iota(jnp.int32, (PAIR_CH, PAIR_ST), 0)
    lane = lax.broadcasted_iota(jnp.int32, (PAIR_CH, PAIR_ST), 1)
    same_group = (row // SSM_GROUP) == (lane // SSM_STATE)

    def block_diag(ref):
        v = ref[0]
        return jnp.where(same_group, jnp.concatenate([v, v], axis=1), 0.0)

    c_re, c_im = block_diag(cre_ref), block_diag(cim_ref)
    b_re, b_im = block_diag(bre_ref), block_diag(bim_ref)
    bb_re = q_re * b_re - q_im * b_im
    bb_im = q_re * b_im + q_im * b_re

    inv_mag = jnp.exp(-dar)
    one = jnp.ones_like(dar)
    powers = {0: (one, jnp.zeros_like(dar)), 1: (mag * cos1, mag * sin1),
              -1: (inv_mag * cos1, -(inv_mag * sin1))}

    def power(k):
        if k not in powers:
            step = 1 if k > 0 else -1
            (p_re, p_im), (s_re, s_im) = power(k - step), powers[step]
            powers[k] = (p_re * s_re - p_im * s_im, p_re * s_im + p_im * s_re)
        return powers[k]

    def cmul_cat(x_re, x_im, p, conj_out):
        p_re, p_im = p
        o_re = x_re * p_re - x_im * p_im
        o_im = x_re * p_im + x_im * p_re
        return jnp.concatenate([o_re, -o_im if conj_out else o_im], axis=1)

    for t in range(CHUNK):
        rs = slice(t * PAIR_CH, (t + 1) * PAIR_CH)
        cl_s[rs, :] = cmul_cat(c_re, c_im, power(t), True)
        lb_s[rs, :] = cmul_cat(bb_re, bb_im, power(-t), False)
        cst_ref[0, rs, :] = cmul_cat(c_re, c_im, power(t + 1), True).astype(BF16)
        bst_ref[0, rs, :] = cmul_cat(bb_re, bb_im, power(CHUNK - 1 - t),
                                     False).astype(BF16)
    lre_ref[0], lim_ref[0] = power(CHUNK)

    k = lax.dot_general(cl_s[...], lb_s[...], (((1,), (1,)), ((), ())),
                        preferred_element_type=F32, precision=lax.Precision.HIGHEST)
    ri = lax.broadcasted_iota(jnp.int32, (PAIR_ROWS, PAIR_ROWS), 0)
    ci = lax.broadcasted_iota(jnp.int32, (PAIR_ROWS, PAIR_ROWS), 1)
    k = jnp.where(ri // PAIR_CH >= ci // PAIR_CH, k, 0.0)
    k = k + jnp.where(ri == ci, dl_ref[0], 0.0)
    kt_ref[0] = k.astype(BF16)


def _ssm_prep_call(a_re, a_im, b_re, b_im, c_re, c_im, d, log_dt):
    np_, pc, ps, st = N_PAIRS, PAIR_CH, PAIR_ST, SSM_STATE
    row = lambda a: a.reshape(np_, 1, ps)
    ldt = jnp.broadcast_to(log_dt[:, None], (SSM_GROUPS, st))
    d_lane = jnp.tile(d.reshape(np_, 1, pc), (1, 1, CHUNK))
    bt = lambda a: jnp.transpose(a, (0, 2, 1)).reshape(np_, pc, st)
    spec = lambda *shape: pl.BlockSpec((1,) + shape, lambda p: (p, 0, 0))
    return pl.pallas_call(
        _ssm_prep_kernel,
        grid=(np_,),
        in_specs=[spec(1, ps), spec(1, ps), spec(1, ps), spec(pc, st), spec(pc, st),
                  spec(pc, st), spec(pc, st), spec(1, PAIR_ROWS)],
        out_specs=[spec(PAIR_ROWS, PAIR_ROWS), spec(PAIR_ROWS, 2 * ps),
                   spec(PAIR_ROWS, 2 * ps), spec(1, ps), spec(1, ps)],
        out_shape=[jax.ShapeDtypeStruct((np_, PAIR_ROWS, PAIR_ROWS), BF16),
                   jax.ShapeDtypeStruct((np_, PAIR_ROWS, 2 * ps), BF16),
                   jax.ShapeDtypeStruct((np_, PAIR_ROWS, 2 * ps), BF16),
                   jax.ShapeDtypeStruct((np_, 1, ps), F32),
                   jax.ShapeDtypeStruct((np_, 1, ps), F32)],
        scratch_shapes=[pltpu.VMEM((PAIR_ROWS, 2 * ps), F32)] * 2,
        name="ssm_prep",
    )(row(a_re), row(a_im), row(ldt), c_re.reshape(np_, pc, st),
      c_im.reshape(np_, pc, st), bt(b_re), bt(b_im), d_lane)


def _ssm_in_kernel(x_hbm, mod_ref, g_ref, wt_ref, o_ref, xbuf, sems):
    n_chunks = o_ref.shape[-1]
    n_s = pl.num_programs(1)
    flat = pl.program_id(0) * n_s + pl.program_id(1)
    slot = flat % 2

    def gather(block, buf_slot):
        bb, c0 = block // n_s, (block % n_s) * n_chunks
        return [pltpu.make_async_copy(
            x_hbm.at[bb, pl.ds(c0, n_chunks), tau, :],
            xbuf.at[buf_slot, pl.ds(tau * n_chunks, n_chunks), :],
            sems.at[buf_slot]) for tau in range(CHUNK)]

    @pl.when(flat == 0)
    def _():
        for cp in gather(flat, slot):
            cp.start()

    @pl.when(flat + 1 < pl.num_programs(0) * n_s)
    def _():
        for cp in gather(flat + 1, 1 - slot):
            cp.start()

    for cp in gather(flat, slot):
        cp.wait()

    h = _rms_mod(xbuf[slot], g_ref[...], mod_ref[0, 1:2, :], mod_ref[0, 0:1, :])
    u_t = lax.dot_general(wt_ref[...].astype(BF16), h.astype(BF16),
                          (((1,), (1,)), ((), ())), preferred_element_type=F32)
    for tau in range(CHUNK):
        o_ref[0, tau] = u_t[:, tau * n_chunks:(tau + 1) * n_chunks].astype(o_ref.dtype)


def _ssm_in_call(x, mod, g, w_ssm_t, chunks_per_step):
    bsz, seq, _ = x.shape
    ts = chunks_per_step * CHUNK
    x_chunked = x.reshape(bsz, seq // CHUNK, CHUNK, D_MODEL)
    return pl.pallas_call(
        _ssm_in_kernel,
        grid=(bsz, seq // ts),
        in_specs=[
            pl.BlockSpec(memory_space=pl.ANY),
            pl.BlockSpec((1, 6, D_MODEL), lambda b, s: (b, 0, 0)),
            pl.BlockSpec((1, D_MODEL), lambda b, s: (0, 0)),
            pl.BlockSpec((SSM_WIDTH, D_MODEL), lambda b, s: (0, 0)),
        ],
        out_specs=pl.BlockSpec((1, CHUNK, SSM_WIDTH, chunks_per_step),
                               lambda b, s: (b, 0, 0, s)),
        out_shape=jax.ShapeDtypeStruct((bsz, CHUNK, SSM_WIDTH, seq // CHUNK), BF16),
        scratch_shapes=[pltpu.VMEM((2, ts, D_MODEL), F32),
                        pltpu.SemaphoreType.DMA((2,))],
        compiler_params=pltpu.CompilerParams(
            dimension_semantics=("arbitrary", "arbitrary"),
            vmem_limit_bytes=VMEM_LIMIT),
        name="ssm_in",
    )(x_chunked, mod, g, w_ssm_t)


def _ssm_kernel(u_ref, kt_ref, bst_ref, cst_ref, lre_ref, lim_ref, *rest):
    n_w = (len(rest) - 7) // 2
    w_in_refs, o_ref, w_out_refs = rest[:n_w], rest[n_w], rest[n_w + 1:2 * n_w + 1]
    sre, sim, xre, xim, ys, yn = rest[2 * n_w + 1:]
    n_chunks = u_ref.shape[-1]
    pw = PAIR_ST

    for src, dst in zip(w_in_refs, w_out_refs):
        dst[...] = src[...].astype(BF16)

    def load_pair(j):
        r = u_ref[0, :, j * PAIR_CH:(j + 1) * PAIR_CH, :]
        return r.reshape(PAIR_ROWS, n_chunks)

    for j in range(PAIRS_PER_STEP):
        s = lax.dot_general(load_pair(j), bst_ref[j], (((0,), (0,)), ((), ())),
                            preferred_element_type=F32)
        sre[:, j * pw:(j + 1) * pw] = s[:, :pw]
        sim[:, j * pw:(j + 1) * pw] = s[:, pw:]

    lre = lre_ref[0]
    lim = lim_ref[0]

    def body(c, carry):
        xr, xi = carry
        xre[pl.ds(c, 1), :] = xr
        xim[pl.ds(c, 1), :] = xi
        nr = lre * xr - lim * xi + sre[pl.ds(c, 1), :]
        ni = lre * xi + lim * xr + sim[pl.ds(c, 1), :]
        return nr, ni

    zero = jnp.zeros((1, PAIRS_PER_STEP * pw), F32)
    lax.fori_loop(0, n_chunks, body, (zero, zero), unroll=8)

    for j in range(PAIRS_PER_STEP):
        y_intra = jnp.dot(kt_ref[j], load_pair(j), preferred_element_type=F32)
        xp = jnp.concatenate([xre[:, j * pw:(j + 1) * pw],
                              xim[:, j * pw:(j + 1) * pw]], axis=1).astype(BF16)
        y_state = lax.dot_general(cst_ref[j], xp, (((1,), (1,)), ((), ())),
                                  preferred_element_type=F32)
        y = jax.nn.gelu(y_intra + y_state)
        ys[:, j * PAIR_CH:(j + 1) * PAIR_CH, :] = y.reshape(CHUNK, PAIR_CH, n_chunks)

    for t in range(CHUNK):
        yn[pl.ds(t, n_chunks, stride=CHUNK), :] = ys[t].T
    o_ref[0] = yn[...].astype(o_ref.dtype)


def _ssm_call(u_t, kt, bst, cst, lam_re, lam_im, weights):
    bsz, _, _, n_chunks = u_t.shape
    seq = n_chunks * CHUNK
    pps = PAIRS_PER_STEP
    ch = pps * PAIR_CH
    state_w = pps * PAIR_ST
    steps = N_PAIRS // pps
    lam_re = lam_re.reshape(steps, 1, state_w)
    lam_im = lam_im.reshape(steps, 1, state_w)
    n_steps = steps * bsz
    w_specs = [pl.BlockSpec((w.shape[0] // n_steps, w.shape[1]),
                            lambda p, b: (p * bsz + b, 0)) for w in weights]
    outs = pl.pallas_call(
        _ssm_kernel,
        grid=(steps, bsz),
        in_specs=[
            pl.BlockSpec((1, CHUNK, ch, n_chunks), lambda p, b: (b, 0, p, 0)),
            pl.BlockSpec((pps, PAIR_ROWS, PAIR_ROWS), lambda p, b: (p, 0, 0)),
            pl.BlockSpec((pps, PAIR_ROWS, 2 * PAIR_ST), lambda p, b: (p, 0, 0)),
            pl.BlockSpec((pps, PAIR_ROWS, 2 * PAIR_ST), lambda p, b: (p, 0, 0)),
            pl.BlockSpec((1, 1, state_w), lambda p, b: (p, 0, 0)),
            pl.BlockSpec((1, 1, state_w), lambda p, b: (p, 0, 0)),
        ] + w_specs,
        out_specs=[pl.BlockSpec((1, seq, ch), lambda p, b: (b, 0, p))] + w_specs,
        out_shape=[jax.ShapeDtypeStruct((bsz, seq, SSM_WIDTH), BF16)] + [
            jax.ShapeDtypeStruct(w.shape, BF16) for w in weights],
        scratch_shapes=[pltpu.VMEM((n_chunks, state_w), F32)] * 4 + [
            pltpu.VMEM((CHUNK, ch, n_chunks), F32), pltpu.VMEM((seq, ch), F32)],
        compiler_params=pltpu.CompilerParams(
            dimension_semantics=("arbitrary", "arbitrary"),
            vmem_limit_bytes=VMEM_LIMIT),
        name="ssm",
    )(u_t, kt, bst, cst, lam_re, lam_im, *weights)
    return outs[0], outs[1:]


def _mixer_kernel(x_ref, y_ref, mod_ref, g_ref, wc_ref, wgc0_ref, wgc1_ref, wgs0_ref,
                  wgs1_ref, cw_ref, cb_ref, lng_ref, lnb_ref, proj_ref, glu_ref,
                  wout_ref, wf_ref, o_ref, wf_o_ref, ubuf, cbuf, sbuf):
    ts = x_ref.shape[1]
    halo = CONV_HALO
    half = D_MODEL // 2
    n_lt = CONV_WIDTH // LANES

    wf_o_ref[...] = wf_ref[...].astype(BF16)

    @pl.when(pl.program_id(1) == 0)
    def _():
        ubuf[:, 0:halo, :] = jnp.zeros((n_lt, halo, LANES), F32)

    x = x_ref[0]
    h = _rms_mod(x, g_ref[...], mod_ref[0, 1:2, :], mod_ref[0, 0:1, :]).astype(BF16)

    zc = jnp.dot(h, wc_ref[...], preferred_element_type=F32)
    u = zc[:, :CONV_WIDTH] * _sigmoid(zc[:, CONV_WIDTH:])
    for k in range(n_lt):
        ubuf[k, halo:halo + ts, :] = u[:, k * LANES:(k + 1) * LANES]

    y_in = y_ref[0]

    def dot_h(w_ref):
        return jnp.dot(h, w_ref[...], preferred_element_type=F32)

    def dot_y(k):
        return jnp.dot(y_in, glu_ref[:, k * half:(k + 1) * half],
                       preferred_element_type=F32)

    gc0, gs0, za0, zb0, gc1, gs1, za1, zb1 = range(N_SIDE)

    def put(k, v):
        sbuf[k] = v

    dyn_zero = jnp.minimum(pl.program_id(1), 0)
    never = lax.broadcasted_iota(jnp.int32, (SUBLANES, LANES), 0) < dyn_zero

    def anchored(term, k):
        fence = sbuf[dyn_zero, k * SUBLANES:(k + 1) * SUBLANES, 0:LANES]
        head = jnp.where(never, fence, term[0:SUBLANES, :])
        return jnp.concatenate([head, term[SUBLANES:, :]], axis=0)

    side_jobs = [
        lambda: put(gc0, dot_h(wgc0_ref)), lambda: put(gs0, dot_h(wgs0_ref)),
        lambda: put(za0, dot_y(0)), lambda: put(zb0, dot_y(2)),
        lambda: put(gc1, dot_h(wgc1_ref)), lambda: put(gs1, dot_h(wgs1_ref)),
        lambda: put(za1, dot_y(1)), lambda: put(zb1, dot_y(3)),
    ]
    first = halo - (CONV_KERNEL - 1)
    n_rt = ts // CONV_ROWS
    for i in range(n_rt):
        for k in range(n_lt):
            lanes = slice(k * LANES, (k + 1) * LANES)
            acc = None
            for j in range(CONV_KERNEL):
                row0 = first + j + i * CONV_ROWS
                term = cw_ref[j:j + 1, lanes] * ubuf[k, row0:row0 + CONV_ROWS, :]
                if j == 0 and i > 0:
                    term = anchored(term, k)
                acc = term if acc is None else acc + term
            cbuf[i * CONV_ROWS:(i + 1) * CONV_ROWS, lanes] = acc + cb_ref[:, lanes]
        if (i + 1) * len(side_jobs) % n_rt == 0:
            side_jobs[(i + 1) * len(side_jobs) // n_rt - 1]()
    for k in range(n_lt):
        ubuf[k, 0:halo, :] = ubuf[k, ts:ts + halo, :]
    acc = cbuf[...]
    mu = jnp.mean(acc, axis=-1, keepdims=True)
    cen = acc - mu
    var = jnp.mean(cen * cen, axis=-1, keepdims=True)
    yn = cen * lax.rsqrt(var + EPS) * lng_ref[...] + lnb_ref[...]
    yn = yn * _sigmoid(yn)
    y_conv = jnp.dot(yn.astype(BF16), proj_ref[...], preferred_element_type=F32)

    def merge(conv_half, g_conv, g_ssm, z_a, z_b):
        y_ssm = sbuf[z_a] * _sigmoid(sbuf[z_b])
        return _sigmoid(sbuf[g_conv]) * conv_half + _sigmoid(sbuf[g_ssm]) * y_ssm

    merged = jnp.concatenate(
        [merge(y_conv[:, :half], gc0, gs0, za0, zb0),
         merge(y_conv[:, half:], gc1, gs1, za1, zb1)], axis=1)
    upd = jnp.dot(merged.astype(BF16), wout_ref[...], preferred_element_type=F32)
    o_ref[0] = x + mod_ref[0, 2:3, :] * upd


def _const_spec(shape, index=None):
    index = (0,) * len(shape) if index is None else index
    return pl.BlockSpec(shape, lambda b, s: index, pipeline_mode=pl.Buffered(1))


def _mixer_call(x, y, mod, g, w_in_b, cw, cb, lng, lnb, proj, glu, wout, w_ffn_in, ts):
    bsz, seq, _ = x.shape
    n_s = seq // ts
    gate0 = (2 * CONV_WIDTH + SSM_WIDTH) // GATE_BLOCK
    gate_spec = lambda i: _const_spec((D_MODEL, GATE_BLOCK), (0, gate0 + i))
    wf_rows, wf_cols = w_ffn_in.shape
    wf_spec = pl.BlockSpec((wf_rows // (bsz * n_s), wf_cols), lambda b, s: (b * n_s + s, 0))
    return pl.pallas_call(
        _mixer_kernel,
        grid=(bsz, seq // ts),
        in_specs=[
            pl.BlockSpec((1, ts, D_MODEL), lambda b, s: (b, s, 0)),
            pl.BlockSpec((1, ts, SSM_WIDTH), lambda b, s: (b, s, 0)),
            pl.BlockSpec((1, 6, D_MODEL), lambda b, s: (b, 0, 0)),
            _const_spec((1, D_MODEL)),
            _const_spec((D_MODEL, 2 * CONV_WIDTH)),
            gate_spec(0), gate_spec(1), gate_spec(2), gate_spec(3),
            _const_spec((CONV_KERNEL, CONV_WIDTH)),
            _const_spec((1, CONV_WIDTH)),
            _const_spec((1, CONV_WIDTH)),
            _const_spec((1, CONV_WIDTH)),
            _const_spec((CONV_WIDTH, D_MODEL)),
            _const_spec((SSM_WIDTH, 2 * D_MODEL)),
            _const_spec((D_MODEL, D_MODEL)),
            wf_spec,
        ],
        out_specs=[pl.BlockSpec((1, ts, D_MODEL), lambda b, s: (b, s, 0)), wf_spec],
        out_shape=[jax.ShapeDtypeStruct(x.shape, F32),
                   jax.ShapeDtypeStruct(w_ffn_in.shape, BF16)],
        scratch_shapes=[pltpu.VMEM((CONV_WIDTH // LANES, ts + CONV_HALO, LANES), F32),
                        pltpu.VMEM((ts, CONV_WIDTH), F32),
                        pltpu.VMEM((N_SIDE, ts, D_MODEL // 2), F32)],
        compiler_params=pltpu.CompilerParams(
            dimension_semantics=("arbitrary", "arbitrary"),
            vmem_limit_bytes=VMEM_LIMIT),
        name="mixer",
    )(x, y, mod, g, w_in_b, w_in_b, w_in_b, w_in_b, w_in_b, cw, cb, lng, lnb, proj, glu,
      wout, w_ffn_in)


def _ffn_kernel(x_ref, mod_ref, g_ref, wg_ref, wu_ref, wo_ref, fg_ref, o_ref):
    x = x_ref[0]
    h = _rms_mod(x, g_ref[...], mod_ref[0, 4:5, :], mod_ref[0, 3:4, :]).astype(BF16)
    fg = jnp.dot(h, wg_ref[...], preferred_element_type=F32)
    fu = jnp.dot(h, wu_ref[...], preferred_element_type=F32)
    act = (fg * _sigmoid(fg) * fu).astype(BF16)
    upd = jnp.dot(act, wo_ref[...], preferred_element_type=F32)
    x2 = x + mod_ref[0, 5:6, :] * upd
    ms = jnp.mean(x2 * x2, axis=-1, keepdims=True)
    o_ref[0] = x2 * lax.rsqrt(ms + EPS) * fg_ref[...]


def _ffn_call(x, mod, g, w_ffn_in_b, wo, final_g, ts):
    bsz, seq, _ = x.shape
    return pl.pallas_call(
        _ffn_kernel,
        grid=(bsz, seq // ts),
        in_specs=[
            pl.BlockSpec((1, ts, D_MODEL), lambda b, s: (b, s, 0)),
            pl.BlockSpec((1, 6, D_MODEL), lambda b, s: (b, 0, 0)),
            _const_spec((1, D_MODEL)),
            _const_spec((D_MODEL, FFN_HIDDEN), (0, 0)),
            _const_spec((D_MODEL, FFN_HIDDEN), (0, 1)),
            _const_spec((FFN_HIDDEN, D_MODEL)),
            _const_spec((1, D_MODEL)),
        ],
        out_specs=pl.BlockSpec((1, ts, D_MODEL), lambda b, s: (b, s, 0)),
        out_shape=jax.ShapeDtypeStruct(x.shape, F32),
        compiler_params=pltpu.CompilerParams(
            dimension_semantics=("arbitrary", "arbitrary"),
            vmem_limit_bytes=VMEM_LIMIT),
        name="ffn",
    )(x, mod, g, w_ffn_in_b, w_ffn_in_b, wo, final_g)


def kernel(x, c, w_ada, b_ada, norm1_g, w_in, conv_w, conv_b, conv_ln_g, conv_ln_b,
           conv_proj, ssm_a_re, ssm_a_im, ssm_b_re, ssm_b_im, ssm_c_re, ssm_c_im,
           ssm_d, ssm_log_dt, ssm_glu, w_out, norm2_g, w_ffn_in, w_ffn_out, final_g):
    bsz, seq, _ = x.shape
    assert w_ada.shape[0] == 1, "single layer only"

    mod = _mod_call(c, w_ada[0], b_ada[0])
    mod = jnp.transpose(mod, (1, 0, 2))

    g1 = norm1_g[0].reshape(1, -1)

    c0 = 2 * CONV_WIDTH
    w_ssm_t = jnp.transpose(w_in[0][:, c0:c0 + SSM_WIDTH])
    u_t = _ssm_in_call(x, mod, g1, w_ssm_t, 128)
    ops = _ssm_prep_call(ssm_a_re[0], ssm_a_im[0], ssm_b_re[0], ssm_b_im[0],
                         ssm_c_re[0], ssm_c_im[0], ssm_d[0], ssm_log_dt[0])
    y, (w_in_b, proj_b, glu_b, w_out_b, w_ffn_out_b) = _ssm_call(
        u_t, *ops, (w_in[0], conv_proj[0], ssm_glu[0], w_out[0], w_ffn_out[0]))

    x1, w_ffn_in_b = _mixer_call(
        x, y, mod, g1, w_in_b, conv_w[0], conv_b[0].reshape(1, -1),
        conv_ln_g[0].reshape(1, -1), conv_ln_b[0].reshape(1, -1),
        proj_b, glu_b, w_out_b, w_ffn_in[0], 512)

    return _ffn_call(x1, mod, norm2_g[0].reshape(1, -1), w_ffn_in_b, w_ffn_out_b,
                     final_g.reshape(1, -1), 512)
```

```python
import jax
import jax.numpy as jnp
from jax import lax
from jax.experimental import pallas as pl
from jax.experimental.pallas import tpu as pltpu

F32 = jnp.float32
BF16 = jnp.bfloat16

D_MODEL = 1024
CONV_WIDTH = 512
CONV_KERNEL = 31
SSM_WIDTH = 512
SSM_GROUP = 16
SSM_GROUPS = 32
SSM_STATE = 64
FFN_HIDDEN = 2816
EPS = 1e-6

CHUNK = 16
PAIR_CH = 2 * SSM_GROUP
PAIR_ST = 2 * SSM_STATE
PAIR_ROWS = CHUNK * PAIR_CH
N_PAIRS = SSM_GROUPS // 2
PAIRS_PER_STEP = 4
CONV_HALO = 32
GATE_BLOCK = 512
LANES = 128
SUBLANES = 8
CONV_ROWS = 64
N_SIDE = 8

VMEM_LIMIT = 56 * 1024 * 1024


def _rms_mod(x, g, scale, shift):
    ms = jnp.mean(x * x, axis=-1, keepdims=True)
    y = x * lax.rsqrt(ms + EPS)
    return (y * g) * (1.0 + scale) + shift


def _sigmoid(v):
    return jax.nn.sigmoid(v)


def _split_bf16(v):
    hi = v.astype(BF16)
    return hi, (v - hi.astype(F32)).astype(BF16)


def _prep_kernel(c_ref, wada_ref, bada_ref, ar_ref, ai_ref, ldt_ref, cre_ref, cim_ref,
                 bre_ref, bim_ref, dl_ref, mod_ref, kt_ref, bst_ref, cst_ref, lre_ref,
                 lim_ref, cl_s, lb_s):
    c = c_ref[...]
    ca = (c * _sigmoid(c)).astype(BF16)
    mod_ref[...] = jnp.dot(ca, wada_ref[...].astype(BF16),
                           preferred_element_type=F32) + bada_ref[...]

    ar, ai = ar_ref[0], ai_ref[0]
    dt = jnp.exp(ldt_ref[0])
    dar, dai = dt * ar, dt * ai
    mag, cos1, sin1 = jnp.exp(dar), jnp.cos(dai), jnp.sin(dai)
    n_re, n_im = mag * cos1 - 1.0, mag * sin1
    den = ar * ar + ai * ai
    q_re = (n_re * ar + n_im * ai) / den
    q_im = (n_im * ar - n_re * ai) / den

    row = lax.broadcasted_iota(jnp.int32, (PAIR_CH, PAIR_ST), 0)
    lane = lax.broadcasted_iota(jnp.int32, (PAIR_CH, PAIR_ST), 1)
    same_group = (row // SSM_GROUP) == (lane // SSM_STATE)

    def block_diag(ref):
        v = ref[0]
        return jnp.where(same_group, jnp.concatenate([v, v], axis=1), 0.0)

    c_re, c_im = block_diag(cre_ref), block_diag(cim_ref)
    b_re, b_im = block_diag(bre_ref), block_diag(bim_ref)
    bb_re = q_re * b_re - q_im * b_im
    bb_im = q_re * b_im + q_im * b_re

    inv_mag = jnp.exp(-dar)
    one = jnp.ones_like(dar)
    powers = {0: (one, jnp.zeros_like(dar)), 1: (mag * cos1, mag * sin1),
              -1: (inv_mag * cos1, -(inv_mag * sin1))}

    def power(k):
        if k not in powers:
            step = 1 if k > 0 else -1
            (p_re, p_im), (s_re, s_im) = power(k - step), powers[step]
            powers[k] = (p_re * s_re - p_im * s_im, p_re * s_im + p_im * s_re)
        return powers[k]

    def cmul_cat(x_re, x_im, p, conj_out):
        p_re, p_im = p
        o_re = x_re * p_re - x_im * p_im
        o_im = x_re * p_im + x_im * p_re
        return jnp.concatenate([o_re, -o_im if conj_out else o_im], axis=1)

    for t in range(CHUNK):
        rs = slice(t * PAIR_CH, (t + 1) * PAIR_CH)
        cl_s[rs, :] = cmul_cat(c_re, c_im, power(t), True)
        lb_s[rs, :] = cmul_cat(bb_re, bb_im, power(-t), False)
        cst_ref[0, rs, :] = cmul_cat(c_re, c_im, power(t + 1), True).astype(BF16)
        bst_ref[0, rs, :] = cmul_cat(bb_re, bb_im, power(CHUNK - 1 - t),
                                     False).astype(BF16)
    lre_ref[0], lim_ref[0] = power(CHUNK)

    cl_hi, cl_lo = _split_bf16(cl_s[...])
    lb_hi, lb_lo = _split_bf16(lb_s[...])

    def dot_nt(a, b):
        return lax.dot_general(a, b, (((1,), (1,)), ((), ())), preferred_element_type=F32)

    k = dot_nt(cl_hi, lb_hi) + (dot_nt(cl_hi, lb_lo) + dot_nt(cl_lo, lb_hi))
    ri = lax.broadcasted_iota(jnp.int32, (PAIR_ROWS, PAIR_ROWS), 0)
    ci = lax.broadcasted_iota(jnp.int32, (PAIR_ROWS, PAIR_ROWS), 1)
    k = jnp.where(ri // PAIR_CH >= ci // PAIR_CH, k, 0.0)
    k = k + jnp.where(ri == ci, dl_ref[0], 0.0)
    kt_ref[0] = k.astype(BF16)


def _prep_call(c, w_ada, b_ada, a_re, a_im, b_re, b_im, c_re, c_im, d, log_dt):
    np_, pc, ps, st = N_PAIRS, PAIR_CH, PAIR_ST, SSM_STATE
    bsz = c.shape[0]
    mod_cols = w_ada.shape[1]
    mod_blk = mod_cols // np_
    row = lambda a: a.reshape(np_, 1, ps)
    ldt = jnp.broadcast_to(log_dt[:, None], (SSM_GROUPS, st))
    d_lane = jnp.tile(d.reshape(np_, 1, pc), (1, 1, CHUNK))
    bt = lambda a: jnp.transpose(a, (0, 2, 1)).reshape(np_, pc, st)
    spec = lambda *shape: pl.BlockSpec((1,) + shape, lambda p: (p, 0, 0))
    col_spec = lambda rows: pl.BlockSpec((rows, mod_blk), lambda p: (0, p))
    return pl.pallas_call(
        _prep_kernel,
        grid=(np_,),
        in_specs=[pl.BlockSpec((bsz, D_MODEL), lambda p: (0, 0)),
                  col_spec(D_MODEL), col_spec(1),
                  spec(1, ps), spec(1, ps), spec(1, ps), spec(pc, st), spec(pc, st),
                  spec(pc, st), spec(pc, st), spec(1, PAIR_ROWS)],
        out_specs=[col_spec(bsz),
                   spec(PAIR_ROWS, PAIR_ROWS), spec(PAIR_ROWS, 2 * ps),
                   spec(PAIR_ROWS, 2 * ps), spec(1, ps), spec(1, ps)],
        out_shape=[jax.ShapeDtypeStruct((bsz, mod_cols), F32),
                   jax.ShapeDtypeStruct((np_, PAIR_ROWS, PAIR_ROWS), BF16),
                   jax.ShapeDtypeStruct((np_, PAIR_ROWS, 2 * ps), BF16),
                   jax.ShapeDtypeStruct((np_, PAIR_ROWS, 2 * ps), BF16),
                   jax.ShapeDtypeStruct((np_, 1, ps), F32),
                   jax.ShapeDtypeStruct((np_, 1, ps), F32)],
        scratch_shapes=[pltpu.VMEM((PAIR_ROWS, 2 * ps), F32)] * 2,
        name="prep",
    )(c, w_ada, b_ada.reshape(1, -1), row(a_re), row(a_im), row(ldt),
      c_re.reshape(np_, pc, st), c_im.reshape(np_, pc, st), bt(b_re), bt(b_im), d_lane)


def _ssm_in_kernel(x_hbm, mod_ref, g_ref, wt_ref, o_ref, xbuf, sems):
    n_chunks = o_ref.shape[-1]
    n_s = pl.num_programs(1)
    flat = pl.program_id(0) * n_s + pl.program_id(1)
    slot = flat % 2

    def gather(block, buf_slot):
        bb, c0 = block // n_s, (block % n_s) * n_chunks
        return [pltpu.make_async_copy(
            x_hbm.at[bb, pl.ds(c0, n_chunks), tau, :],
            xbuf.at[buf_slot, pl.ds(tau * n_chunks, n_chunks), :],
            sems.at[buf_slot]) for tau in range(CHUNK)]

    @pl.when(flat == 0)
    def _():
        for cp in gather(flat, slot):
            cp.start()

    @pl.when(flat + 1 < pl.num_programs(0) * n_s)
    def _():
        for cp in gather(flat + 1, 1 - slot):
            cp.start()

    for cp in gather(flat, slot):
        cp.wait()

    h = _rms_mod(xbuf[slot], g_ref[...], mod_ref[0, 1:2, :], mod_ref[0, 0:1, :])
    u_t = lax.dot_general(wt_ref[...].astype(BF16), h.astype(BF16),
                          (((1,), (1,)), ((), ())), preferred_element_type=F32)
    for tau in range(CHUNK):
        o_ref[0, tau] = u_t[:, tau * n_chunks:(tau + 1) * n_chunks].astype(o_ref.dtype)


def _ssm_in_call(x, mod, g, w_ssm_t, chunks_per_step):
    bsz, seq, _ = x.shape
    ts = chunks_per_step * CHUNK
    x_chunked = x.reshape(bsz, seq // CHUNK, CHUNK, D_MODEL)
    return pl.pallas_call(
        _ssm_in_kernel,
        grid=(bsz, seq // ts),
        in_specs=[
            pl.BlockSpec(memory_space=pl.ANY),
            pl.BlockSpec((1, 6, D_MODEL), lambda b, s: (b, 0, 0)),
            pl.BlockSpec((1, D_MODEL), lambda b, s: (0, 0)),
            pl.BlockSpec((SSM_WIDTH, D_MODEL), lambda b, s: (0, 0)),
        ],
        out_specs=pl.BlockSpec((1, CHUNK, SSM_WIDTH, chunks_per_step),
                               lambda b, s: (b, 0, 0, s)),
        out_shape=jax.ShapeDtypeStruct((bsz, CHUNK, SSM_WIDTH, seq // CHUNK), BF16),
        scratch_shapes=[pltpu.VMEM((2, ts, D_MODEL), F32),
                        pltpu.SemaphoreType.DMA((2,))],
        compiler_params=pltpu.CompilerParams(
            dimension_semantics=("arbitrary", "arbitrary"),
            vmem_limit_bytes=VMEM_LIMIT),
        name="ssm_in",
    )(x_chunked, mod, g, w_ssm_t)


def _ssm_kernel(u_ref, kt_ref, bst_ref, cst_ref, lre_ref, lim_ref, *rest):
    n_w = (len(rest) - 7) // 2
    w_in_refs, o_ref, w_out_refs = rest[:n_w], rest[n_w], rest[n_w + 1:2 * n_w + 1]
    sre, sim, xre, xim, ys, yn = rest[2 * n_w + 1:]
    n_chunks = u_ref.shape[-1]
    pw = PAIR_ST

    for src, dst in zip(w_in_refs, w_out_refs):
        dst[...] = src[...].astype(BF16)

    def load_pair(j):
        r = u_ref[0, :, j * PAIR_CH:(j + 1) * PAIR_CH, :]
        return r.reshape(PAIR_ROWS, n_chunks)

    for j in range(PAIRS_PER_STEP):
        s = lax.dot_general(load_pair(j), bst_ref[j], (((0,), (0,)), ((), ())),
                            preferred_element_type=F32)
        sre[:, j * pw:(j + 1) * pw] = s[:, :pw]
        sim[:, j * pw:(j + 1) * pw] = s[:, pw:]

    lre = lre_ref[0]
    lim = lim_ref[0]

    def body(c, carry):
        xr, xi = carry
        xre[pl.ds(c, 1), :] = xr
        xim[pl.ds(c, 1), :] = xi
        nr = lre * xr - lim * xi + sre[pl.ds(c, 1), :]
        ni = lre * xi + lim * xr + sim[pl.ds(c, 1), :]
        return nr, ni

    zero = jnp.zeros((1, PAIRS_PER_STEP * pw), F32)
    lax.fori_loop(0, n_chunks, body, (zero, zero), unroll=8)

    for j in range(PAIRS_PER_STEP):
        y_intra = jnp.dot(kt_ref[j], load_pair(j), preferred_element_type=F32)
        xp = jnp.concatenate([xre[:, j * pw:(j + 1) * pw],
                              xim[:, j * pw:(j + 1) * pw]], axis=1).astype(BF16)
        y_state = lax.dot_general(cst_ref[j], xp, (((1,), (1,)), ((), ())),
                                  preferred_element_type=F32)
        y = jax.nn.gelu(y_intra + y_state)
        ys[:, j * PAIR_CH:(j + 1) * PAIR_CH, :] = y.reshape(CHUNK, PAIR_CH, n_chunks)

    for t in range(CHUNK):
        yn[pl.ds(t, n_chunks, stride=CHUNK), :] = ys[t].T
    o_ref[0] = yn[...].astype(o_ref.dtype)


def _ssm_call(u_t, kt, bst, cst, lam_re, lam_im, weights):
    bsz, _, _, n_chunks = u_t.shape
    seq = n_chunks * CHUNK
    pps = PAIRS_PER_STEP
    ch = pps * PAIR_CH
    state_w = pps * PAIR_ST
    steps = N_PAIRS // pps
    lam_re = lam_re.reshape(steps, 1, state_w)
    lam_im = lam_im.reshape(steps, 1, state_w)
    n_steps = steps * bsz
    w_specs = [pl.BlockSpec((w.shape[0] // n_steps, w.shape[1]),
                            lambda p, b: (p * bsz + b, 0)) for w in weights]
    outs = pl.pallas_call(
        _ssm_kernel,
        grid=(steps, bsz),
        in_specs=[
            pl.BlockSpec((1, CHUNK, ch, n_chunks), lambda p, b: (b, 0, p, 0)),
            pl.BlockSpec((pps, PAIR_ROWS, PAIR_ROWS), lambda p, b: (p, 0, 0)),
            pl.BlockSpec((pps, PAIR_ROWS, 2 * PAIR_ST), lambda p, b: (p, 0, 0)),
            pl.BlockSpec((pps, PAIR_ROWS, 2 * PAIR_ST), lambda p, b: (p, 0, 0)),
            pl.BlockSpec((1, 1, state_w), lambda p, b: (p, 0, 0)),
            pl.BlockSpec((1, 1, state_w), lambda p, b: (p, 0, 0)),
        ] + w_specs,
        out_specs=[pl.BlockSpec((1, seq, ch), lambda p, b: (b, 0, p))] + w_specs,
        out_shape=[jax.ShapeDtypeStruct((bsz, seq, SSM_WIDTH), BF16)] + [
            jax.ShapeDtypeStruct(w.shape, BF16) for w in weights],
        scratch_shapes=[pltpu.VMEM((n_chunks, state_w), F32)] * 4 + [
            pltpu.VMEM((CHUNK, ch, n_chunks), F32), pltpu.VMEM((seq, ch), F32)],
        compiler_params=pltpu.CompilerParams(
            dimension_semantics=("arbitrary", "arbitrary"),
            vmem_limit_bytes=VMEM_LIMIT),
        name="ssm",
    )(u_t, kt, bst, cst, lam_re, lam_im, *weights)
    return outs[0], outs[1:]


def _mixer_kernel(x_ref, y_ref, mod_ref, g_ref, wc_ref, wgc0_ref, wgc1_ref, wgs0_ref,
                  wgs1_ref, cw_ref, cb_ref, lng_ref, lnb_ref, proj_ref, glu_ref,
                  wout_ref, wf_ref, o_ref, wf_o_ref, ubuf, cbuf, sbuf):
    ts = x_ref.shape[1]
    halo = CONV_HALO
    half = D_MODEL // 2
    n_lt = CONV_WIDTH // LANES

    wf_o_ref[...] = wf_ref[...].astype(BF16)

    @pl.when(pl.program_id(1) == 0)
    def _():
        ubuf[:, 0:halo, :] = jnp.zeros((n_lt, halo, LANES), F32)

    x = x_ref[0]
    h = _rms_mod(x, g_ref[...], mod_ref[0, 1:2, :], mod_ref[0, 0:1, :]).astype(BF16)

    zc = jnp.dot(h, wc_ref[...], preferred_element_type=F32)
    u = zc[:, :CONV_WIDTH] * _sigmoid(zc[:, CONV_WIDTH:])
    for k in range(n_lt):
        ubuf[k, halo:halo + ts, :] = u[:, k * LANES:(k + 1) * LANES]

    y_in = y_ref[0]

    def dot_h(w_ref):
        return jnp.dot(h, w_ref[...], preferred_element_type=F32)

    def dot_y(k):
        return jnp.dot(y_in, glu_ref[:, k * half:(k + 1) * half],
                       preferred_element_type=F32)

    gc0, gs0, za0, zb0, gc1, gs1, za1, zb1 = range(N_SIDE)

    def put(k, v):
        sbuf[k] = v

    dyn_zero = jnp.minimum(pl.program_id(1), 0)
    never = lax.broadcasted_iota(jnp.int32, (SUBLANES, LANES), 0) < dyn_zero

    def anchored(term, k):
        fence = sbuf[dyn_zero, k * SUBLANES:(k + 1) * SUBLANES, 0:LANES]
        head = jnp.where(never, fence, term[0:SUBLANES, :])
        return jnp.concatenate([head, term[SUBLANES:, :]], axis=0)

    side_jobs = [
        lambda: put(gc0, dot_h(wgc0_ref)), lambda: put(gs0, dot_h(wgs0_ref)),
        lambda: put(za0, dot_y(0)), lambda: put(zb0, dot_y(2)),
        lambda: put(gc1, dot_h(wgc1_ref)), lambda: put(gs1, dot_h(wgs1_ref)),
        lambda: put(za1, dot_y(1)), lambda: put(zb1, dot_y(3)),
    ]
    first = halo - (CONV_KERNEL - 1)
    n_rt = ts // CONV_ROWS
    for i in range(n_rt):
        for k in range(n_lt):
            lanes = slice(k * LANES, (k + 1) * LANES)
            acc = None
            for j in range(CONV_KERNEL):
                row0 = first + j + i * CONV_ROWS
                term = cw_ref[j:j + 1, lanes] * ubuf[k, row0:row0 + CONV_ROWS, :]
                if j == 0 and i > 0:
                    term = anchored(term, k)
                acc = term if acc is None else acc + term
            cbuf[i * CONV_ROWS:(i + 1) * CONV_ROWS, lanes] = acc + cb_ref[:, lanes]
        if (i + 1) * len(side_jobs) % n_rt == 0:
            side_jobs[(i + 1) * len(side_jobs) // n_rt - 1]()
    for k in range(n_lt):
        ubuf[k, 0:halo, :] = ubuf[k, ts:ts + halo, :]
    acc = cbuf[...]
    mu = jnp.mean(acc, axis=-1, keepdims=True)
    cen = acc - mu
    var = jnp.mean(cen * cen, axis=-1, keepdims=True)
    yn = cen * lax.rsqrt(var + EPS) * lng_ref[...] + lnb_ref[...]
    yn = yn * _sigmoid(yn)
    y_conv = jnp.dot(yn.astype(BF16), proj_ref[...], preferred_element_type=F32)

    def merge(conv_half, g_conv, g_ssm, z_a, z_b):
        y_ssm = sbuf[z_a] * _sigmoid(sbuf[z_b])
        return _sigmoid(sbuf[g_conv]) * conv_half + _sigmoid(sbuf[g_ssm]) * y_ssm

    merged = jnp.concatenate(
        [merge(y_conv[:, :half], gc0, gs0, za0, zb0),
         merge(y_conv[:, half:], gc1, gs1, za1, zb1)], axis=1)
    upd = jnp.dot(merged.astype(BF16), wout_ref[...], preferred_element_type=F32)
    o_ref[0] = x + mod_ref[0, 2:3, :] * upd


def _const_spec(shape, index=None):
    index = (0,) * len(shape) if index is None else index
    return pl.BlockSpec(shape, lambda b, s: index, pipeline_mode=pl.Buffered(1))


def _mixer_call(x, y, mod, g, w_in_b, cw, cb, lng, lnb, proj, glu, wout, w_ffn_in, ts):
    bsz, seq, _ = x.shape
    n_s = seq // ts
    gate0 = (2 * CONV_WIDTH + SSM_WIDTH) // GATE_BLOCK
    gate_spec = lambda i: _const_spec((D_MODEL, GATE_BLOCK), (0, gate0 + i))
    wf_rows, wf_cols = w_ffn_in.shape
    wf_spec = pl.BlockSpec((wf_rows // (bsz * n_s), wf_cols), lambda b, s: (b * n_s + s, 0))
    return pl.pallas_call(
        _mixer_kernel,
        grid=(bsz, seq // ts),
        in_specs=[
            pl.BlockSpec((1, ts, D_MODEL), lambda b, s: (b, s, 0)),
            pl.BlockSpec((1, ts, SSM_WIDTH), lambda b, s: (b, s, 0)),
            pl.BlockSpec((1, 6, D_MODEL), lambda b, s: (b, 0, 0)),
            _const_spec((1, D_MODEL)),
            _const_spec((D_MODEL, 2 * CONV_WIDTH)),
            gate_spec(0), gate_spec(1), gate_spec(2), gate_spec(3),
            _const_spec((CONV_KERNEL, CONV_WIDTH)),
            _const_spec((1, CONV_WIDTH)),
            _const_spec((1, CONV_WIDTH)),
            _const_spec((1, CONV_WIDTH)),
            _const_spec((CONV_WIDTH, D_MODEL)),
            _const_spec((SSM_WIDTH, 2 * D_MODEL)),
            _const_spec((D_MODEL, D_MODEL)),
            wf_spec,
        ],
        out_specs=[pl.BlockSpec((1, ts, D_MODEL), lambda b, s: (b, s, 0)), wf_spec],
        out_shape=[jax.ShapeDtypeStruct(x.shape, F32),
                   jax.ShapeDtypeStruct(w_ffn_in.shape, BF16)],
        scratch_shapes=[pltpu.VMEM((CONV_WIDTH // LANES, ts + CONV_HALO, LANES), F32),
                        pltpu.VMEM((ts, CONV_WIDTH), F32),
                        pltpu.VMEM((N_SIDE, ts, D_MODEL // 2), F32)],
        compiler_params=pltpu.CompilerParams(
            dimension_semantics=("arbitrary", "arbitrary"),
            vmem_limit_bytes=VMEM_LIMIT),
        name="mixer",
    )(x, y, mod, g, w_in_b, w_in_b, w_in_b, w_in_b, w_in_b, cw, cb, lng, lnb, proj, glu,
      wout, w_ffn_in)


def _ffn_kernel(x_ref, mod_ref, g_ref, wg_ref, wu_ref, wo_ref, fg_ref, o_ref):
    x = x_ref[0]
    h = _rms_mod(x, g_ref[...], mod_ref[0, 4:5, :], mod_ref[0, 3:4, :]).astype(BF16)
    fg = jnp.dot(h, wg_ref[...], preferred_element_type=F32)
    fu = jnp.dot(h, wu_ref[...], preferred_element_type=F32)
    act = (fg * _sigmoid(fg) * fu).astype(BF16)
    upd = jnp.dot(act, wo_ref[...], preferred_element_type=F32)
    x2 = x + mod_ref[0, 5:6, :] * upd
    ms = jnp.mean(x2 * x2, axis=-1, keepdims=True)
    o_ref[0] = x2 * lax.rsqrt(ms + EPS) * fg_ref[...]


def _ffn_call(x, mod, g, w_ffn_in_b, wo, final_g, ts):
    bsz, seq, _ = x.shape
    return pl.pallas_call(
        _ffn_kernel,
        grid=(bsz, seq // ts),
        in_specs=[
            pl.BlockSpec((1, ts, D_MODEL), lambda b, s: (b, s, 0)),
            pl.BlockSpec((1, 6, D_MODEL), lambda b, s: (b, 0, 0)),
            _const_spec((1, D_MODEL)),
            _const_spec((D_MODEL, FFN_HIDDEN), (0, 0)),
            _const_spec((D_MODEL, FFN_HIDDEN), (0, 1)),
            _const_spec((FFN_HIDDEN, D_MODEL)),
            _const_spec((1, D_MODEL)),
        ],
        out_specs=pl.BlockSpec((1, ts, D_MODEL), lambda b, s: (b, s, 0)),
        out_shape=jax.ShapeDtypeStruct(x.shape, F32),
        compiler_params=pltpu.CompilerParams(
            dimension_semantics=("arbitrary", "arbitrary"),
            vmem_limit_bytes=VMEM_LIMIT),
        name="ffn",
    )(x, mod, g, w_ffn_in_b, w_ffn_in_b, wo, final_g)


def kernel(x, c, w_ada, b_ada, norm1_g, w_in, conv_w, conv_b, conv_ln_g, conv_ln_b,
           conv_proj, ssm_a_re, ssm_a_im, ssm_b_re, ssm_b_im, ssm_c_re, ssm_c_im,
           ssm_d, ssm_log_dt, ssm_glu, w_out, norm2_g, w_ffn_in, w_ffn_out, final_g):
    bsz, seq, _ = x.shape
    assert w_ada.shape[0] == 1, "single layer only"

    mod, *ops = _prep_call(c, w_ada[0], b_ada[0], ssm_a_re[0], ssm_a_im[0], ssm_b_re[0],
                           ssm_b_im[0], ssm_c_re[0], ssm_c_im[0], ssm_d[0],
                           ssm_log_dt[0])
    mod = mod.reshape(bsz, -1, D_MODEL)

    g1 = norm1_g[0].reshape(1, -1)

    c0 = 2 * CONV_WIDTH
    w_ssm_t = jnp.transpose(w_in[0][:, c0:c0 + SSM_WIDTH])
    u_t = _ssm_in_call(x, mod, g1, w_ssm_t, 128)
    y, (w_in_b, proj_b, glu_b, w_out_b, w_ffn_out_b) = _ssm_call(
        u_t, *ops, (w_in[0], conv_proj[0], ssm_glu[0], w_out[0], w_ffn_out[0]))

    x1, w_ffn_in_b = _mixer_call(
        x, y, mod, g1, w_in_b, conv_w[0], conv_b[0].reshape(1, -1),
        conv_ln_g[0].reshape(1, -1), conv_ln_b[0].reshape(1, -1),
        proj_b, glu_b, w_out_b, w_ffn_in[0], 512)

    return _ffn_call(x1, mod, norm2_g[0].reshape(1, -1), w_ffn_in_b, w_ffn_out_b,
                     final_g.reshape(1, -1), 512)
```

```python
import jax
import jax.numpy as jnp
from jax import lax
from jax.experimental import pallas as pl
from jax.experimental.pallas import tpu as pltpu

F32 = jnp.float32
BF16 = jnp.bfloat16

D_MODEL = 1024
CONV_WIDTH = 512
CONV_KERNEL = 31
SSM_WIDTH = 512
SSM_GROUP = 16
SSM_GROUPS = 32
SSM_STATE = 64
FFN_HIDDEN = 2816
EPS = 1e-6

CHUNK = 16
PAIR_CH = 2 * SSM_GROUP
PAIR_ST = 2 * SSM_STATE
PAIR_ROWS = CHUNK * PAIR_CH
N_PAIRS = SSM_GROUPS // 2
PAIRS_PER_STEP = 4
CONV_HALO = 32
GATE_BLOCK = 512
LANES = 128
SUBLANES = 8
CONV_ROWS = 64
FFN_CHUNK = 256
N_SIDE = 8

VMEM_LIMIT = 56 * 1024 * 1024


def _rms_mod(x, g, scale, shift):
    ms = jnp.mean(x * x, axis=-1, keepdims=True)
    return (x * lax.rsqrt(ms + EPS)) * (g * (1.0 + scale)) + shift


def _sigmoid(v):
    return jax.nn.sigmoid(v)


def _split_bf16(v):
    hi = v.astype(BF16)
    return hi, (v - hi.astype(F32)).astype(BF16)


def _prep_kernel(c_ref, wada_ref, bada_ref, ar_ref, ai_ref, ldt_ref, cre_ref, cim_ref,
                 bre_ref, bim_ref, dl_ref, mod_ref, kt_ref, bst_ref, cst_ref, lre_ref,
                 lim_ref, cl_s, lb_s):
    c = c_ref[...]
    ca = (c * _sigmoid(c)).astype(BF16)
    mod_ref[...] = jnp.dot(ca, wada_ref[...].astype(BF16),
                           preferred_element_type=F32) + bada_ref[...]

    ar, ai = ar_ref[0], ai_ref[0]
    dt = jnp.exp(ldt_ref[0])
    dar, dai = dt * ar, dt * ai
    mag, cos1, sin1 = jnp.exp(dar), jnp.cos(dai), jnp.sin(dai)
    n_re, n_im = mag * cos1 - 1.0, mag * sin1
    den = ar * ar + ai * ai
    q_re = (n_re * ar + n_im * ai) / den
    q_im = (n_im * ar - n_re * ai) / den

    row = lax.broadcasted_iota(jnp.int32, (PAIR_CH, PAIR_ST), 0)
    lane = lax.broadcasted_iota(jnp.int32, (PAIR_CH, PAIR_ST), 1)
    same_group = (row // SSM_GROUP) == (lane // SSM_STATE)

    def block_diag(ref):
        v = ref[0]
        return jnp.where(same_group, jnp.concatenate([v, v], axis=1), 0.0)

    c_re, c_im = block_diag(cre_ref), block_diag(cim_ref)
    b_re, b_im = block_diag(bre_ref), block_diag(bim_ref)
    bb_re = q_re * b_re - q_im * b_im
    bb_im = q_re * b_im + q_im * b_re

    inv_mag = jnp.exp(-dar)
    one = jnp.ones_like(dar)
    powers = {0: (one, jnp.zeros_like(dar)), 1: (mag * cos1, mag * sin1),
              -1: (inv_mag * cos1, -(inv_mag * sin1))}

    def power(k):
        if k not in powers:
            step = 1 if k > 0 else -1
            (p_re, p_im), (s_re, s_im) = power(k - step), powers[step]
            powers[k] = (p_re * s_re - p_im * s_im, p_re * s_im + p_im * s_re)
        return powers[k]

    def cmul_cat(x_re, x_im, p, conj_out):
        p_re, p_im = p
        o_re = x_re * p_re - x_im * p_im
        o_im = x_re * p_im + x_im * p_re
        return jnp.concatenate([o_re, -o_im if conj_out else o_im], axis=1)

    for t in range(CHUNK):
        rs = slice(t * PAIR_CH, (t + 1) * PAIR_CH)
        cl_s[rs, :] = cmul_cat(c_re, c_im, power(t), True)
        lb_s[rs, :] = cmul_cat(bb_re, bb_im, power(-t), False)
        cst_ref[0, rs, :] = cmul_cat(c_re, c_im, power(t + 1), True).astype(BF16)
        bst_ref[0, rs, :] = cmul_cat(bb_re, bb_im, power(CHUNK - 1 - t),
                                     False).astype(BF16)
    lre_ref[0], lim_ref[0] = power(CHUNK)

    cl_hi, cl_lo = _split_bf16(cl_s[...])
    lb_hi, lb_lo = _split_bf16(lb_s[...])

    def dot_nt(a, b):
        return lax.dot_general(a, b, (((1,), (1,)), ((), ())), preferred_element_type=F32)

    k = dot_nt(cl_hi, lb_hi) + (dot_nt(cl_hi, lb_lo) + dot_nt(cl_lo, lb_hi))
    ri = lax.broadcasted_iota(jnp.int32, (PAIR_ROWS, PAIR_ROWS), 0)
    ci = lax.broadcasted_iota(jnp.int32, (PAIR_ROWS, PAIR_ROWS), 1)
    k = jnp.where(ri // PAIR_CH >= ci // PAIR_CH, k, 0.0)
    k = k + jnp.where(ri == ci, dl_ref[0], 0.0)
    kt_ref[0] = k.astype(BF16)


def _prep_call(c, w_ada, b_ada, a_re, a_im, b_re, b_im, c_re, c_im, d, log_dt):
    np_, pc, ps, st = N_PAIRS, PAIR_CH, PAIR_ST, SSM_STATE
    bsz = c.shape[0]
    mod_cols = w_ada.shape[1]
    mod_blk = mod_cols // np_
    row = lambda a: a.reshape(np_, 1, ps)
    ldt = jnp.broadcast_to(log_dt[:, None], (SSM_GROUPS, st))
    d_lane = jnp.tile(d.reshape(np_, 1, pc), (1, 1, CHUNK))
    bt = lambda a: jnp.transpose(a, (0, 2, 1)).reshape(np_, pc, st)
    spec = lambda *shape: pl.BlockSpec((1,) + shape, lambda p: (p, 0, 0))
    col_spec = lambda rows: pl.BlockSpec((rows, mod_blk), lambda p: (0, p))
    return pl.pallas_call(
        _prep_kernel,
        grid=(np_,),
        in_specs=[pl.BlockSpec((bsz, D_MODEL), lambda p: (0, 0)),
                  col_spec(D_MODEL), col_spec(1),
                  spec(1, ps), spec(1, ps), spec(1, ps), spec(pc, st), spec(pc, st),
                  spec(pc, st), spec(pc, st), spec(1, PAIR_ROWS)],
        out_specs=[col_spec(bsz),
                   spec(PAIR_ROWS, PAIR_ROWS), spec(PAIR_ROWS, 2 * ps),
                   spec(PAIR_ROWS, 2 * ps), spec(1, ps), spec(1, ps)],
        out_shape=[jax.ShapeDtypeStruct((bsz, mod_cols), F32),
                   jax.ShapeDtypeStruct((np_, PAIR_ROWS, PAIR_ROWS), BF16),
                   jax.ShapeDtypeStruct((np_, PAIR_ROWS, 2 * ps), BF16),
                   jax.ShapeDtypeStruct((np_, PAIR_ROWS, 2 * ps), BF16),
                   jax.ShapeDtypeStruct((np_, 1, ps), F32),
                   jax.ShapeDtypeStruct((np_, 1, ps), F32)],
        scratch_shapes=[pltpu.VMEM((PAIR_ROWS, 2 * ps), F32)] * 2,
        name="prep",
    )(c, w_ada, b_ada.reshape(1, -1), row(a_re), row(a_im), row(ldt),
      c_re.reshape(np_, pc, st), c_im.reshape(np_, pc, st), bt(b_re), bt(b_im), d_lane)


def _ssm_in_kernel(x_hbm, mod_ref, g_ref, wt_ref, o_ref, xbuf, sems):
    n_chunks = o_ref.shape[-1]
    n_s = pl.num_programs(1)
    flat = pl.program_id(0) * n_s + pl.program_id(1)
    slot = flat % 2

    def gather(block, buf_slot):
        bb, c0 = block // n_s, (block % n_s) * n_chunks
        return [pltpu.make_async_copy(
            x_hbm.at[bb, pl.ds(c0, n_chunks), tau, :],
            xbuf.at[buf_slot, pl.ds(tau * n_chunks, n_chunks), :],
            sems.at[buf_slot]) for tau in range(CHUNK)]

    @pl.when(flat == 0)
    def _():
        for cp in gather(flat, slot):
            cp.start()

    @pl.when(flat + 1 < pl.num_programs(0) * n_s)
    def _():
        for cp in gather(flat + 1, 1 - slot):
            cp.start()

    for cp in gather(flat, slot):
        cp.wait()

    h = _rms_mod(xbuf[slot], g_ref[...], mod_ref[0, 1:2, :], mod_ref[0, 0:1, :])
    u_t = lax.dot_general(wt_ref[...].astype(BF16), h.astype(BF16),
                          (((1,), (1,)), ((), ())), preferred_element_type=F32)
    for tau in range(CHUNK):
        o_ref[0, tau] = u_t[:, tau * n_chunks:(tau + 1) * n_chunks].astype(o_ref.dtype)


def _ssm_in_call(x, mod, g, w_ssm_t, chunks_per_step):
    bsz, seq, _ = x.shape
    ts = chunks_per_step * CHUNK
    x_chunked = x.reshape(bsz, seq // CHUNK, CHUNK, D_MODEL)
    return pl.pallas_call(
        _ssm_in_kernel,
        grid=(bsz, seq // ts),
        in_specs=[
            pl.BlockSpec(memory_space=pl.ANY),
            pl.BlockSpec((1, 6, D_MODEL), lambda b, s: (b, 0, 0)),
            pl.BlockSpec((1, D_MODEL), lambda b, s: (0, 0)),
            pl.BlockSpec((SSM_WIDTH, D_MODEL), lambda b, s: (0, 0)),
        ],
        out_specs=pl.BlockSpec((1, CHUNK, SSM_WIDTH, chunks_per_step),
                               lambda b, s: (b, 0, 0, s)),
        out_shape=jax.ShapeDtypeStruct((bsz, CHUNK, SSM_WIDTH, seq // CHUNK), BF16),
        scratch_shapes=[pltpu.VMEM((2, ts, D_MODEL), F32),
                        pltpu.SemaphoreType.DMA((2,))],
        compiler_params=pltpu.CompilerParams(
            dimension_semantics=("arbitrary", "arbitrary"),
            vmem_limit_bytes=VMEM_LIMIT),
        name="ssm_in",
    )(x_chunked, mod, g, w_ssm_t)


def _ssm_kernel(u_ref, kt_ref, bst_ref, cst_ref, lre_ref, lim_ref, *rest):
    n_w = (len(rest) - 7) // 2
    w_in_refs, o_ref, w_out_refs = rest[:n_w], rest[n_w], rest[n_w + 1:2 * n_w + 1]
    sre, sim, xre, xim, ys, yn = rest[2 * n_w + 1:]
    n_chunks = u_ref.shape[-1]
    pw = PAIR_ST

    for src, dst in zip(w_in_refs, w_out_refs):
        dst[...] = src[...].astype(BF16)

    def load_pair(j):
        r = u_ref[0, :, j * PAIR_CH:(j + 1) * PAIR_CH, :]
        return r.reshape(PAIR_ROWS, n_chunks)

    for j in range(PAIRS_PER_STEP):
        s = lax.dot_general(load_pair(j), bst_ref[j], (((0,), (0,)), ((), ())),
                            preferred_element_type=F32)
        sre[:, j * pw:(j + 1) * pw] = s[:, :pw]
        sim[:, j * pw:(j + 1) * pw] = s[:, pw:]

    lre = lre_ref[0]
    lim = lim_ref[0]

    def body(c, carry):
        xr, xi = carry
        xre[pl.ds(c, 1), :] = xr
        xim[pl.ds(c, 1), :] = xi
        nr = lre * xr - lim * xi + sre[pl.ds(c, 1), :]
        ni = lre * xi + lim * xr + sim[pl.ds(c, 1), :]
        return nr, ni

    zero = jnp.zeros((1, PAIRS_PER_STEP * pw), F32)
    lax.fori_loop(0, n_chunks, body, (zero, zero), unroll=8)

    for j in range(PAIRS_PER_STEP):
        y_intra = jnp.dot(kt_ref[j], load_pair(j), preferred_element_type=F32)
        xp = jnp.concatenate([xre[:, j * pw:(j + 1) * pw],
                              xim[:, j * pw:(j + 1) * pw]], axis=1).astype(BF16)
        y_state = lax.dot_general(cst_ref[j], xp, (((1,), (1,)), ((), ())),
                                  preferred_element_type=F32)
        y = jax.nn.gelu(y_intra + y_state)
        ys[:, j * PAIR_CH:(j + 1) * PAIR_CH, :] = y.reshape(CHUNK, PAIR_CH, n_chunks)

    for t in range(CHUNK):
        yn[pl.ds(t, n_chunks, stride=CHUNK), :] = ys[t].T
    o_ref[0] = yn[...].astype(o_ref.dtype)


def _ssm_call(u_t, kt, bst, cst, lam_re, lam_im, weights):
    bsz, _, _, n_chunks = u_t.shape
    seq = n_chunks * CHUNK
    pps = PAIRS_PER_STEP
    ch = pps * PAIR_CH
    state_w = pps * PAIR_ST
    steps = N_PAIRS // pps
    lam_re = lam_re.reshape(steps, 1, state_w)
    lam_im = lam_im.reshape(steps, 1, state_w)
    n_steps = steps * bsz
    w_specs = [pl.BlockSpec((w.shape[0] // n_steps, w.shape[1]),
                            lambda p, b: (p * bsz + b, 0)) for w in weights]
    outs = pl.pallas_call(
        _ssm_kernel,
        grid=(steps, bsz),
        in_specs=[
            pl.BlockSpec((1, CHUNK, ch, n_chunks), lambda p, b: (b, 0, p, 0)),
            pl.BlockSpec((pps, PAIR_ROWS, PAIR_ROWS), lambda p, b: (p, 0, 0)),
            pl.BlockSpec((pps, PAIR_ROWS, 2 * PAIR_ST), lambda p, b: (p, 0, 0)),
            pl.BlockSpec((pps, PAIR_ROWS, 2 * PAIR_ST), lambda p, b: (p, 0, 0)),
            pl.BlockSpec((1, 1, state_w), lambda p, b: (p, 0, 0)),
            pl.BlockSpec((1, 1, state_w), lambda p, b: (p, 0, 0)),
        ] + w_specs,
        out_specs=[pl.BlockSpec((1, seq, ch), lambda p, b: (b, 0, p))] + w_specs,
        out_shape=[jax.ShapeDtypeStruct((bsz, seq, SSM_WIDTH), BF16)] + [
            jax.ShapeDtypeStruct(w.shape, BF16) for w in weights],
        scratch_shapes=[pltpu.VMEM((n_chunks, state_w), F32)] * 4 + [
            pltpu.VMEM((CHUNK, ch, n_chunks), F32), pltpu.VMEM((seq, ch), F32)],
        compiler_params=pltpu.CompilerParams(
            dimension_semantics=("arbitrary", "arbitrary"),
            vmem_limit_bytes=VMEM_LIMIT),
        name="ssm",
    )(u_t, kt, bst, cst, lam_re, lam_im, *weights)
    return outs[0], outs[1:]


def _mixer_kernel(x_ref, y_ref, mod_ref, g_ref, wc_ref, wgc0_ref, wgc1_ref, wgs0_ref,
                  wgs1_ref, cw_ref, cb_ref, lng_ref, lnb_ref, proj_ref, glu_ref,
                  wout_ref, wf_ref, o_ref, wf_o_ref, ubuf, cbuf, sbuf):
    ts = x_ref.shape[1]
    halo = CONV_HALO
    half = D_MODEL // 2
    n_lt = CONV_WIDTH // LANES

    wf_o_ref[...] = wf_ref[...].astype(BF16)

    @pl.when(pl.program_id(1) == 0)
    def _():
        ubuf[:, 0:halo, :] = jnp.zeros((n_lt, halo, LANES), F32)

    x = x_ref[0]
    h = _rms_mod(x, g_ref[...], mod_ref[0, 1:2, :], mod_ref[0, 0:1, :]).astype(BF16)

    zc = jnp.dot(h, wc_ref[...], preferred_element_type=F32)
    u = zc[:, :CONV_WIDTH] * _sigmoid(zc[:, CONV_WIDTH:])
    for k in range(n_lt):
        ubuf[k, halo:halo + ts, :] = u[:, k * LANES:(k + 1) * LANES]

    y_in = y_ref[0]

    def dot_h(w_ref):
        return jnp.dot(h, w_ref[...], preferred_element_type=F32)

    def dot_y(k):
        return jnp.dot(y_in, glu_ref[:, k * half:(k + 1) * half],
                       preferred_element_type=F32)

    gc0, gs0, za0, zb0, gc1, gs1, za1, zb1 = range(N_SIDE)

    def put(k, v):
        sbuf[k] = v

    dyn_zero = jnp.minimum(pl.program_id(1), 0)
    never = lax.broadcasted_iota(jnp.int32, (SUBLANES, LANES), 0) < dyn_zero

    def anchored(term, k):
        fence = sbuf[dyn_zero, k * SUBLANES:(k + 1) * SUBLANES, 0:LANES]
        head = jnp.where(never, fence, term[0:SUBLANES, :])
        return jnp.concatenate([head, term[SUBLANES:, :]], axis=0)

    side_jobs = [
        lambda: put(gc0, dot_h(wgc0_ref)), lambda: put(gs0, dot_h(wgs0_ref)),
        lambda: put(za0, dot_y(0)), lambda: put(zb0, dot_y(2)),
        lambda: put(gc1, dot_h(wgc1_ref)), lambda: put(gs1, dot_h(wgs1_ref)),
        lambda: put(za1, dot_y(1)), lambda: put(zb1, dot_y(3)),
    ]
    first = halo - (CONV_KERNEL - 1)
    n_rt = ts // CONV_ROWS
    for i in range(n_rt):
        for k in range(n_lt):
            lanes = slice(k * LANES, (k + 1) * LANES)
            acc = None
            for j in range(CONV_KERNEL):
                row0 = first + j + i * CONV_ROWS
                term = cw_ref[j:j + 1, lanes] * ubuf[k, row0:row0 + CONV_ROWS, :]
                if j == 0 and i > 0:
                    term = anchored(term, k)
                acc = term if acc is None else acc + term
            cbuf[i * CONV_ROWS:(i + 1) * CONV_ROWS, lanes] = acc + cb_ref[:, lanes]
        if (i + 1) * len(side_jobs) % n_rt == 0:
            side_jobs[(i + 1) * len(side_jobs) // n_rt - 1]()
    for k in range(n_lt):
        ubuf[k, 0:halo, :] = ubuf[k, ts:ts + halo, :]
    acc = cbuf[...]
    mu = jnp.mean(acc, axis=-1, keepdims=True)
    cen = acc - mu
    var = jnp.mean(cen * cen, axis=-1, keepdims=True)
    yn = cen * lax.rsqrt(var + EPS) * lng_ref[...] + lnb_ref[...]
    yn = yn * _sigmoid(yn)
    y_conv = jnp.dot(yn.astype(BF16), proj_ref[...], preferred_element_type=F32)

    def merge(conv_half, g_conv, g_ssm, z_a, z_b):
        y_ssm = sbuf[z_a] * _sigmoid(sbuf[z_b])
        return _sigmoid(sbuf[g_conv]) * conv_half + _sigmoid(sbuf[g_ssm]) * y_ssm

    merged = jnp.concatenate(
        [merge(y_conv[:, :half], gc0, gs0, za0, zb0),
         merge(y_conv[:, half:], gc1, gs1, za1, zb1)], axis=1)
    upd = jnp.dot(merged.astype(BF16), wout_ref[...], preferred_element_type=F32)
    o_ref[0] = x + mod_ref[0, 2:3, :] * upd


def _const_spec(shape, index=None):
    index = (0,) * len(shape) if index is None else index
    return pl.BlockSpec(shape, lambda b, s: index, pipeline_mode=pl.Buffered(1))


def _mixer_call(x, y, mod, g, w_in_b, cw, cb, lng, lnb, proj, glu, wout, w_ffn_in, ts):
    bsz, seq, _ = x.shape
    n_s = seq // ts
    gate0 = (2 * CONV_WIDTH + SSM_WIDTH) // GATE_BLOCK
    gate_spec = lambda i: _const_spec((D_MODEL, GATE_BLOCK), (0, gate0 + i))
    wf_rows, wf_cols = w_ffn_in.shape
    wf_spec = pl.BlockSpec((wf_rows // (bsz * n_s), wf_cols), lambda b, s: (b * n_s + s, 0))
    return pl.pallas_call(
        _mixer_kernel,
        grid=(bsz, seq // ts),
        in_specs=[
            pl.BlockSpec((1, ts, D_MODEL), lambda b, s: (b, s, 0)),
            pl.BlockSpec((1, ts, SSM_WIDTH), lambda b, s: (b, s, 0)),
            pl.BlockSpec((1, 6, D_MODEL), lambda b, s: (b, 0, 0)),
            _const_spec((1, D_MODEL)),
            _const_spec((D_MODEL, 2 * CONV_WIDTH)),
            gate_spec(0), gate_spec(1), gate_spec(2), gate_spec(3),
            _const_spec((CONV_KERNEL, CONV_WIDTH)),
            _const_spec((1, CONV_WIDTH)),
            _const_spec((1, CONV_WIDTH)),
            _const_spec((1, CONV_WIDTH)),
            _const_spec((CONV_WIDTH, D_MODEL)),
            _const_spec((SSM_WIDTH, 2 * D_MODEL)),
            _const_spec((D_MODEL, D_MODEL)),
            wf_spec,
        ],
        out_specs=[pl.BlockSpec((1, ts, D_MODEL), lambda b, s: (b, s, 0)), wf_spec],
        out_shape=[jax.ShapeDtypeStruct(x.shape, F32),
                   jax.ShapeDtypeStruct(w_ffn_in.shape, BF16)],
        scratch_shapes=[pltpu.VMEM((CONV_WIDTH // LANES, ts + CONV_HALO, LANES), F32),
                        pltpu.VMEM((ts, CONV_WIDTH), F32),
                        pltpu.VMEM((N_SIDE, ts, D_MODEL // 2), F32)],
        compiler_params=pltpu.CompilerParams(
            dimension_semantics=("arbitrary", "arbitrary"),
            vmem_limit_bytes=VMEM_LIMIT),
        name="mixer",
    )(x, y, mod, g, w_in_b, w_in_b, w_in_b, w_in_b, w_in_b, cw, cb, lng, lnb, proj, glu,
      wout, w_ffn_in)


def _ffn_kernel(x_ref, mod_ref, g_ref, wg_ref, wu_ref, wo_ref, fg_ref, o_ref, act_s):
    x = x_ref[0]
    h = _rms_mod(x, g_ref[...], mod_ref[0, 4:5, :], mod_ref[0, 3:4, :]).astype(BF16)
    for c0 in range(0, FFN_HIDDEN, FFN_CHUNK):
        cols = slice(c0, c0 + FFN_CHUNK)
        fg = jnp.dot(h, wg_ref[:, cols], preferred_element_type=F32)
        fu = jnp.dot(h, wu_ref[:, cols], preferred_element_type=F32)
        act_s[:, cols] = (fg * _sigmoid(fg) * fu).astype(BF16)
    upd = jnp.dot(act_s[...], wo_ref[...], preferred_element_type=F32)
    x2 = x + mod_ref[0, 5:6, :] * upd
    ms = jnp.mean(x2 * x2, axis=-1, keepdims=True)
    o_ref[0] = x2 * lax.rsqrt(ms + EPS) * fg_ref[...]


def _ffn_call(x, mod, g, w_ffn_in_b, wo, final_g, ts):
    bsz, seq, _ = x.shape
    return pl.pallas_call(
        _ffn_kernel,
        grid=(bsz, seq // ts),
        in_specs=[
            pl.BlockSpec((1, ts, D_MODEL), lambda b, s: (b, s, 0)),
            pl.BlockSpec((1, 6, D_MODEL), lambda b, s: (b, 0, 0)),
            _const_spec((1, D_MODEL)),
            _const_spec((D_MODEL, FFN_HIDDEN), (0, 0)),
            _const_spec((D_MODEL, FFN_HIDDEN), (0, 1)),
            _const_spec((FFN_HIDDEN, D_MODEL)),
            _const_spec((1, D_MODEL)),
        ],
        out_specs=pl.BlockSpec((1, ts, D_MODEL), lambda b, s: (b, s, 0)),
        out_shape=jax.ShapeDtypeStruct(x.shape, F32),
        scratch_shapes=[pltpu.VMEM((ts, FFN_HIDDEN), BF16)],
        compiler_params=pltpu.CompilerParams(
            dimension_semantics=("arbitrary", "arbitrary"),
            vmem_limit_bytes=VMEM_LIMIT),
        name="ffn",
    )(x, mod, g, w_ffn_in_b, w_ffn_in_b, wo, final_g)


def kernel(x, c, w_ada, b_ada, norm1_g, w_in, conv_w, conv_b, conv_ln_g, conv_ln_b,
           conv_proj, ssm_a_re, ssm_a_im, ssm_b_re, ssm_b_im, ssm_c_re, ssm_c_im,
           ssm_d, ssm_log_dt, ssm_glu, w_out, norm2_g, w_ffn_in, w_ffn_out, final_g):
    bsz, seq, _ = x.shape
    assert w_ada.shape[0] == 1, "single layer only"

    mod, *ops = _prep_call(c, w_ada[0], b_ada[0], ssm_a_re[0], ssm_a_im[0], ssm_b_re[0],
                           ssm_b_im[0], ssm_c_re[0], ssm_c_im[0], ssm_d[0],
                           ssm_log_dt[0])
    mod = mod.reshape(bsz, -1, D_MODEL)

    g1 = norm1_g[0].reshape(1, -1)

    c0 = 2 * CONV_WIDTH
    w_ssm_t = jnp.transpose(w_in[0][:, c0:c0 + SSM_WIDTH])
    u_t = _ssm_in_call(x, mod, g1, w_ssm_t, 128)
    y, (w_in_b, proj_b, glu_b, w_out_b, w_ffn_out_b) = _ssm_call(
        u_t, *ops, (w_in[0], conv_proj[0], ssm_glu[0], w_out[0], w_ffn_out[0]))

    x1, w_ffn_in_b = _mixer_call(
        x, y, mod, g1, w_in_b, conv_w[0], conv_b[0].reshape(1, -1),
        conv_ln_g[0].reshape(1, -1), conv_ln_b[0].reshape(1, -1),
        proj_b, glu_b, w_out_b, w_ffn_in[0], 512)

    return _ffn_call(x1, mod, norm2_g[0].reshape(1, -1), w_ffn_in_b, w_ffn_out_b,
                     final_g.reshape(1, -1), 1024)
```

```python
import jax
import jax.numpy as jnp
from jax import lax
from jax.experimental import pallas as pl
from jax.experimental.pallas import tpu as pltpu

F32 = jnp.float32
BF16 = jnp.bfloat16

D_MODEL = 1024
CONV_WIDTH = 512
CONV_KERNEL = 31
SSM_WIDTH = 512
SSM_GROUP = 16
SSM_GROUPS = 32
SSM_STATE = 64
FFN_HIDDEN = 2816
EPS = 1e-6

CHUNK = 16
PAIR_CH = 2 * SSM_GROUP
PAIR_ST = 2 * SSM_STATE
PAIR_ROWS = CHUNK * PAIR_CH
N_PAIRS = SSM_GROUPS // 2
PAIRS_PER_STEP = 8
CONV_HALO = 32
GATE_BLOCK = 512
LANES = 128
SUBLANES = 8
CONV_ROWS = 64
FFN_CHUNK = 256
N_SIDE = 8

VMEM_LIMIT = 56 * 1024 * 1024


def _rms_mod(x, g, scale, shift):
    ms = jnp.mean(x * x, axis=-1, keepdims=True)
    return (x * lax.rsqrt(ms + EPS)) * (g * (1.0 + scale)) + shift


def _sigmoid(v):
    return jax.nn.sigmoid(v)


def _split_bf16(v):
    hi = v.astype(BF16)
    return hi, (v - hi.astype(F32)).astype(BF16)


def _prep_kernel(c_ref, wada_ref, bada_ref, ar_ref, ai_ref, ldt_ref, cre_ref, cim_ref,
                 bre_ref, bim_ref, dl_ref, mod_ref, kt_ref, bst_ref, cst_ref, lre_ref,
                 lim_ref, cl_s, lb_s):
    c = c_ref[...]
    ca = (c * _sigmoid(c)).astype(BF16)
    mod_ref[...] = jnp.dot(ca, wada_ref[...].astype(BF16),
                           preferred_element_type=F32) + bada_ref[...]

    ar, ai = ar_ref[0], ai_ref[0]
    dt = jnp.exp(ldt_ref[0])
    dar, dai = dt * ar, dt * ai
    mag, cos1, sin1 = jnp.exp(dar), jnp.cos(dai), jnp.sin(dai)
    n_re, n_im = mag * cos1 - 1.0, mag * sin1
    den = ar * ar + ai * ai
    q_re = (n_re * ar + n_im * ai) / den
    q_im = (n_im * ar - n_re * ai) / den

    row = lax.broadcasted_iota(jnp.int32, (PAIR_CH, PAIR_ST), 0)
    lane = lax.broadcasted_iota(jnp.int32, (PAIR_CH, PAIR_ST), 1)
    same_group = (row // SSM_GROUP) == (lane // SSM_STATE)

    def block_diag(ref):
        v = ref[0]
        return jnp.where(same_group, jnp.concatenate([v, v], axis=1), 0.0)

    c_re, c_im = block_diag(cre_ref), block_diag(cim_ref)
    b_re, b_im = block_diag(bre_ref), block_diag(bim_ref)
    bb_re = q_re * b_re - q_im * b_im
    bb_im = q_re * b_im + q_im * b_re

    inv_mag = jnp.exp(-dar)
    one = jnp.ones_like(dar)
    powers = {0: (one, jnp.zeros_like(dar)), 1: (mag * cos1, mag * sin1),
              -1: (inv_mag * cos1, -(inv_mag * sin1))}

    def power(k):
        if k not in powers:
            step = 1 if k > 0 else -1
            (p_re, p_im), (s_re, s_im) = power(k - step), powers[step]
            powers[k] = (p_re * s_re - p_im * s_im, p_re * s_im + p_im * s_re)
        return powers[k]

    def cmul_cat(x_re, x_im, p, conj_out):
        p_re, p_im = p
        o_re = x_re * p_re - x_im * p_im
        o_im = x_re * p_im + x_im * p_re
        return jnp.concatenate([o_re, -o_im if conj_out else o_im], axis=1)

    for t in range(CHUNK):
        rs = slice(t * PAIR_CH, (t + 1) * PAIR_CH)
        cl_s[rs, :] = cmul_cat(c_re, c_im, power(t), True)
        lb_s[rs, :] = cmul_cat(bb_re, bb_im, power(-t), False)
        cst_ref[0, rs, :] = cmul_cat(c_re, c_im, power(t + 1), True).astype(BF16)
        bst_ref[0, rs, :] = cmul_cat(bb_re, bb_im, power(CHUNK - 1 - t),
                                     False).astype(BF16)
    lre_ref[0], lim_ref[0] = power(CHUNK)

    cl_hi, cl_lo = _split_bf16(cl_s[...])
    lb_hi, lb_lo = _split_bf16(lb_s[...])

    def dot_nt(a, b):
        return lax.dot_general(a, b, (((1,), (1,)), ((), ())), preferred_element_type=F32)

    k = dot_nt(cl_hi, lb_hi) + (dot_nt(cl_hi, lb_lo) + dot_nt(cl_lo, lb_hi))
    ri = lax.broadcasted_iota(jnp.int32, (PAIR_ROWS, PAIR_ROWS), 0)
    ci = lax.broadcasted_iota(jnp.int32, (PAIR_ROWS, PAIR_ROWS), 1)
    k = jnp.where(ri // PAIR_CH >= ci // PAIR_CH, k, 0.0)
    k = k + jnp.where(ri == ci, dl_ref[0], 0.0)
    kt_ref[0] = k.astype(BF16)


def _prep_call(c, w_ada, b_ada, a_re, a_im, b_re, b_im, c_re, c_im, d, log_dt):
    np_, pc, ps, st = N_PAIRS, PAIR_CH, PAIR_ST, SSM_STATE
    bsz = c.shape[0]
    mod_cols = w_ada.shape[1]
    mod_blk = mod_cols // np_
    row = lambda a: a.reshape(np_, 1, ps)
    ldt = jnp.broadcast_to(log_dt[:, None], (SSM_GROUPS, st))
    d_lane = jnp.tile(d.reshape(np_, 1, pc), (1, 1, CHUNK))
    bt = lambda a: jnp.transpose(a, (0, 2, 1)).reshape(np_, pc, st)
    spec = lambda *shape: pl.BlockSpec((1,) + shape, lambda p: (p, 0, 0))
    col_spec = lambda rows: pl.BlockSpec((rows, mod_blk), lambda p: (0, p))
    return pl.pallas_call(
        _prep_kernel,
        grid=(np_,),
        in_specs=[pl.BlockSpec((bsz, D_MODEL), lambda p: (0, 0)),
                  col_spec(D_MODEL), col_spec(1),
                  spec(1, ps), spec(1, ps), spec(1, ps), spec(pc, st), spec(pc, st),
                  spec(pc, st), spec(pc, st), spec(1, PAIR_ROWS)],
        out_specs=[col_spec(bsz),
                   spec(PAIR_ROWS, PAIR_ROWS), spec(PAIR_ROWS, 2 * ps),
                   spec(PAIR_ROWS, 2 * ps), spec(1, ps), spec(1, ps)],
        out_shape=[jax.ShapeDtypeStruct((bsz, mod_cols), F32),
                   jax.ShapeDtypeStruct((np_, PAIR_ROWS, PAIR_ROWS), BF16),
                   jax.ShapeDtypeStruct((np_, PAIR_ROWS, 2 * ps), BF16),
                   jax.ShapeDtypeStruct((np_, PAIR_ROWS, 2 * ps), BF16),
                   jax.ShapeDtypeStruct((np_, 1, ps), F32),
                   jax.ShapeDtypeStruct((np_, 1, ps), F32)],
        scratch_shapes=[pltpu.VMEM((PAIR_ROWS, 2 * ps), F32)] * 2,
        name="prep",
    )(c, w_ada, b_ada.reshape(1, -1), row(a_re), row(a_im), row(ldt),
      c_re.reshape(np_, pc, st), c_im.reshape(np_, pc, st), bt(b_re), bt(b_im), d_lane)


def _ssm_in_kernel(x_hbm, mod_ref, g_ref, wt_ref, o_ref, xbuf, sems):
    n_chunks = o_ref.shape[-1]
    n_s = pl.num_programs(1)
    flat = pl.program_id(0) * n_s + pl.program_id(1)
    slot = flat % 2

    def gather(block, buf_slot):
        bb, c0 = block // n_s, (block % n_s) * n_chunks
        return [pltpu.make_async_copy(
            x_hbm.at[bb, pl.ds(c0, n_chunks), tau, :],
            xbuf.at[buf_slot, pl.ds(tau * n_chunks, n_chunks), :],
            sems.at[buf_slot]) for tau in range(CHUNK)]

    @pl.when(flat == 0)
    def _():
        for cp in gather(flat, slot):
            cp.start()

    @pl.when(flat + 1 < pl.num_programs(0) * n_s)
    def _():
        for cp in gather(flat + 1, 1 - slot):
            cp.start()

    for cp in gather(flat, slot):
        cp.wait()

    h = _rms_mod(xbuf[slot], g_ref[...], mod_ref[0, 1:2, :], mod_ref[0, 0:1, :])
    u_t = lax.dot_general(wt_ref[...].astype(BF16), h.astype(BF16),
                          (((1,), (1,)), ((), ())), preferred_element_type=F32)
    for tau in range(CHUNK):
        o_ref[0, tau] = u_t[:, tau * n_chunks:(tau + 1) * n_chunks].astype(o_ref.dtype)


def _ssm_in_call(x, mod, g, w_ssm_t, chunks_per_step):
    bsz, seq, _ = x.shape
    ts = chunks_per_step * CHUNK
    x_chunked = x.reshape(bsz, seq // CHUNK, CHUNK, D_MODEL)
    return pl.pallas_call(
        _ssm_in_kernel,
        grid=(bsz, seq // ts),
        in_specs=[
            pl.BlockSpec(memory_space=pl.ANY),
            pl.BlockSpec((1, 6, D_MODEL), lambda b, s: (b, 0, 0)),
            pl.BlockSpec((1, D_MODEL), lambda b, s: (0, 0)),
            pl.BlockSpec((SSM_WIDTH, D_MODEL), lambda b, s: (0, 0)),
        ],
        out_specs=pl.BlockSpec((1, CHUNK, SSM_WIDTH, chunks_per_step),
                               lambda b, s: (b, 0, 0, s)),
        out_shape=jax.ShapeDtypeStruct((bsz, CHUNK, SSM_WIDTH, seq // CHUNK), BF16),
        scratch_shapes=[pltpu.VMEM((2, ts, D_MODEL), F32),
                        pltpu.SemaphoreType.DMA((2,))],
        compiler_params=pltpu.CompilerParams(
            dimension_semantics=("arbitrary", "arbitrary"),
            vmem_limit_bytes=VMEM_LIMIT),
        name="ssm_in",
    )(x_chunked, mod, g, w_ssm_t)


def _ssm_kernel(u_ref, kt_ref, bst_ref, cst_ref, lre_ref, lim_ref, *rest):
    n_w = (len(rest) - 7) // 2
    w_in_refs, o_ref, w_out_refs = rest[:n_w], rest[n_w], rest[n_w + 1:2 * n_w + 1]
    sre, sim, xre, xim, ys, yn = rest[2 * n_w + 1:]
    n_chunks = u_ref.shape[-1]
    pw = PAIR_ST

    for src, dst in zip(w_in_refs, w_out_refs):
        dst[...] = src[...].astype(BF16)

    def load_pair(j):
        r = u_ref[0, :, j * PAIR_CH:(j + 1) * PAIR_CH, :]
        return r.reshape(PAIR_ROWS, n_chunks)

    for j in range(PAIRS_PER_STEP):
        s = lax.dot_general(load_pair(j), bst_ref[j], (((0,), (0,)), ((), ())),
                            preferred_element_type=F32)
        sre[:, j * pw:(j + 1) * pw] = s[:, :pw]
        sim[:, j * pw:(j + 1) * pw] = s[:, pw:]

    lre = lre_ref[0]
    lim = lim_ref[0]

    def body(c, carry):
        xr, xi = carry
        xre[pl.ds(c, 1), :] = xr
        xim[pl.ds(c, 1), :] = xi
        nr = lre * xr - lim * xi + sre[pl.ds(c, 1), :]
        ni = lre * xi + lim * xr + sim[pl.ds(c, 1), :]
        return nr, ni

    zero = jnp.zeros((1, PAIRS_PER_STEP * pw), F32)
    lax.fori_loop(0, n_chunks, body, (zero, zero), unroll=8)

    for j in range(PAIRS_PER_STEP):
        y_intra = jnp.dot(kt_ref[j], load_pair(j), preferred_element_type=F32)
        xp = jnp.concatenate([xre[:, j * pw:(j + 1) * pw],
                              xim[:, j * pw:(j + 1) * pw]], axis=1).astype(BF16)
        y_state = lax.dot_general(cst_ref[j], xp, (((1,), (1,)), ((), ())),
                                  preferred_element_type=F32)
        y = jax.nn.gelu(y_intra + y_state)
        ys[:, j * PAIR_CH:(j + 1) * PAIR_CH, :] = y.reshape(CHUNK, PAIR_CH, n_chunks)

    for k in range(yn.shape[0]):
        lanes = slice(k * LANES, (k + 1) * LANES)
        for t in range(CHUNK):
            yn[k, pl.ds(t, n_chunks, stride=CHUNK), :] = ys[t, lanes, :].T
        o_ref[0, :, lanes] = yn[k].astype(o_ref.dtype)


def _ssm_call(u_t, kt, bst, cst, lam_re, lam_im, weights):
    bsz, _, _, n_chunks = u_t.shape
    seq = n_chunks * CHUNK
    pps = PAIRS_PER_STEP
    ch = pps * PAIR_CH
    state_w = pps * PAIR_ST
    steps = N_PAIRS // pps
    lam_re = lam_re.reshape(steps, 1, state_w)
    lam_im = lam_im.reshape(steps, 1, state_w)
    n_steps = steps * bsz
    w_specs = [pl.BlockSpec((w.shape[0] // n_steps, w.shape[1]),
                            lambda p, b: (p * bsz + b, 0)) for w in weights]
    outs = pl.pallas_call(
        _ssm_kernel,
        grid=(steps, bsz),
        in_specs=[
            pl.BlockSpec((1, CHUNK, ch, n_chunks), lambda p, b: (b, 0, p, 0)),
            pl.BlockSpec((pps, PAIR_ROWS, PAIR_ROWS), lambda p, b: (p, 0, 0)),
            pl.BlockSpec((pps, PAIR_ROWS, 2 * PAIR_ST), lambda p, b: (p, 0, 0)),
            pl.BlockSpec((pps, PAIR_ROWS, 2 * PAIR_ST), lambda p, b: (p, 0, 0)),
            pl.BlockSpec((1, 1, state_w), lambda p, b: (p, 0, 0)),
            pl.BlockSpec((1, 1, state_w), lambda p, b: (p, 0, 0)),
        ] + w_specs,
        out_specs=[pl.BlockSpec((1, seq, ch), lambda p, b: (b, 0, p))] + w_specs,
        out_shape=[jax.ShapeDtypeStruct((bsz, seq, SSM_WIDTH), BF16)] + [
            jax.ShapeDtypeStruct(w.shape, BF16) for w in weights],
        scratch_shapes=[pltpu.VMEM((n_chunks, state_w), F32)] * 4 + [
            pltpu.VMEM((CHUNK, ch, n_chunks), F32),
            pltpu.VMEM((ch // LANES, seq, LANES), F32)],
        compiler_params=pltpu.CompilerParams(
            dimension_semantics=("arbitrary", "arbitrary"),
            vmem_limit_bytes=VMEM_LIMIT),
        name="ssm",
    )(u_t, kt, bst, cst, lam_re, lam_im, *weights)
    return outs[0], outs[1:]


def _mixer_kernel(x_ref, y_ref, mod_ref, g_ref, wc_ref, wgc0_ref, wgc1_ref, wgs0_ref,
                  wgs1_ref, cw_ref, cb_ref, lng_ref, lnb_ref, proj_ref, glu_ref,
                  wout_ref, wf_ref, o_ref, wf_o_ref, ubuf, cbuf, sbuf):
    ts = x_ref.shape[1]
    halo = CONV_HALO
    half = D_MODEL // 2
    n_lt = CONV_WIDTH // LANES

    wf_o_ref[...] = wf_ref[...].astype(BF16)

    @pl.when(pl.program_id(1) == 0)
    def _():
        ubuf[:, 0:halo, :] = jnp.zeros((n_lt, halo, LANES), F32)

    x = x_ref[0]
    h = _rms_mod(x, g_ref[...], mod_ref[0, 1:2, :], mod_ref[0, 0:1, :]).astype(BF16)

    zc = jnp.dot(h, wc_ref[...], preferred_element_type=F32)
    u = zc[:, :CONV_WIDTH] * _sigmoid(zc[:, CONV_WIDTH:])
    for k in range(n_lt):
        ubuf[k, halo:halo + ts, :] = u[:, k * LANES:(k + 1) * LANES]

    y_in = y_ref[0]

    def dot_h(w_ref):
        return jnp.dot(h, w_ref[...], preferred_element_type=F32)

    def dot_y(k):
        return jnp.dot(y_in, glu_ref[:, k * half:(k + 1) * half],
                       preferred_element_type=F32)

    gc0, gs0, za0, zb0, gc1, gs1, za1, zb1 = range(N_SIDE)

    def put(k, v):
        sbuf[k] = v

    dyn_zero = jnp.minimum(pl.program_id(1), 0)
    never = lax.broadcasted_iota(jnp.int32, (SUBLANES, LANES), 0) < dyn_zero

    def anchored(term, k):
        fence = sbuf[dyn_zero, k * SUBLANES:(k + 1) * SUBLANES, 0:LANES]
        head = jnp.where(never, fence, term[0:SUBLANES, :])
        return jnp.concatenate([head, term[SUBLANES:, :]], axis=0)

    side_jobs = [
        lambda: put(gc0, dot_h(wgc0_ref)), lambda: put(gs0, dot_h(wgs0_ref)),
        lambda: put(za0, dot_y(0)), lambda: put(zb0, dot_y(2)),
        lambda: put(gc1, dot_h(wgc1_ref)), lambda: put(gs1, dot_h(wgs1_ref)),
        lambda: put(za1, dot_y(1)), lambda: put(zb1, dot_y(3)),
    ]
    first = halo - (CONV_KERNEL - 1)
    n_rt = ts // CONV_ROWS
    for i in range(n_rt):
        for k in range(n_lt):
            lanes = slice(k * LANES, (k + 1) * LANES)
            acc = None
            for j in range(CONV_KERNEL):
                row0 = first + j + i * CONV_ROWS
                term = cw_ref[j:j + 1, lanes] * ubuf[k, row0:row0 + CONV_ROWS, :]
                if j == 0 and i > 0:
                    term = anchored(term, k)
                acc = term if acc is None else acc + term
            cbuf[i * CONV_ROWS:(i + 1) * CONV_ROWS, lanes] = acc + cb_ref[:, lanes]
        if (i + 1) * len(side_jobs) % n_rt == 0:
            side_jobs[(i + 1) * len(side_jobs) // n_rt - 1]()
    for k in range(n_lt):
        ubuf[k, 0:halo, :] = ubuf[k, ts:ts + halo, :]
    acc = cbuf[...]
    mu = jnp.mean(acc, axis=-1, keepdims=True)
    cen = acc - mu
    var = jnp.mean(cen * cen, axis=-1, keepdims=True)
    yn = cen * lax.rsqrt(var + EPS) * lng_ref[...] + lnb_ref[...]
    yn = yn * _sigmoid(yn)
    y_conv = jnp.dot(yn.astype(BF16), proj_ref[...], preferred_element_type=F32)

    def merge(conv_half, g_conv, g_ssm, z_a, z_b):
        y_ssm = sbuf[z_a] * _sigmoid(sbuf[z_b])
        return _sigmoid(sbuf[g_conv]) * conv_half + _sigmoid(sbuf[g_ssm]) * y_ssm

    merged = jnp.concatenate(
        [merge(y_conv[:, :half], gc0, gs0, za0, zb0),
         merge(y_conv[:, half:], gc1, gs1, za1, zb1)], axis=1)
    upd = jnp.dot(merged.astype(BF16), wout_ref[...], preferred_element_type=F32)
    o_ref[0] = x + mod_ref[0, 2:3, :] * upd


def _const_spec(shape, index=None):
    index = (0,) * len(shape) if index is None else index
    return pl.BlockSpec(shape, lambda b, s: index, pipeline_mode=pl.Buffered(1))


def _mixer_call(x, y, mod, g, w_in_b, cw, cb, lng, lnb, proj, glu, wout, w_ffn_in, ts):
    bsz, seq, _ = x.shape
    n_s = seq // ts
    gate0 = (2 * CONV_WIDTH + SSM_WIDTH) // GATE_BLOCK
    gate_spec = lambda i: _const_spec((D_MODEL, GATE_BLOCK), (0, gate0 + i))
    wf_rows, wf_cols = w_ffn_in.shape
    wf_spec = pl.BlockSpec((wf_rows // (bsz * n_s), wf_cols), lambda b, s: (b * n_s + s, 0))
    return pl.pallas_call(
        _mixer_kernel,
        grid=(bsz, seq // ts),
        in_specs=[
            pl.BlockSpec((1, ts, D_MODEL), lambda b, s: (b, s, 0)),
            pl.BlockSpec((1, ts, SSM_WIDTH), lambda b, s: (b, s, 0)),
            pl.BlockSpec((1, 6, D_MODEL), lambda b, s: (b, 0, 0)),
            _const_spec((1, D_MODEL)),
            _const_spec((D_MODEL, 2 * CONV_WIDTH)),
            gate_spec(0), gate_spec(1), gate_spec(2), gate_spec(3),
            _const_spec((CONV_KERNEL, CONV_WIDTH)),
            _const_spec((1, CONV_WIDTH)),
            _const_spec((1, CONV_WIDTH)),
            _const_spec((1, CONV_WIDTH)),
            _const_spec((CONV_WIDTH, D_MODEL)),
            _const_spec((SSM_WIDTH, 2 * D_MODEL)),
            _const_spec((D_MODEL, D_MODEL)),
            wf_spec,
        ],
        out_specs=[pl.BlockSpec((1, ts, D_MODEL), lambda b, s: (b, s, 0)), wf_spec],
        out_shape=[jax.ShapeDtypeStruct(x.shape, F32),
                   jax.ShapeDtypeStruct(w_ffn_in.shape, BF16)],
        scratch_shapes=[pltpu.VMEM((CONV_WIDTH // LANES, ts + CONV_HALO, LANES), F32),
                        pltpu.VMEM((ts, CONV_WIDTH), F32),
                        pltpu.VMEM((N_SIDE, ts, D_MODEL // 2), F32)],
        compiler_params=pltpu.CompilerParams(
            dimension_semantics=("arbitrary", "arbitrary"),
            vmem_limit_bytes=VMEM_LIMIT),
        name="mixer",
    )(x, y, mod, g, w_in_b, w_in_b, w_in_b, w_in_b, w_in_b, cw, cb, lng, lnb, proj, glu,
      wout, w_ffn_in)


def _ffn_kernel(x_ref, mod_ref, g_ref, wg_ref, wu_ref, wo_ref, fg_ref, o_ref, act_s):
    x = x_ref[0]
    h = _rms_mod(x, g_ref[...], mod_ref[0, 4:5, :], mod_ref[0, 3:4, :]).astype(BF16)
    for c0 in range(0, FFN_HIDDEN, FFN_CHUNK):
        cols = slice(c0, c0 + FFN_CHUNK)
        fg = jnp.dot(h, wg_ref[:, cols], preferred_element_type=F32)
        fu = jnp.dot(h, wu_ref[:, cols], preferred_element_type=F32)
        act_s[:, cols] = (fg * _sigmoid(fg) * fu).astype(BF16)
    upd = jnp.dot(act_s[...], wo_ref[...], preferred_element_type=F32)
    x2 = x + mod_ref[0, 5:6, :] * upd
    ms = jnp.mean(x2 * x2, axis=-1, keepdims=True)
    o_ref[0] = x2 * lax.rsqrt(ms + EPS) * fg_ref[...]


def _ffn_call(x, mod, g, w_ffn_in_b, wo, final_g, ts):
    bsz, seq, _ = x.shape
    return pl.pallas_call(
        _ffn_kernel,
        grid=(bsz, seq // ts),
        in_specs=[
            pl.BlockSpec((1, ts, D_MODEL), lambda b, s: (b, s, 0)),
            pl.BlockSpec((1, 6, D_MODEL), lambda b, s: (b, 0, 0)),
            _const_spec((1, D_MODEL)),
            _const_spec((D_MODEL, FFN_HIDDEN), (0, 0)),
            _const_spec((D_MODEL, FFN_HIDDEN), (0, 1)),
            _const_spec((FFN_HIDDEN, D_MODEL)),
            _const_spec((1, D_MODEL)),
        ],
        out_specs=pl.BlockSpec((1, ts, D_MODEL), lambda b, s: (b, s, 0)),
        out_shape=jax.ShapeDtypeStruct(x.shape, F32),
        scratch_shapes=[pltpu.VMEM((ts, FFN_HIDDEN), BF16)],
        compiler_params=pltpu.CompilerParams(
            dimension_semantics=("arbitrary", "arbitrary"),
            vmem_limit_bytes=VMEM_LIMIT),
        name="ffn",
    )(x, mod, g, w_ffn_in_b, w_ffn_in_b, wo, final_g)


def kernel(x, c, w_ada, b_ada, norm1_g, w_in, conv_w, conv_b, conv_ln_g, conv_ln_b,
           conv_proj, ssm_a_re, ssm_a_im, ssm_b_re, ssm_b_im, ssm_c_re, ssm_c_im,
           ssm_d, ssm_log_dt, ssm_glu, w_out, norm2_g, w_ffn_in, w_ffn_out, final_g):
    bsz, seq, _ = x.shape
    assert w_ada.shape[0] == 1, "single layer only"

    mod, *ops = _prep_call(c, w_ada[0], b_ada[0], ssm_a_re[0], ssm_a_im[0], ssm_b_re[0],
                           ssm_b_im[0], ssm_c_re[0], ssm_c_im[0], ssm_d[0],
                           ssm_log_dt[0])
    mod = mod.reshape(bsz, -1, D_MODEL)

    g1 = norm1_g[0].reshape(1, -1)

    c0 = 2 * CONV_WIDTH
    w_ssm_t = jnp.transpose(w_in[0][:, c0:c0 + SSM_WIDTH])
    u_t = _ssm_in_call(x, mod, g1, w_ssm_t, 128)
    y, (w_in_b, proj_b, glu_b, w_out_b, w_ffn_out_b) = _ssm_call(
        u_t, *ops, (w_in[0], conv_proj[0], ssm_glu[0], w_out[0], w_ffn_out[0]))

    x1, w_ffn_in_b = _mixer_call(
        x, y, mod, g1, w_in_b, conv_w[0], conv_b[0].reshape(1, -1),
        conv_ln_g[0].reshape(1, -1), conv_ln_b[0].reshape(1, -1),
        proj_b, glu_b, w_out_b, w_ffn_in[0], 512)

    return _ffn_call(x1, mod, norm2_g[0].reshape(1, -1), w_ffn_in_b, w_ffn_out_b,
                     final_g.reshape(1, -1), 1024)
```

```python
import jax
import jax.numpy as jnp
from jax import lax
from jax.experimental import pallas as pl
from jax.experimental.pallas import tpu as pltpu

F32 = jnp.float32
BF16 = jnp.bfloat16

D_MODEL = 1024
CONV_WIDTH = 512
CONV_KERNEL = 31
SSM_WIDTH = 512
SSM_GROUP = 16
SSM_GROUPS = 32
SSM_STATE = 64
FFN_HIDDEN = 2816
EPS = 1e-6

CHUNK = 16
PAIR_CH = 2 * SSM_GROUP
PAIR_ST = 2 * SSM_STATE
PAIR_ROWS = CHUNK * PAIR_CH
N_PAIRS = SSM_GROUPS // 2
PAIRS_PER_STEP = 8
CONV_HALO = 32
GATE_BLOCK = 512
LANES = 128
SUBLANES = 8
CONV_ROWS = 64
FFN_CHUNK = 256
N_SIDE = 8
MIX_ROWS = 512

VMEM_LIMIT = 56 * 1024 * 1024


def _rms_mod(x, g, scale, shift):
    ms = jnp.mean(x * x, axis=-1, keepdims=True)
    return (x * lax.rsqrt(ms + EPS)) * (g * (1.0 + scale)) + shift


def _sigmoid(v):
    return jax.nn.sigmoid(v)


def _split_bf16(v):
    hi = v.astype(BF16)
    return hi, (v - hi.astype(F32)).astype(BF16)


def _prep_kernel(c_ref, wada_ref, bada_ref, ar_ref, ai_ref, ldt_ref, cre_ref, cim_ref,
                 bre_ref, bim_ref, dl_ref, mod_ref, kt_ref, bst_ref, cst_ref, lre_ref,
                 lim_ref, cl_s, lb_s):
    c = c_ref[...]
    ca = (c * _sigmoid(c)).astype(BF16)
    mod_ref[...] = jnp.dot(ca, wada_ref[...].astype(BF16),
                           preferred_element_type=F32) + bada_ref[...]

    ar, ai = ar_ref[0], ai_ref[0]
    dt = jnp.exp(ldt_ref[0])
    dar, dai = dt * ar, dt * ai
    mag, cos1, sin1 = jnp.exp(dar), jnp.cos(dai), jnp.sin(dai)
    n_re, n_im = mag * cos1 - 1.0, mag * sin1
    den = ar * ar + ai * ai
    q_re = (n_re * ar + n_im * ai) / den
    q_im = (n_im * ar - n_re * ai) / den

    row = lax.broadcasted_iota(jnp.int32, (PAIR_CH, PAIR_ST), 0)
    lane = lax.broadcasted_iota(jnp.int32, (PAIR_CH, PAIR_ST), 1)
    same_group = (row // SSM_GROUP) == (lane // SSM_STATE)

    def block_diag(ref):
        v = ref[0]
        return jnp.where(same_group, jnp.concatenate([v, v], axis=1), 0.0)

    c_re, c_im = block_diag(cre_ref), block_diag(cim_ref)
    b_re, b_im = block_diag(bre_ref), block_diag(bim_ref)
    bb_re = q_re * b_re - q_im * b_im
    bb_im = q_re * b_im + q_im * b_re

    inv_mag = jnp.exp(-dar)
    one = jnp.ones_like(dar)
    powers = {0: (one, jnp.zeros_like(dar)), 1: (mag * cos1, mag * sin1),
              -1: (inv_mag * cos1, -(inv_mag * sin1))}

    def power(k):
        if k not in powers:
            step = 1 if k > 0 else -1
            (p_re, p_im), (s_re, s_im) = power(k - step), powers[step]
            powers[k] = (p_re * s_re - p_im * s_im, p_re * s_im + p_im * s_re)
        return powers[k]

    def cmul_cat(x_re, x_im, p, conj_out):
        p_re, p_im = p
        o_re = x_re * p_re - x_im * p_im
        o_im = x_re * p_im + x_im * p_re
        return jnp.concatenate([o_re, -o_im if conj_out else o_im], axis=1)

    for t in range(CHUNK):
        rs = slice(t * PAIR_CH, (t + 1) * PAIR_CH)
        cl_s[rs, :] = cmul_cat(c_re, c_im, power(t), True)
        lb_s[rs, :] = cmul_cat(bb_re, bb_im, power(-t), False)
        cst_ref[0, rs, :] = cmul_cat(c_re, c_im, power(t + 1), True).astype(BF16)
        bst_ref[0, rs, :] = cmul_cat(bb_re, bb_im, power(CHUNK - 1 - t),
                                     False).astype(BF16)
    lre_ref[0], lim_ref[0] = power(CHUNK)

    cl_hi, cl_lo = _split_bf16(cl_s[...])
    lb_hi, lb_lo = _split_bf16(lb_s[...])

    def dot_nt(a, b):
        return lax.dot_general(a, b, (((1,), (1,)), ((), ())), preferred_element_type=F32)

    k = dot_nt(cl_hi, lb_hi) + (dot_nt(cl_hi, lb_lo) + dot_nt(cl_lo, lb_hi))
    ri = lax.broadcasted_iota(jnp.int32, (PAIR_ROWS, PAIR_ROWS), 0)
    ci = lax.broadcasted_iota(jnp.int32, (PAIR_ROWS, PAIR_ROWS), 1)
    k = jnp.where(ri // PAIR_CH >= ci // PAIR_CH, k, 0.0)
    k = k + jnp.where(ri == ci, dl_ref[0], 0.0)
    kt_ref[0] = k.astype(BF16)


def _prep_call(c, w_ada, b_ada, a_re, a_im, b_re, b_im, c_re, c_im, d, log_dt):
    np_, pc, ps, st = N_PAIRS, PAIR_CH, PAIR_ST, SSM_STATE
    bsz = c.shape[0]
    mod_cols = w_ada.shape[1]
    mod_blk = mod_cols // np_
    row = lambda a: a.reshape(np_, 1, ps)
    ldt = jnp.broadcast_to(log_dt[:, None], (SSM_GROUPS, st))
    d_lane = jnp.tile(d.reshape(np_, 1, pc), (1, 1, CHUNK))
    bt = lambda a: jnp.transpose(a, (0, 2, 1)).reshape(np_, pc, st)
    spec = lambda *shape: pl.BlockSpec((1,) + shape, lambda p: (p, 0, 0))
    col_spec = lambda rows: pl.BlockSpec((rows, mod_blk), lambda p: (0, p))
    return pl.pallas_call(
        _prep_kernel,
        grid=(np_,),
        in_specs=[pl.BlockSpec((bsz, D_MODEL), lambda p: (0, 0)),
                  col_spec(D_MODEL), col_spec(1),
                  spec(1, ps), spec(1, ps), spec(1, ps), spec(pc, st), spec(pc, st),
                  spec(pc, st), spec(pc, st), spec(1, PAIR_ROWS)],
        out_specs=[col_spec(bsz),
                   spec(PAIR_ROWS, PAIR_ROWS), spec(PAIR_ROWS, 2 * ps),
                   spec(PAIR_ROWS, 2 * ps), spec(1, ps), spec(1, ps)],
        out_shape=[jax.ShapeDtypeStruct((bsz, mod_cols), F32),
                   jax.ShapeDtypeStruct((np_, PAIR_ROWS, PAIR_ROWS), BF16),
                   jax.ShapeDtypeStruct((np_, PAIR_ROWS, 2 * ps), BF16),
                   jax.ShapeDtypeStruct((np_, PAIR_ROWS, 2 * ps), BF16),
                   jax.ShapeDtypeStruct((np_, 1, ps), F32),
                   jax.ShapeDtypeStruct((np_, 1, ps), F32)],
        scratch_shapes=[pltpu.VMEM((PAIR_ROWS, 2 * ps), F32)] * 2,
        name="prep",
    )(c, w_ada, b_ada.reshape(1, -1), row(a_re), row(a_im), row(ldt),
      c_re.reshape(np_, pc, st), c_im.reshape(np_, pc, st), bt(b_re), bt(b_im), d_lane)


def _ssm_in_kernel(x_hbm, mod_ref, g_ref, w_ref, o_ref, xbuf, sems, wt_s):
    n_chunks = o_ref.shape[-1]
    n_s = pl.num_programs(1)
    flat = pl.program_id(0) * n_s + pl.program_id(1)
    slot = flat % 2

    def gather(block, buf_slot):
        bb, c0 = block // n_s, (block % n_s) * n_chunks
        return [pltpu.make_async_copy(
            x_hbm.at[bb, pl.ds(c0, n_chunks), tau, :],
            xbuf.at[buf_slot, pl.ds(tau * n_chunks, n_chunks), :],
            sems.at[buf_slot]) for tau in range(CHUNK)]

    @pl.when(flat == 0)
    def _():
        for cp in gather(flat, slot):
            cp.start()
        wt_s[...] = w_ref[...].T.astype(BF16)

    @pl.when(flat + 1 < pl.num_programs(0) * n_s)
    def _():
        for cp in gather(flat + 1, 1 - slot):
            cp.start()

    for cp in gather(flat, slot):
        cp.wait()

    h = _rms_mod(xbuf[slot], g_ref[...], mod_ref[0, 1:2, :], mod_ref[0, 0:1, :])
    u_t = lax.dot_general(wt_s[...], h.astype(BF16), (((1,), (1,)), ((), ())),
                          preferred_element_type=F32)
    for tau in range(CHUNK):
        o_ref[0, tau] = u_t[:, tau * n_chunks:(tau + 1) * n_chunks].astype(o_ref.dtype)


def _ssm_in_call(x, mod, g, w_in, chunks_per_step):
    bsz, seq, _ = x.shape
    ts = chunks_per_step * CHUNK
    ssm_col_block = 2 * CONV_WIDTH // SSM_WIDTH
    x_chunked = x.reshape(bsz, seq // CHUNK, CHUNK, D_MODEL)
    return pl.pallas_call(
        _ssm_in_kernel,
        grid=(bsz, seq // ts),
        in_specs=[
            pl.BlockSpec(memory_space=pl.ANY),
            pl.BlockSpec((1, 6, D_MODEL), lambda b, s: (b, 0, 0)),
            pl.BlockSpec((1, D_MODEL), lambda b, s: (0, 0)),
            pl.BlockSpec((D_MODEL, SSM_WIDTH), lambda b, s: (0, ssm_col_block)),
        ],
        out_specs=pl.BlockSpec((1, CHUNK, SSM_WIDTH, chunks_per_step),
                               lambda b, s: (b, 0, 0, s)),
        out_shape=jax.ShapeDtypeStruct((bsz, CHUNK, SSM_WIDTH, seq // CHUNK), BF16),
        scratch_shapes=[pltpu.VMEM((2, ts, D_MODEL), F32),
                        pltpu.SemaphoreType.DMA((2,)),
                        pltpu.VMEM((SSM_WIDTH, D_MODEL), BF16)],
        compiler_params=pltpu.CompilerParams(
            dimension_semantics=("arbitrary", "arbitrary"),
            vmem_limit_bytes=VMEM_LIMIT),
        name="ssm_in",
    )(x_chunked, mod, g, w_in)


def _ssm_kernel(u_ref, kt_ref, bst_ref, cst_ref, lre_ref, lim_ref, *rest):
    n_w = (len(rest) - 7) // 2
    w_in_refs, o_ref, w_out_refs = rest[:n_w], rest[n_w], rest[n_w + 1:2 * n_w + 1]
    sre, sim, xre, xim, ys, yn = rest[2 * n_w + 1:]
    n_chunks = u_ref.shape[-1]
    pw = PAIR_ST

    for src, dst in zip(w_in_refs, w_out_refs):
        dst[...] = src[...].astype(BF16)

    def load_pair(j):
        r = u_ref[0, :, j * PAIR_CH:(j + 1) * PAIR_CH, :]
        return r.reshape(PAIR_ROWS, n_chunks)

    for j in range(PAIRS_PER_STEP):
        s = lax.dot_general(load_pair(j), bst_ref[j], (((0,), (0,)), ((), ())),
                            preferred_element_type=F32)
        sre[:, j * pw:(j + 1) * pw] = s[:, :pw]
        sim[:, j * pw:(j + 1) * pw] = s[:, pw:]

    lre = lre_ref[0]
    lim = lim_ref[0]

    def body(c, carry):
        xr, xi = carry
        xre[pl.ds(c, 1), :] = xr
        xim[pl.ds(c, 1), :] = xi
        nr = lre * xr - lim * xi + sre[pl.ds(c, 1), :]
        ni = lre * xi + lim * xr + sim[pl.ds(c, 1), :]
        return nr, ni

    zero = jnp.zeros((1, PAIRS_PER_STEP * pw), F32)
    lax.fori_loop(0, n_chunks, body, (zero, zero), unroll=8)

    for j in range(PAIRS_PER_STEP):
        y_intra = jnp.dot(kt_ref[j], load_pair(j), preferred_element_type=F32)
        xp = jnp.concatenate([xre[:, j * pw:(j + 1) * pw],
                              xim[:, j * pw:(j + 1) * pw]], axis=1).astype(BF16)
        y_state = lax.dot_general(cst_ref[j], xp, (((1,), (1,)), ((), ())),
                                  preferred_element_type=F32)
        y = jax.nn.gelu(y_intra + y_state)
        ys[:, j * PAIR_CH:(j + 1) * PAIR_CH, :] = y.reshape(CHUNK, PAIR_CH, n_chunks)

    for k in range(yn.shape[0]):
        lanes = slice(k * LANES, (k + 1) * LANES)
        for t in range(CHUNK):
            yn[k, pl.ds(t, n_chunks, stride=CHUNK), :] = ys[t, lanes, :].T
        o_ref[0, :, lanes] = yn[k].astype(o_ref.dtype)


def _ssm_call(u_t, kt, bst, cst, lam_re, lam_im, weights):
    bsz, _, _, n_chunks = u_t.shape
    seq = n_chunks * CHUNK
    pps = PAIRS_PER_STEP
    ch = pps * PAIR_CH
    state_w = pps * PAIR_ST
    steps = N_PAIRS // pps
    lam_re = lam_re.reshape(steps, 1, state_w)
    lam_im = lam_im.reshape(steps, 1, state_w)
    n_steps = steps * bsz
    w_specs = [pl.BlockSpec((w.shape[0] // n_steps, w.shape[1]),
                            lambda p, b: (p * bsz + b, 0)) for w in weights]
    outs = pl.pallas_call(
        _ssm_kernel,
        grid=(steps, bsz),
        in_specs=[
            pl.BlockSpec((1, CHUNK, ch, n_chunks), lambda p, b: (b, 0, p, 0)),
            pl.BlockSpec((pps, PAIR_ROWS, PAIR_ROWS), lambda p, b: (p, 0, 0)),
            pl.BlockSpec((pps, PAIR_ROWS, 2 * PAIR_ST), lambda p, b: (p, 0, 0)),
            pl.BlockSpec((pps, PAIR_ROWS, 2 * PAIR_ST), lambda p, b: (p, 0, 0)),
            pl.BlockSpec((1, 1, state_w), lambda p, b: (p, 0, 0)),
            pl.BlockSpec((1, 1, state_w), lambda p, b: (p, 0, 0)),
        ] + w_specs,
        out_specs=[pl.BlockSpec((1, seq, ch), lambda p, b: (b, 0, p))] + w_specs,
        out_shape=[jax.ShapeDtypeStruct((bsz, seq, SSM_WIDTH), BF16)] + [
            jax.ShapeDtypeStruct(w.shape, BF16) for w in weights],
        scratch_shapes=[pltpu.VMEM((n_chunks, state_w), F32)] * 4 + [
            pltpu.VMEM((CHUNK, ch, n_chunks), F32),
            pltpu.VMEM((ch // LANES, seq, LANES), F32)],
        compiler_params=pltpu.CompilerParams(
            dimension_semantics=("arbitrary", "arbitrary"),
            vmem_limit_bytes=VMEM_LIMIT),
        name="ssm",
    )(u_t, kt, bst, cst, lam_re, lam_im, *weights)
    return outs[0], outs[1:]


def _mixer_kernel(x_ref, y_ref, mod_ref, g_ref, wc_ref, wgc0_ref, wgc1_ref, wgs0_ref,
                  wgs1_ref, cw_ref, cb_ref, lng_ref, lnb_ref, proj_ref, glu_ref,
                  wout_ref, wf_ref, o_ref, wf_o_ref, ubuf, cbuf, sbuf):
    wf_o_ref[...] = wf_ref[...].astype(BF16)

    @pl.when(pl.program_id(1) == 0)
    def _():
        ubuf[:, 0:CONV_HALO, :] = jnp.zeros((CONV_WIDTH // LANES, CONV_HALO, LANES), F32)

    for r0 in range(0, x_ref.shape[1], MIX_ROWS):
        _mixer_rows(r0, x_ref, y_ref, mod_ref, g_ref, wc_ref, wgc0_ref, wgc1_ref,
                    wgs0_ref, wgs1_ref, cw_ref, cb_ref, lng_ref, lnb_ref, proj_ref,
                    glu_ref, wout_ref, o_ref, ubuf, cbuf, sbuf)


def _mixer_rows(r0, x_ref, y_ref, mod_ref, g_ref, wc_ref, wgc0_ref, wgc1_ref, wgs0_ref,
                wgs1_ref, cw_ref, cb_ref, lng_ref, lnb_ref, proj_ref, glu_ref,
                wout_ref, o_ref, ubuf, cbuf, sbuf):
    ts = MIX_ROWS
    rows = slice(r0, r0 + ts)
    halo = CONV_HALO
    half = D_MODEL // 2
    n_lt = CONV_WIDTH // LANES

    x = x_ref[0, rows, :]
    h = _rms_mod(x, g_ref[...], mod_ref[0, 1:2, :], mod_ref[0, 0:1, :]).astype(BF16)

    zc = jnp.dot(h, wc_ref[...], preferred_element_type=F32)
    u = zc[:, :CONV_WIDTH] * _sigmoid(zc[:, CONV_WIDTH:])
    for k in range(n_lt):
        ubuf[k, halo:halo + ts, :] = u[:, k * LANES:(k + 1) * LANES]

    y_in = y_ref[0, rows, :]

    def dot_h(w_ref):
        return jnp.dot(h, w_ref[...], preferred_element_type=F32)

    def dot_y(k):
        return jnp.dot(y_in, glu_ref[:, k * half:(k + 1) * half],
                       preferred_element_type=F32)

    gc0, gs0, za0, zb0, gc1, gs1, za1, zb1 = range(N_SIDE)

    def put(k, v):
        sbuf[k] = v

    dyn_zero = jnp.minimum(pl.program_id(1), 0)
    never = lax.broadcasted_iota(jnp.int32, (SUBLANES, LANES), 0) < dyn_zero

    def anchored(term, k):
        fence = sbuf[dyn_zero, k * SUBLANES:(k + 1) * SUBLANES, 0:LANES]
        head = jnp.where(never, fence, term[0:SUBLANES, :])
        return jnp.concatenate([head, term[SUBLANES:, :]], axis=0)

    side_jobs = [
        lambda: put(gc0, dot_h(wgc0_ref)), lambda: put(gs0, dot_h(wgs0_ref)),
        lambda: put(za0, dot_y(0)), lambda: put(zb0, dot_y(2)),
        lambda: put(gc1, dot_h(wgc1_ref)), lambda: put(gs1, dot_h(wgs1_ref)),
        lambda: put(za1, dot_y(1)), lambda: put(zb1, dot_y(3)),
    ]
    first = halo - (CONV_KERNEL - 1)
    n_rt = ts // CONV_ROWS
    for i in range(n_rt):
        for k in range(n_lt):
            lanes = slice(k * LANES, (k + 1) * LANES)
            acc = None
            for j in range(CONV_KERNEL):
                row0 = first + j + i * CONV_ROWS
                term = cw_ref[j:j + 1, lanes] * ubuf[k, row0:row0 + CONV_ROWS, :]
                if j == 0 and i > 0:
                    term = anchored(term, k)
                acc = term if acc is None else acc + term
            cbuf[i * CONV_ROWS:(i + 1) * CONV_ROWS, lanes] = acc + cb_ref[:, lanes]
        if (i + 1) * len(side_jobs) % n_rt == 0:
            side_jobs[(i + 1) * len(side_jobs) // n_rt - 1]()
    for k in range(n_lt):
        ubuf[k, 0:halo, :] = ubuf[k, ts:ts + halo, :]
    acc = cbuf[...]
    mu = jnp.mean(acc, axis=-1, keepdims=True)
    cen = acc - mu
    var = jnp.mean(cen * cen, axis=-1, keepdims=True)
    yn = cen * lax.rsqrt(var + EPS) * lng_ref[...] + lnb_ref[...]
    yn = yn * _sigmoid(yn)
    y_conv = jnp.dot(yn.astype(BF16), proj_ref[...], preferred_element_type=F32)

    def merge(conv_half, g_conv, g_ssm, z_a, z_b):
        y_ssm = sbuf[z_a] * _sigmoid(sbuf[z_b])
        return _sigmoid(sbuf[g_conv]) * conv_half + _sigmoid(sbuf[g_ssm]) * y_ssm

    merged = jnp.concatenate(
        [merge(y_conv[:, :half], gc0, gs0, za0, zb0),
         merge(y_conv[:, half:], gc1, gs1, za1, zb1)], axis=1)
    upd = jnp.dot(merged.astype(BF16), wout_ref[...], preferred_element_type=F32)
    o_ref[0, rows, :] = x + mod_ref[0, 2:3, :] * upd


def _const_spec(shape, index=None):
    index = (0,) * len(shape) if index is None else index
    return pl.BlockSpec(shape, lambda b, s: index, pipeline_mode=pl.Buffered(1))


def _mixer_call(x, y, mod, g, w_in_b, cw, cb, lng, lnb, proj, glu, wout, w_ffn_in, ts):
    bsz, seq, _ = x.shape
    n_s = seq // ts
    gate0 = (2 * CONV_WIDTH + SSM_WIDTH) // GATE_BLOCK
    gate_spec = lambda i: _const_spec((D_MODEL, GATE_BLOCK), (0, gate0 + i))
    wf_rows, wf_cols = w_ffn_in.shape
    wf_spec = pl.BlockSpec((wf_rows // (bsz * n_s), wf_cols), lambda b, s: (b * n_s + s, 0))
    return pl.pallas_call(
        _mixer_kernel,
        grid=(bsz, seq // ts),
        in_specs=[
            pl.BlockSpec((1, ts, D_MODEL), lambda b, s: (b, s, 0)),
            pl.BlockSpec((1, ts, SSM_WIDTH), lambda b, s: (b, s, 0)),
            pl.BlockSpec((1, 6, D_MODEL), lambda b, s: (b, 0, 0)),
            _const_spec((1, D_MODEL)),
            _const_spec((D_MODEL, 2 * CONV_WIDTH)),
            gate_spec(0), gate_spec(1), gate_spec(2), gate_spec(3),
            _const_spec((CONV_KERNEL, CONV_WIDTH)),
            _const_spec((1, CONV_WIDTH)),
            _const_spec((1, CONV_WIDTH)),
            _const_spec((1, CONV_WIDTH)),
            _const_spec((CONV_WIDTH, D_MODEL)),
            _const_spec((SSM_WIDTH, 2 * D_MODEL)),
            _const_spec((D_MODEL, D_MODEL)),
            wf_spec,
        ],
        out_specs=[pl.BlockSpec((1, ts, D_MODEL), lambda b, s: (b, s, 0)), wf_spec],
        out_shape=[jax.ShapeDtypeStruct(x.shape, F32),
                   jax.ShapeDtypeStruct(w_ffn_in.shape, BF16)],
        scratch_shapes=[
            pltpu.VMEM((CONV_WIDTH // LANES, MIX_ROWS + CONV_HALO, LANES), F32),
            pltpu.VMEM((MIX_ROWS, CONV_WIDTH), F32),
            pltpu.VMEM((N_SIDE, MIX_ROWS, D_MODEL // 2), F32)],
        compiler_params=pltpu.CompilerParams(
            dimension_semantics=("arbitrary", "arbitrary"),
            vmem_limit_bytes=VMEM_LIMIT),
        name="mixer",
    )(x, y, mod, g, w_in_b, w_in_b, w_in_b, w_in_b, w_in_b, cw, cb, lng, lnb, proj, glu,
      wout, w_ffn_in)


def _ffn_kernel(x_ref, mod_ref, g_ref, wg_ref, wu_ref, wo_ref, fg_ref, o_ref, act_s):
    x = x_ref[0]
    h = _rms_mod(x, g_ref[...], mod_ref[0, 4:5, :], mod_ref[0, 3:4, :]).astype(BF16)
    for c0 in range(0, FFN_HIDDEN, FFN_CHUNK):
        cols = slice(c0, c0 + FFN_CHUNK)
        fg = jnp.dot(h, wg_ref[:, cols], preferred_element_type=F32)
        fu = jnp.dot(h, wu_ref[:, cols], preferred_element_type=F32)
        act_s[:, cols] = (fg * _sigmoid(fg) * fu).astype(BF16)
    upd = jnp.dot(act_s[...], wo_ref[...], preferred_element_type=F32)
    x2 = x + mod_ref[0, 5:6, :] * upd
    ms = jnp.mean(x2 * x2, axis=-1, keepdims=True)
    o_ref[0] = x2 * lax.rsqrt(ms + EPS) * fg_ref[...]


def _ffn_call(x, mod, g, w_ffn_in_b, wo, final_g, ts):
    bsz, seq, _ = x.shape
    return pl.pallas_call(
        _ffn_kernel,
        grid=(bsz, seq // ts),
        in_specs=[
            pl.BlockSpec((1, ts, D_MODEL), lambda b, s: (b, s, 0)),
            pl.BlockSpec((1, 6, D_MODEL), lambda b, s: (b, 0, 0)),
            _const_spec((1, D_MODEL)),
            _const_spec((D_MODEL, FFN_HIDDEN), (0, 0)),
            _const_spec((D_MODEL, FFN_HIDDEN), (0, 1)),
            _const_spec((FFN_HIDDEN, D_MODEL)),
            _const_spec((1, D_MODEL)),
        ],
        out_specs=pl.BlockSpec((1, ts, D_MODEL), lambda b, s: (b, s, 0)),
        out_shape=jax.ShapeDtypeStruct(x.shape, F32),
        scratch_shapes=[pltpu.VMEM((ts, FFN_HIDDEN), BF16)],
        compiler_params=pltpu.CompilerParams(
            dimension_semantics=("arbitrary", "arbitrary"),
            vmem_limit_bytes=VMEM_LIMIT),
        name="ffn",
    )(x, mod, g, w_ffn_in_b, w_ffn_in_b, wo, final_g)


def kernel(x, c, w_ada, b_ada, norm1_g, w_in, conv_w, conv_b, conv_ln_g, conv_ln_b,
           conv_proj, ssm_a_re, ssm_a_im, ssm_b_re, ssm_b_im, ssm_c_re, ssm_c_im,
           ssm_d, ssm_log_dt, ssm_glu, w_out, norm2_g, w_ffn_in, w_ffn_out, final_g):
    bsz, seq, _ = x.shape
    assert w_ada.shape[0] == 1, "single layer only"

    mod, *ops = _prep_call(c, w_ada[0], b_ada[0], ssm_a_re[0], ssm_a_im[0], ssm_b_re[0],
                           ssm_b_im[0], ssm_c_re[0], ssm_c_im[0], ssm_d[0],
                           ssm_log_dt[0])
    mod = mod.reshape(bsz, -1, D_MODEL)

    g1 = norm1_g[0].reshape(1, -1)

    u_t = _ssm_in_call(x, mod, g1, w_in[0], 128)
    y, (w_in_b, proj_b, glu_b, w_out_b, w_ffn_out_b) = _ssm_call(
        u_t, *ops, (w_in[0], conv_proj[0], ssm_glu[0], w_out[0], w_ffn_out[0]))

    x1, w_ffn_in_b = _mixer_call(
        x, y, mod, g1, w_in_b, conv_w[0], conv_b[0].reshape(1, -1),
        conv_ln_g[0].reshape(1, -1), conv_ln_b[0].reshape(1, -1),
        proj_b, glu_b, w_out_b, w_ffn_in[0], 1024)

    return _ffn_call(x1, mod, norm2_g[0].reshape(1, -1), w_ffn_in_b, w_ffn_out_b,
                     final_g.reshape(1, -1), 1024)
```

```python
import jax
import jax.numpy as jnp
from jax import lax
from jax.experimental import pallas as pl
from jax.experimental.pallas import tpu as pltpu

F32 = jnp.float32
BF16 = jnp.bfloat16

D_MODEL = 1024
CONV_WIDTH = 512
CONV_KERNEL = 31
SSM_WIDTH = 512
SSM_GROUP = 16
SSM_GROUPS = 32
SSM_STATE = 64
FFN_HIDDEN = 2816
EPS = 1e-6

CHUNK = 16
PAIR_CH = 2 * SSM_GROUP
PAIR_ST = 2 * SSM_STATE
PAIR_ROWS = CHUNK * PAIR_CH
N_PAIRS = SSM_GROUPS // 2
PAIRS_PER_STEP = 8
CONV_HALO = 32
GATE_BLOCK = 512
LANES = 128
SUBLANES = 8
CONV_ROWS = 64
FFN_CHUNK = 256
N_SIDE = 8
MIX_ROWS = 512
FFN_ROWS = 512

VMEM_LIMIT = 56 * 1024 * 1024


def _rms_mod(x, g, scale, shift):
    ms = jnp.mean(x * x, axis=-1, keepdims=True)
    return (x * lax.rsqrt(ms + EPS)) * (g * (1.0 + scale)) + shift


def _sigmoid(v):
    return jax.nn.sigmoid(v)


def _split_bf16(v):
    hi = v.astype(BF16)
    return hi, (v - hi.astype(F32)).astype(BF16)


def _prep_kernel(c_ref, wada_ref, bada_ref, ar_ref, ai_ref, ldt_ref, cre_ref, cim_ref,
                 bre_ref, bim_ref, dl_ref, mod_ref, kt_ref, bst_ref, cst_ref, lre_ref,
                 lim_ref, cl_s, lb_s):
    @pl.when(pl.program_id(0) == 0)
    def _():
        mod_ref[...] = jnp.broadcast_to(bada_ref[...], mod_ref.shape)

    c = c_ref[0]
    ca = (c * _sigmoid(c)).astype(BF16)
    mod_ref[...] += jnp.dot(ca, wada_ref[...].astype(BF16), preferred_element_type=F32)

    ar, ai = ar_ref[0], ai_ref[0]
    dt = jnp.exp(ldt_ref[0])
    dar, dai = dt * ar, dt * ai
    mag, cos1, sin1 = jnp.exp(dar), jnp.cos(dai), jnp.sin(dai)
    n_re, n_im = mag * cos1 - 1.0, mag * sin1
    den = ar * ar + ai * ai
    q_re = (n_re * ar + n_im * ai) / den
    q_im = (n_im * ar - n_re * ai) / den

    row = lax.broadcasted_iota(jnp.int32, (PAIR_CH, PAIR_ST), 0)
    lane = lax.broadcasted_iota(jnp.int32, (PAIR_CH, PAIR_ST), 1)
    same_group = (row // SSM_GROUP) == (lane // SSM_STATE)

    def block_diag(ref):
        v = ref[0]
        return jnp.where(same_group, jnp.concatenate([v, v], axis=1), 0.0)

    c_re, c_im = block_diag(cre_ref), block_diag(cim_ref)
    b_re, b_im = block_diag(bre_ref), block_diag(bim_ref)
    bb_re = q_re * b_re - q_im * b_im
    bb_im = q_re * b_im + q_im * b_re

    inv_mag = jnp.exp(-dar)
    one = jnp.ones_like(dar)
    powers = {0: (one, jnp.zeros_like(dar)), 1: (mag * cos1, mag * sin1),
              -1: (inv_mag * cos1, -(inv_mag * sin1))}

    def power(k):
        if k not in powers:
            step = 1 if k > 0 else -1
            (p_re, p_im), (s_re, s_im) = power(k - step), powers[step]
            powers[k] = (p_re * s_re - p_im * s_im, p_re * s_im + p_im * s_re)
        return powers[k]

    def cmul_cat(x_re, x_im, p, conj_out):
        p_re, p_im = p
        o_re = x_re * p_re - x_im * p_im
        o_im = x_re * p_im + x_im * p_re
        return jnp.concatenate([o_re, -o_im if conj_out else o_im], axis=1)

    for t in range(CHUNK):
        rs = slice(t * PAIR_CH, (t + 1) * PAIR_CH)
        cl_s[rs, :] = cmul_cat(c_re, c_im, power(t), True)
        lb_s[rs, :] = cmul_cat(bb_re, bb_im, power(-t), False)
        cst_ref[0, rs, :] = cmul_cat(c_re, c_im, power(t + 1), True).astype(BF16)
        bst_ref[0, rs, :] = cmul_cat(bb_re, bb_im, power(CHUNK - 1 - t),
                                     False).astype(BF16)
    lre_ref[0], lim_ref[0] = power(CHUNK)

    cl_hi, cl_lo = _split_bf16(cl_s[...])
    lb_hi, lb_lo = _split_bf16(lb_s[...])

    def dot_nt(a, b):
        return lax.dot_general(a, b, (((1,), (1,)), ((), ())), preferred_element_type=F32)

    k = dot_nt(cl_hi, lb_hi) + (dot_nt(cl_hi, lb_lo) + dot_nt(cl_lo, lb_hi))
    ri = lax.broadcasted_iota(jnp.int32, (PAIR_ROWS, PAIR_ROWS), 0)
    ci = lax.broadcasted_iota(jnp.int32, (PAIR_ROWS, PAIR_ROWS), 1)
    k = jnp.where(ri // PAIR_CH >= ci // PAIR_CH, k, 0.0)
    k = k + jnp.where(ri == ci, dl_ref[0], 0.0)
    kt_ref[0] = k.astype(BF16)


def _prep_call(c, w_ada, b_ada, a_re, a_im, b_re, b_im, c_re, c_im, d, log_dt):
    np_, pc, ps, st = N_PAIRS, PAIR_CH, PAIR_ST, SSM_STATE
    bsz = c.shape[0]
    mod_rows, mod_cols = w_ada.shape
    k_blk = mod_rows // np_
    c_blocks = jnp.transpose(c.reshape(bsz, np_, k_blk), (1, 0, 2))
    row = lambda a: a.reshape(np_, 1, ps)
    ldt = jnp.broadcast_to(log_dt[:, None], (SSM_GROUPS, st))
    d_lane = jnp.tile(d.reshape(np_, 1, pc), (1, 1, CHUNK))
    bt = lambda a: jnp.transpose(a, (0, 2, 1)).reshape(np_, pc, st)
    spec = lambda *shape: pl.BlockSpec((1,) + shape, lambda p: (p, 0, 0))
    whole = lambda rows: pl.BlockSpec((rows, mod_cols), lambda p: (0, 0))
    return pl.pallas_call(
        _prep_kernel,
        grid=(np_,),
        in_specs=[spec(bsz, k_blk),
                  pl.BlockSpec((k_blk, mod_cols), lambda p: (p, 0)), whole(1),
                  spec(1, ps), spec(1, ps), spec(1, ps), spec(pc, st), spec(pc, st),
                  spec(pc, st), spec(pc, st), spec(1, PAIR_ROWS)],
        out_specs=[whole(bsz),
                   spec(PAIR_ROWS, PAIR_ROWS), spec(PAIR_ROWS, 2 * ps),
                   spec(PAIR_ROWS, 2 * ps), spec(1, ps), spec(1, ps)],
        out_shape=[jax.ShapeDtypeStruct((bsz, mod_cols), F32),
                   jax.ShapeDtypeStruct((np_, PAIR_ROWS, PAIR_ROWS), BF16),
                   jax.ShapeDtypeStruct((np_, PAIR_ROWS, 2 * ps), BF16),
                   jax.ShapeDtypeStruct((np_, PAIR_ROWS, 2 * ps), BF16),
                   jax.ShapeDtypeStruct((np_, 1, ps), F32),
                   jax.ShapeDtypeStruct((np_, 1, ps), F32)],
        scratch_shapes=[pltpu.VMEM((PAIR_ROWS, 2 * ps), F32)] * 2,
        name="prep",
    )(c_blocks, w_ada, b_ada.reshape(1, -1), row(a_re), row(a_im), row(ldt),
      c_re.reshape(np_, pc, st), c_im.reshape(np_, pc, st), bt(b_re), bt(b_im), d_lane)


def _ssm_in_kernel(x_hbm, mod_ref, g_ref, w_ref, o_ref, xbuf, sems, wt_s):
    n_chunks = o_ref.shape[-1]
    n_s = pl.num_programs(1)
    flat = pl.program_id(0) * n_s + pl.program_id(1)
    slot = flat % 2

    def gather(block, buf_slot):
        bb, c0 = block // n_s, (block % n_s) * n_chunks
        return [pltpu.make_async_copy(
            x_hbm.at[bb, pl.ds(c0, n_chunks), tau, :],
            xbuf.at[buf_slot, pl.ds(tau * n_chunks, n_chunks), :],
            sems.at[buf_slot]) for tau in range(CHUNK)]

    @pl.when(flat == 0)
    def _():
        for cp in gather(flat, slot):
            cp.start()
        wt_s[...] = w_ref[...].T.astype(BF16)

    @pl.when(flat + 1 < pl.num_programs(0) * n_s)
    def _():
        for cp in gather(flat + 1, 1 - slot):
            cp.start()

    for cp in gather(flat, slot):
        cp.wait()

    h = _rms_mod(xbuf[slot], g_ref[...], mod_ref[0, 1:2, :], mod_ref[0, 0:1, :])
    u_t = lax.dot_general(wt_s[...], h.astype(BF16), (((1,), (1,)), ((), ())),
                          preferred_element_type=F32)
    for tau in range(CHUNK):
        o_ref[0, tau] = u_t[:, tau * n_chunks:(tau + 1) * n_chunks].astype(o_ref.dtype)


def _ssm_in_call(x, mod, g, w_in, chunks_per_step):
    bsz, seq, _ = x.shape
    ts = chunks_per_step * CHUNK
    ssm_col_block = 2 * CONV_WIDTH // SSM_WIDTH
    x_chunked = x.reshape(bsz, seq // CHUNK, CHUNK, D_MODEL)
    return pl.pallas_call(
        _ssm_in_kernel,
        grid=(bsz, seq // ts),
        in_specs=[
            pl.BlockSpec(memory_space=pl.ANY),
            pl.BlockSpec((1, 6, D_MODEL), lambda b, s: (b, 0, 0)),
            pl.BlockSpec((1, D_MODEL), lambda b, s: (0, 0)),
            pl.BlockSpec((D_MODEL, SSM_WIDTH), lambda b, s: (0, ssm_col_block)),
        ],
        out_specs=pl.BlockSpec((1, CHUNK, SSM_WIDTH, chunks_per_step),
                               lambda b, s: (b, 0, 0, s)),
        out_shape=jax.ShapeDtypeStruct((bsz, CHUNK, SSM_WIDTH, seq // CHUNK), BF16),
        scratch_shapes=[pltpu.VMEM((2, ts, D_MODEL), F32),
                        pltpu.SemaphoreType.DMA((2,)),
                        pltpu.VMEM((SSM_WIDTH, D_MODEL), BF16)],
        compiler_params=pltpu.CompilerParams(
            dimension_semantics=("arbitrary", "arbitrary"),
            vmem_limit_bytes=VMEM_LIMIT),
        name="ssm_in",
    )(x_chunked, mod, g, w_in)


def _ssm_kernel(u_ref, kt_ref, bst_ref, cst_ref, lre_ref, lim_ref, *rest):
    n_w = (len(rest) - 7) // 2
    w_in_refs, o_ref, w_out_refs = rest[:n_w], rest[n_w], rest[n_w + 1:2 * n_w + 1]
    sre, sim, xre, xim, ys, yn = rest[2 * n_w + 1:]
    n_chunks = u_ref.shape[-1]
    pw = PAIR_ST

    for src, dst in zip(w_in_refs, w_out_refs):
        dst[...] = src[...].astype(BF16)

    def load_pair(j):
        r = u_ref[0, :, j * PAIR_CH:(j + 1) * PAIR_CH, :]
        return r.reshape(PAIR_ROWS, n_chunks)

    for j in range(PAIRS_PER_STEP):
        s = lax.dot_general(load_pair(j), bst_ref[j], (((0,), (0,)), ((), ())),
                            preferred_element_type=F32)
        sre[:, j * pw:(j + 1) * pw] = s[:, :pw]
        sim[:, j * pw:(j + 1) * pw] = s[:, pw:]

    lre = lre_ref[0]
    lim = lim_ref[0]

    def body(c, carry):
        xr, xi = carry
        xre[pl.ds(c, 1), :] = xr
        xim[pl.ds(c, 1), :] = xi
        nr = lre * xr - lim * xi + sre[pl.ds(c, 1), :]
        ni = lre * xi + lim * xr + sim[pl.ds(c, 1), :]
        return nr, ni

    zero = jnp.zeros((1, PAIRS_PER_STEP * pw), F32)
    lax.fori_loop(0, n_chunks, body, (zero, zero), unroll=8)

    for j in range(PAIRS_PER_STEP):
        y_intra = jnp.dot(kt_ref[j], load_pair(j), preferred_element_type=F32)
        xp = jnp.concatenate([xre[:, j * pw:(j + 1) * pw],
                              xim[:, j * pw:(j + 1) * pw]], axis=1).astype(BF16)
        y_state = lax.dot_general(cst_ref[j], xp, (((1,), (1,)), ((), ())),
                                  preferred_element_type=F32)
        y = jax.nn.gelu(y_intra + y_state)
        ys[:, j * PAIR_CH:(j + 1) * PAIR_CH, :] = y.reshape(CHUNK, PAIR_CH, n_chunks)

    for k in range(yn.shape[0]):
        lanes = slice(k * LANES, (k + 1) * LANES)
        for t in range(CHUNK):
            yn[k, pl.ds(t, n_chunks, stride=CHUNK), :] = ys[t, lanes, :].T
        o_ref[0, :, lanes] = yn[k].astype(o_ref.dtype)


def _ssm_call(u_t, kt, bst, cst, lam_re, lam_im, weights):
    bsz, _, _, n_chunks = u_t.shape
    seq = n_chunks * CHUNK
    pps = PAIRS_PER_STEP
    ch = pps * PAIR_CH
    state_w = pps * PAIR_ST
    steps = N_PAIRS // pps
    lam_re = lam_re.reshape(steps, 1, state_w)
    lam_im = lam_im.reshape(steps, 1, state_w)
    n_steps = steps * bsz
    w_specs = [pl.BlockSpec((w.shape[0] // n_steps, w.shape[1]),
                            lambda p, b: (p * bsz + b, 0)) for w in weights]
    outs = pl.pallas_call(
        _ssm_kernel,
        grid=(steps, bsz),
        in_specs=[
            pl.BlockSpec((1, CHUNK, ch, n_chunks), lambda p, b: (b, 0, p, 0)),
            pl.BlockSpec((pps, PAIR_ROWS, PAIR_ROWS), lambda p, b: (p, 0, 0)),
            pl.BlockSpec((pps, PAIR_ROWS, 2 * PAIR_ST), lambda p, b: (p, 0, 0)),
            pl.BlockSpec((pps, PAIR_ROWS, 2 * PAIR_ST), lambda p, b: (p, 0, 0)),
            pl.BlockSpec((1, 1, state_w), lambda p, b: (p, 0, 0)),
            pl.BlockSpec((1, 1, state_w), lambda p, b: (p, 0, 0)),
        ] + w_specs,
        out_specs=[pl.BlockSpec((1, seq, ch), lambda p, b: (b, 0, p))] + w_specs,
        out_shape=[jax.ShapeDtypeStruct((bsz, seq, SSM_WIDTH), BF16)] + [
            jax.ShapeDtypeStruct(w.shape, BF16) for w in weights],
        scratch_shapes=[pltpu.VMEM((n_chunks, state_w), F32)] * 4 + [
            pltpu.VMEM((CHUNK, ch, n_chunks), F32),
            pltpu.VMEM((ch // LANES, seq, LANES), F32)],
        compiler_params=pltpu.CompilerParams(
            dimension_semantics=("arbitrary", "arbitrary"),
            vmem_limit_bytes=VMEM_LIMIT),
        name="ssm",
    )(u_t, kt, bst, cst, lam_re, lam_im, *weights)
    return outs[0], outs[1:]


def _mixer_kernel(x_ref, y_ref, mod_ref, g_ref, wc_ref, wgc0_ref, wgc1_ref, wgs0_ref,
                  wgs1_ref, cw_ref, cb_ref, lng_ref, lnb_ref, proj_ref, glu_ref,
                  wout_ref, wf_ref, o_ref, wf_o_ref, ubuf, cbuf, sbuf):
    wf_o_ref[...] = wf_ref[...].astype(BF16)

    @pl.when(pl.program_id(1) == 0)
    def _():
        ubuf[:, 0:CONV_HALO, :] = jnp.zeros((CONV_WIDTH // LANES, CONV_HALO, LANES), F32)

    for r0 in range(0, x_ref.shape[1], MIX_ROWS):
        _mixer_rows(r0, x_ref, y_ref, mod_ref, g_ref, wc_ref, wgc0_ref, wgc1_ref,
                    wgs0_ref, wgs1_ref, cw_ref, cb_ref, lng_ref, lnb_ref, proj_ref,
                    glu_ref, wout_ref, o_ref, ubuf, cbuf, sbuf)


def _mixer_rows(r0, x_ref, y_ref, mod_ref, g_ref, wc_ref, wgc0_ref, wgc1_ref, wgs0_ref,
                wgs1_ref, cw_ref, cb_ref, lng_ref, lnb_ref, proj_ref, glu_ref,
                wout_ref, o_ref, ubuf, cbuf, sbuf):
    ts = MIX_ROWS
    rows = slice(r0, r0 + ts)
    halo = CONV_HALO
    half = D_MODEL // 2
    n_lt = CONV_WIDTH // LANES

    x = x_ref[0, rows, :]
    h = _rms_mod(x, g_ref[...], mod_ref[0, 1:2, :], mod_ref[0, 0:1, :]).astype(BF16)

    zc = jnp.dot(h, wc_ref[...], preferred_element_type=F32)
    u = zc[:, :CONV_WIDTH] * _sigmoid(zc[:, CONV_WIDTH:])
    for k in range(n_lt):
        ubuf[k, halo:halo + ts, :] = u[:, k * LANES:(k + 1) * LANES]

    y_in = y_ref[0, rows, :]

    def dot_h(w_ref):
        return jnp.dot(h, w_ref[...], preferred_element_type=F32)

    def dot_y(k):
        return jnp.dot(y_in, glu_ref[:, k * half:(k + 1) * half],
                       preferred_element_type=F32)

    gc0, gs0, za0, zb0, gc1, gs1, za1, zb1 = range(N_SIDE)

    def put(k, v):
        sbuf[k] = v

    dyn_zero = jnp.minimum(pl.program_id(1), 0)
    never = lax.broadcasted_iota(jnp.int32, (SUBLANES, LANES), 0) < dyn_zero

    def anchored(term, k):
        fence = sbuf[dyn_zero, k * SUBLANES:(k + 1) * SUBLANES, 0:LANES]
        head = jnp.where(never, fence, term[0:SUBLANES, :])
        return jnp.concatenate([head, term[SUBLANES:, :]], axis=0)

    side_jobs = [
        lambda: put(gc0, dot_h(wgc0_ref)), lambda: put(gs0, dot_h(wgs0_ref)),
        lambda: put(za0, dot_y(0)), lambda: put(zb0, dot_y(2)),
        lambda: put(gc1, dot_h(wgc1_ref)), lambda: put(gs1, dot_h(wgs1_ref)),
        lambda: put(za1, dot_y(1)), lambda: put(zb1, dot_y(3)),
    ]
    first = halo - (CONV_KERNEL - 1)
    n_rt = ts // CONV_ROWS
    for i in range(n_rt):
        for k in range(n_lt):
            lanes = slice(k * LANES, (k + 1) * LANES)
            acc = None
            for j in range(CONV_KERNEL):
                row0 = first + j + i * CONV_ROWS
                term = cw_ref[j:j + 1, lanes] * ubuf[k, row0:row0 + CONV_ROWS, :]
                if j == 0 and i > 0:
                    term = anchored(term, k)
                acc = term if acc is None else acc + term
            cbuf[i * CONV_ROWS:(i + 1) * CONV_ROWS, lanes] = acc + cb_ref[:, lanes]
        if (i + 1) * len(side_jobs) % n_rt == 0:
            side_jobs[(i + 1) * len(side_jobs) // n_rt - 1]()
    for k in range(n_lt):
        ubuf[k, 0:halo, :] = ubuf[k, ts:ts + halo, :]
    acc = cbuf[...]
    mu = jnp.mean(acc, axis=-1, keepdims=True)
    cen = acc - mu
    var = jnp.mean(cen * cen, axis=-1, keepdims=True)
    yn = cen * lax.rsqrt(var + EPS) * lng_ref[...] + lnb_ref[...]
    yn = yn * _sigmoid(yn)
    y_conv = jnp.dot(yn.astype(BF16), proj_ref[...], preferred_element_type=F32)

    def merge(conv_half, g_conv, g_ssm, z_a, z_b):
        y_ssm = sbuf[z_a] * _sigmoid(sbuf[z_b])
        return _sigmoid(sbuf[g_conv]) * conv_half + _sigmoid(sbuf[g_ssm]) * y_ssm

    merged = jnp.concatenate(
        [merge(y_conv[:, :half], gc0, gs0, za0, zb0),
         merge(y_conv[:, half:], gc1, gs1, za1, zb1)], axis=1)
    upd = jnp.dot(merged.astype(BF16), wout_ref[...], preferred_element_type=F32)
    o_ref[0, rows, :] = x + mod_ref[0, 2:3, :] * upd


def _const_spec(shape, index=None):
    index = (0,) * len(shape) if index is None else index
    return pl.BlockSpec(shape, lambda b, s: index, pipeline_mode=pl.Buffered(1))


def _mixer_call(x, y, mod, g, w_in_b, cw, cb, lng, lnb, proj, glu, wout, w_ffn_in, ts):
    bsz, seq, _ = x.shape
    n_s = seq // ts
    gate0 = (2 * CONV_WIDTH + SSM_WIDTH) // GATE_BLOCK
    gate_spec = lambda i: _const_spec((D_MODEL, GATE_BLOCK), (0, gate0 + i))
    wf_rows, wf_cols = w_ffn_in.shape
    wf_spec = pl.BlockSpec((wf_rows // (bsz * n_s), wf_cols), lambda b, s: (b * n_s + s, 0))
    return pl.pallas_call(
        _mixer_kernel,
        grid=(bsz, seq // ts),
        in_specs=[
            pl.BlockSpec((1, ts, D_MODEL), lambda b, s: (b, s, 0)),
            pl.BlockSpec((1, ts, SSM_WIDTH), lambda b, s: (b, s, 0)),
            pl.BlockSpec((1, 6, D_MODEL), lambda b, s: (b, 0, 0)),
            _const_spec((1, D_MODEL)),
            _const_spec((D_MODEL, 2 * CONV_WIDTH)),
            gate_spec(0), gate_spec(1), gate_spec(2), gate_spec(3),
            _const_spec((CONV_KERNEL, CONV_WIDTH)),
            _const_spec((1, CONV_WIDTH)),
            _const_spec((1, CONV_WIDTH)),
            _const_spec((1, CONV_WIDTH)),
            _const_spec((CONV_WIDTH, D_MODEL)),
            _const_spec((SSM_WIDTH, 2 * D_MODEL)),
            _const_spec((D_MODEL, D_MODEL)),
            wf_spec,
        ],
        out_specs=[pl.BlockSpec((1, ts, D_MODEL), lambda b, s: (b, s, 0)), wf_spec],
        out_shape=[jax.ShapeDtypeStruct(x.shape, F32),
                   jax.ShapeDtypeStruct(w_ffn_in.shape, BF16)],
        scratch_shapes=[
            pltpu.VMEM((CONV_WIDTH // LANES, MIX_ROWS + CONV_HALO, LANES), F32),
            pltpu.VMEM((MIX_ROWS, CONV_WIDTH), F32),
            pltpu.VMEM((N_SIDE, MIX_ROWS, D_MODEL // 2), F32)],
        compiler_params=pltpu.CompilerParams(
            dimension_semantics=("arbitrary", "arbitrary"),
            vmem_limit_bytes=VMEM_LIMIT),
        name="mixer",
    )(x, y, mod, g, w_in_b, w_in_b, w_in_b, w_in_b, w_in_b, cw, cb, lng, lnb, proj, glu,
      wout, w_ffn_in)


def _ffn_kernel(x_ref, mod_ref, g_ref, wg_ref, wu_ref, wo_ref, fg_ref, o_ref, act_s):
    for r0 in range(0, x_ref.shape[1], FFN_ROWS):
        rows = slice(r0, r0 + FFN_ROWS)
        x = x_ref[0, rows, :]
        h = _rms_mod(x, g_ref[...], mod_ref[0, 4:5, :], mod_ref[0, 3:4, :]).astype(BF16)
        for c0 in range(0, FFN_HIDDEN, FFN_CHUNK):
            cols = slice(c0, c0 + FFN_CHUNK)
            fg = jnp.dot(h, wg_ref[:, cols], preferred_element_type=F32)
            fu = jnp.dot(h, wu_ref[:, cols], preferred_element_type=F32)
            act_s[rows, cols] = (fg * _sigmoid(fg) * fu).astype(BF16)
        upd = jnp.dot(act_s[rows, :], wo_ref[...], preferred_element_type=F32)
        x2 = x + mod_ref[0, 5:6, :] * upd
        ms = jnp.mean(x2 * x2, axis=-1, keepdims=True)
        o_ref[0, rows, :] = x2 * lax.rsqrt(ms + EPS) * fg_ref[...]


def _ffn_call(x, mod, g, w_ffn_in_b, wo, final_g, ts):
    bsz, seq, _ = x.shape
    return pl.pallas_call(
        _ffn_kernel,
        grid=(bsz, seq // ts),
        in_specs=[
            pl.BlockSpec((1, ts, D_MODEL), lambda b, s: (b, s, 0)),
            pl.BlockSpec((1, 6, D_MODEL), lambda b, s: (b, 0, 0)),
            _const_spec((1, D_MODEL)),
            _const_spec((D_MODEL, FFN_HIDDEN), (0, 0)),
            _const_spec((D_MODEL, FFN_HIDDEN), (0, 1)),
            _const_spec((FFN_HIDDEN, D_MODEL)),
            _const_spec((1, D_MODEL)),
        ],
        out_specs=pl.BlockSpec((1, ts, D_MODEL), lambda b, s: (b, s, 0)),
        out_shape=jax.ShapeDtypeStruct(x.shape, F32),
        scratch_shapes=[pltpu.VMEM((ts, FFN_HIDDEN), BF16)],
        compiler_params=pltpu.CompilerParams(
            dimension_semantics=("arbitrary", "arbitrary"),
            vmem_limit_bytes=VMEM_LIMIT),
        name="ffn",
    )(x, mod, g, w_ffn_in_b, w_ffn_in_b, wo, final_g)


def kernel(x, c, w_ada, b_ada, norm1_g, w_in, conv_w, conv_b, conv_ln_g, conv_ln_b,
           conv_proj, ssm_a_re, ssm_a_im, ssm_b_re, ssm_b_im, ssm_c_re, ssm_c_im,
           ssm_d, ssm_log_dt, ssm_glu, w_out, norm2_g, w_ffn_in, w_ffn_out, final_g):
    bsz, seq, _ = x.shape
    assert w_ada.shape[0] == 1, "single layer only"

    mod, *ops = _prep_call(c, w_ada[0], b_ada[0], ssm_a_re[0], ssm_a_im[0], ssm_b_re[0],
                           ssm_b_im[0], ssm_c_re[0], ssm_c_im[0], ssm_d[0],
                           ssm_log_dt[0])
    mod = mod.reshape(bsz, -1, D_MODEL)

    g1 = norm1_g[0].reshape(1, -1)

    u_t = _ssm_in_call(x, mod, g1, w_in[0], 128)
    y, (w_in_b, proj_b, glu_b, w_out_b, w_ffn_out_b) = _ssm_call(
        u_t, *ops, (w_in[0], conv_proj[0], ssm_glu[0], w_out[0], w_ffn_out[0]))

    x1, w_ffn_in_b = _mixer_call(
        x, y, mod, g1, w_in_b, conv_w[0], conv_b[0].reshape(1, -1),
        conv_ln_g[0].reshape(1, -1), conv_ln_b[0].reshape(1, -1),
        proj_b, glu_b, w_out_b, w_ffn_in[0], 1024)

    return _ffn_call(x1, mod, norm2_g[0].reshape(1, -1), w_ffn_in_b, w_ffn_out_b,
                     final_g.reshape(1, -1), 1024)
```

```python
import jax
import jax.numpy as jnp
from jax import lax
from jax.experimental import pallas as pl
from jax.experimental.pallas import tpu as pltpu

F32 = jnp.float32
BF16 = jnp.bfloat16

D_MODEL = 1024
CONV_WIDTH = 512
CONV_KERNEL = 31
SSM_WIDTH = 512
SSM_GROUP = 16
SSM_GROUPS = 32
SSM_STATE = 64
FFN_HIDDEN = 2816
EPS = 1e-6

CHUNK = 16
PAIR_CH = 2 * SSM_GROUP
PAIR_ST = 2 * SSM_STATE
PAIR_ROWS = CHUNK * PAIR_CH
N_PAIRS = SSM_GROUPS // 2
PAIRS_PER_STEP = 8
CONV_HALO = 32
GATE_BLOCK = 512
LANES = 128
SUBLANES = 8
CONV_ROWS = 64
FFN_CHUNK = 256
N_SIDE = 8
MIX_ROWS = 512

VMEM_LIMIT = 56 * 1024 * 1024


def _rms_mod(x, g, scale, shift):
    ms = jnp.mean(x * x, axis=-1, keepdims=True)
    return (x * lax.rsqrt(ms + EPS)) * (g * (1.0 + scale)) + shift


def _sigmoid(v):
    return jax.nn.sigmoid(v)


def _split_bf16(v):
    hi = v.astype(BF16)
    return hi, (v - hi.astype(F32)).astype(BF16)


def _cast_rows(src_refs, dst_refs):
    for src, dst in zip(src_refs, dst_refs):
        dst[...] = src[...].astype(BF16)


def _prep_kernel(c_ref, wada_ref, bada_ref, ar_ref, ai_ref, ldt_ref, cre_ref, cim_ref,
                 bre_ref, bim_ref, dl_ref, *rest):
    n_w = (len(rest) - 8) // 2
    mod_ref, kt_ref, bst_ref, cst_ref, lre_ref, lim_ref = rest[n_w:n_w + 6]
    cl_s, lb_s = rest[2 * n_w + 6:]
    _cast_rows(rest[:n_w], rest[n_w + 6:2 * n_w + 6])

    c = c_ref[...]
    ca = (c * _sigmoid(c)).astype(BF16)
    mod_ref[...] = jnp.dot(ca, wada_ref[...].astype(BF16),
                           preferred_element_type=F32) + bada_ref[...]

    ar, ai = ar_ref[0], ai_ref[0]
    dt = jnp.exp(ldt_ref[0])
    dar, dai = dt * ar, dt * ai
    mag, cos1, sin1 = jnp.exp(dar), jnp.cos(dai), jnp.sin(dai)
    n_re, n_im = mag * cos1 - 1.0, mag * sin1
    den = ar * ar + ai * ai
    q_re = (n_re * ar + n_im * ai) / den
    q_im = (n_im * ar - n_re * ai) / den

    row = lax.broadcasted_iota(jnp.int32, (PAIR_CH, PAIR_ST), 0)
    lane = lax.broadcasted_iota(jnp.int32, (PAIR_CH, PAIR_ST), 1)
    same_group = (row // SSM_GROUP) == (lane // SSM_STATE)

    def block_diag(ref):
        v = ref[0]
        return jnp.where(same_group, jnp.concatenate([v, v], axis=1), 0.0)

    c_re, c_im = block_diag(cre_ref), block_diag(cim_ref)
    b_re, b_im = block_diag(bre_ref), block_diag(bim_ref)
    bb_re = q_re * b_re - q_im * b_im
    bb_im = q_re * b_im + q_im * b_re

    inv_mag = jnp.exp(-dar)
    one = jnp.ones_like(dar)
    powers = {0: (one, jnp.zeros_like(dar)), 1: (mag * cos1, mag * sin1),
              -1: (inv_mag * cos1, -(inv_mag * sin1))}

    def power(k):
        if k not in powers:
            step = 1 if k > 0 else -1
            (p_re, p_im), (s_re, s_im) = power(k - step), powers[step]
            powers[k] = (p_re * s_re - p_im * s_im, p_re * s_im + p_im * s_re)
        return powers[k]

    def cmul_cat(x_re, x_im, p, conj_out):
        p_re, p_im = p
        o_re = x_re * p_re - x_im * p_im
        o_im = x_re * p_im + x_im * p_re
        return jnp.concatenate([o_re, -o_im if conj_out else o_im], axis=1)

    for t in range(CHUNK):
        rs = slice(t * PAIR_CH, (t + 1) * PAIR_CH)
        cl_s[rs, :] = cmul_cat(c_re, c_im, power(t), True)
        lb_s[rs, :] = cmul_cat(bb_re, bb_im, power(-t), False)
        cst_ref[0, rs, :] = cmul_cat(c_re, c_im, power(t + 1), True).astype(BF16)
        bst_ref[0, rs, :] = cmul_cat(bb_re, bb_im, power(CHUNK - 1 - t),
                                     False).astype(BF16)
    lre_ref[0], lim_ref[0] = power(CHUNK)

    cl_hi, cl_lo = _split_bf16(cl_s[...])
    lb_hi, lb_lo = _split_bf16(lb_s[...])

    def dot_nt(a, b):
        return lax.dot_general(a, b, (((1,), (1,)), ((), ())), preferred_element_type=F32)

    k = dot_nt(cl_hi, lb_hi) + (dot_nt(cl_hi, lb_lo) + dot_nt(cl_lo, lb_hi))
    ri = lax.broadcasted_iota(jnp.int32, (PAIR_ROWS, PAIR_ROWS), 0)
    ci = lax.broadcasted_iota(jnp.int32, (PAIR_ROWS, PAIR_ROWS), 1)
    k = jnp.where(ri // PAIR_CH >= ci // PAIR_CH, k, 0.0)
    k = k + jnp.where(ri == ci, dl_ref[0], 0.0)
    kt_ref[0] = k.astype(BF16)


def _prep_call(c, w_ada, b_ada, a_re, a_im, b_re, b_im, c_re, c_im, d, log_dt, weights):
    np_, pc, ps, st = N_PAIRS, PAIR_CH, PAIR_ST, SSM_STATE
    bsz = c.shape[0]
    mod_cols = w_ada.shape[1]
    mod_blk = mod_cols // np_
    row = lambda a: a.reshape(np_, 1, ps)
    ldt = jnp.broadcast_to(log_dt[:, None], (SSM_GROUPS, st))
    d_lane = jnp.tile(d.reshape(np_, 1, pc), (1, 1, CHUNK))
    bt = lambda a: jnp.transpose(a, (0, 2, 1)).reshape(np_, pc, st)
    spec = lambda *shape: pl.BlockSpec((1,) + shape, lambda p: (p, 0, 0))
    col_spec = lambda rows: pl.BlockSpec((rows, mod_blk), lambda p: (0, p))
    w_specs = [pl.BlockSpec((w.shape[0] // np_, w.shape[1]), lambda p: (p, 0))
               for w in weights]
    outs = pl.pallas_call(
        _prep_kernel,
        grid=(np_,),
        in_specs=[pl.BlockSpec((bsz, D_MODEL), lambda p: (0, 0)),
                  col_spec(D_MODEL), col_spec(1),
                  spec(1, ps), spec(1, ps), spec(1, ps), spec(pc, st), spec(pc, st),
                  spec(pc, st), spec(pc, st), spec(1, PAIR_ROWS)] + w_specs,
        out_specs=[col_spec(bsz),
                   spec(PAIR_ROWS, PAIR_ROWS), spec(PAIR_ROWS, 2 * ps),
                   spec(PAIR_ROWS, 2 * ps), spec(1, ps), spec(1, ps)] + w_specs,
        out_shape=[jax.ShapeDtypeStruct((bsz, mod_cols), F32),
                   jax.ShapeDtypeStruct((np_, PAIR_ROWS, PAIR_ROWS), BF16),
                   jax.ShapeDtypeStruct((np_, PAIR_ROWS, 2 * ps), BF16),
                   jax.ShapeDtypeStruct((np_, PAIR_ROWS, 2 * ps), BF16),
                   jax.ShapeDtypeStruct((np_, 1, ps), F32),
                   jax.ShapeDtypeStruct((np_, 1, ps), F32)] + [
                       jax.ShapeDtypeStruct(w.shape, BF16) for w in weights],
        scratch_shapes=[pltpu.VMEM((PAIR_ROWS, 2 * ps), F32)] * 2,
        name="prep",
    )(c, w_ada, b_ada.reshape(1, -1), row(a_re), row(a_im), row(ldt),
      c_re.reshape(np_, pc, st), c_im.reshape(np_, pc, st), bt(b_re), bt(b_im), d_lane,
      *weights)
    return outs[0], outs[1:6], outs[6:]


def _ssm_in_kernel(x_hbm, mod_ref, g_ref, w_ref, o_ref, xbuf, sems, wt_s):
    n_chunks = o_ref.shape[-1]
    n_s = pl.num_programs(1)
    flat = pl.program_id(0) * n_s + pl.program_id(1)
    slot = flat % 2

    def gather(block, buf_slot):
        bb, c0 = block // n_s, (block % n_s) * n_chunks
        return [pltpu.make_async_copy(
            x_hbm.at[bb, pl.ds(c0, n_chunks), tau, :],
            xbuf.at[buf_slot, pl.ds(tau * n_chunks, n_chunks), :],
            sems.at[buf_slot]) for tau in range(CHUNK)]

    @pl.when(flat == 0)
    def _():
        for cp in gather(flat, slot):
            cp.start()
        wt_s[...] = w_ref[...].T.astype(BF16)

    @pl.when(flat + 1 < pl.num_programs(0) * n_s)
    def _():
        for cp in gather(flat + 1, 1 - slot):
            cp.start()

    for cp in gather(flat, slot):
        cp.wait()

    h = _rms_mod(xbuf[slot], g_ref[...], mod_ref[0, 1:2, :], mod_ref[0, 0:1, :])
    u_t = lax.dot_general(wt_s[...], h.astype(BF16), (((1,), (1,)), ((), ())),
                          preferred_element_type=F32)
    for tau in range(CHUNK):
        o_ref[0, tau] = u_t[:, tau * n_chunks:(tau + 1) * n_chunks].astype(o_ref.dtype)


def _ssm_in_call(x, mod, g, w_in, chunks_per_step):
    bsz, seq, _ = x.shape
    ts = chunks_per_step * CHUNK
    ssm_col_block = 2 * CONV_WIDTH // SSM_WIDTH
    x_chunked = x.reshape(bsz, seq // CHUNK, CHUNK, D_MODEL)
    return pl.pallas_call(
        _ssm_in_kernel,
        grid=(bsz, seq // ts),
        in_specs=[
            pl.BlockSpec(memory_space=pl.ANY),
            pl.BlockSpec((1, 6, D_MODEL), lambda b, s: (b, 0, 0)),
            pl.BlockSpec((1, D_MODEL), lambda b, s: (0, 0)),
            pl.BlockSpec((D_MODEL, SSM_WIDTH), lambda b, s: (0, ssm_col_block)),
        ],
        out_specs=pl.BlockSpec((1, CHUNK, SSM_WIDTH, chunks_per_step),
                               lambda b, s: (b, 0, 0, s)),
        out_shape=jax.ShapeDtypeStruct((bsz, CHUNK, SSM_WIDTH, seq // CHUNK), BF16),
        scratch_shapes=[pltpu.VMEM((2, ts, D_MODEL), F32),
                        pltpu.SemaphoreType.DMA((2,)),
                        pltpu.VMEM((SSM_WIDTH, D_MODEL), BF16)],
        compiler_params=pltpu.CompilerParams(
            dimension_semantics=("arbitrary", "arbitrary"),
            vmem_limit_bytes=VMEM_LIMIT),
        name="ssm_in",
    )(x_chunked, mod, g, w_in)


def _ssm_kernel(u_ref, kt_ref, bst_ref, cst_ref, lre_ref, lim_ref, *rest):
    n_w = (len(rest) - 7) // 2
    w_in_refs, o_ref, w_out_refs = rest[:n_w], rest[n_w], rest[n_w + 1:2 * n_w + 1]
    sre, sim, xre, xim, ys, yn = rest[2 * n_w + 1:]
    n_chunks = u_ref.shape[-1]
    pw = PAIR_ST

    _cast_rows(w_in_refs, w_out_refs)

    def load_pair(j):
        r = u_ref[0, :, j * PAIR_CH:(j + 1) * PAIR_CH, :]
        return r.reshape(PAIR_ROWS, n_chunks)

    for j in range(PAIRS_PER_STEP):
        s = lax.dot_general(load_pair(j), bst_ref[j], (((0,), (0,)), ((), ())),
                            preferred_element_type=F32)
        sre[:, j * pw:(j + 1) * pw] = s[:, :pw]
        sim[:, j * pw:(j + 1) * pw] = s[:, pw:]

    lre = lre_ref[0]
    lim = lim_ref[0]

    def body(c, carry):
        xr, xi = carry
        xre[pl.ds(c, 1), :] = xr
        xim[pl.ds(c, 1), :] = xi
        nr = lre * xr - lim * xi + sre[pl.ds(c, 1), :]
        ni = lre * xi + lim * xr + sim[pl.ds(c, 1), :]
        return nr, ni

    zero = jnp.zeros((1, PAIRS_PER_STEP * pw), F32)
    lax.fori_loop(0, n_chunks, body, (zero, zero), unroll=8)

    for j in range(PAIRS_PER_STEP):
        y_intra = jnp.dot(kt_ref[j], load_pair(j), preferred_element_type=F32)
        xp = jnp.concatenate([xre[:, j * pw:(j + 1) * pw],
                              xim[:, j * pw:(j + 1) * pw]], axis=1).astype(BF16)
        y_state = lax.dot_general(cst_ref[j], xp, (((1,), (1,)), ((), ())),
                                  preferred_element_type=F32)
        y = jax.nn.gelu(y_intra + y_state)
        ys[:, j * PAIR_CH:(j + 1) * PAIR_CH, :] = y.reshape(CHUNK, PAIR_CH, n_chunks)

    for k in range(yn.shape[0]):
        lanes = slice(k * LANES, (k + 1) * LANES)
        for t in range(CHUNK):
            yn[k, pl.ds(t, n_chunks, stride=CHUNK), :] = ys[t, lanes, :].T
        o_ref[0, 0, :, lanes] = yn[k].astype(o_ref.dtype)


def _ssm_call(u_t, kt, bst, cst, lam_re, lam_im, weights):
    bsz, _, _, n_chunks = u_t.shape
    seq = n_chunks * CHUNK
    pps = PAIRS_PER_STEP
    ch = pps * PAIR_CH
    state_w = pps * PAIR_ST
    steps = N_PAIRS // pps
    lam_re = lam_re.reshape(steps, 1, state_w)
    lam_im = lam_im.reshape(steps, 1, state_w)
    n_steps = steps * bsz
    w_specs = [pl.BlockSpec((w.shape[0] // n_steps, w.shape[1]),
                            lambda p, b: (p * bsz + b, 0)) for w in weights]
    outs = pl.pallas_call(
        _ssm_kernel,
        grid=(steps, bsz),
        in_specs=[
            pl.BlockSpec((1, CHUNK, ch, n_chunks), lambda p, b: (b, 0, p, 0)),
            pl.BlockSpec((pps, PAIR_ROWS, PAIR_ROWS), lambda p, b: (p, 0, 0)),
            pl.BlockSpec((pps, PAIR_ROWS, 2 * PAIR_ST), lambda p, b: (p, 0, 0)),
            pl.BlockSpec((pps, PAIR_ROWS, 2 * PAIR_ST), lambda p, b: (p, 0, 0)),
            pl.BlockSpec((1, 1, state_w), lambda p, b: (p, 0, 0)),
            pl.BlockSpec((1, 1, state_w), lambda p, b: (p, 0, 0)),
        ] + w_specs,
        out_specs=[pl.BlockSpec((1, 1, seq, ch), lambda p, b: (b, p, 0, 0))] + w_specs,
        out_shape=[jax.ShapeDtypeStruct((bsz, steps, seq, ch), BF16)] + [
            jax.ShapeDtypeStruct(w.shape, BF16) for w in weights],
        scratch_shapes=[pltpu.VMEM((n_chunks, state_w), F32)] * 4 + [
            pltpu.VMEM((CHUNK, ch, n_chunks), F32),
            pltpu.VMEM((ch // LANES, seq, LANES), F32)],
        compiler_params=pltpu.CompilerParams(
            dimension_semantics=("arbitrary", "arbitrary"),
            vmem_limit_bytes=VMEM_LIMIT),
        name="ssm",
    )(u_t, kt, bst, cst, lam_re, lam_im, *weights)
    return outs[0], outs[1:]


def _mixer_kernel(x_ref, y_ref, mod_ref, g_ref, wc_ref, wgc0_ref, wgc1_ref, wgs0_ref,
                  wgs1_ref, cw_ref, cb_ref, lng_ref, lnb_ref, proj_ref, glu_ref,
                  wout_ref, wfi_ref, wfo_ref, o_ref, wfi_o_ref, wfo_o_ref,
                  ubuf, cbuf, sbuf):
    _cast_rows((wfi_ref, wfo_ref), (wfi_o_ref, wfo_o_ref))

    @pl.when(pl.program_id(1) == 0)
    def _():
        ubuf[:, 0:CONV_HALO, :] = jnp.zeros((CONV_WIDTH // LANES, CONV_HALO, LANES), F32)

    for r0 in range(0, x_ref.shape[1], MIX_ROWS):
        _mixer_rows(r0, x_ref, y_ref, mod_ref, g_ref, wc_ref, wgc0_ref, wgc1_ref,
                    wgs0_ref, wgs1_ref, cw_ref, cb_ref, lng_ref, lnb_ref, proj_ref,
                    glu_ref, wout_ref, o_ref, ubuf, cbuf, sbuf)


def _mixer_rows(r0, x_ref, y_ref, mod_ref, g_ref, wc_ref, wgc0_ref, wgc1_ref, wgs0_ref,
                wgs1_ref, cw_ref, cb_ref, lng_ref, lnb_ref, proj_ref, glu_ref,
                wout_ref, o_ref, ubuf, cbuf, sbuf):
    ts = MIX_ROWS
    rows = slice(r0, r0 + ts)
    halo = CONV_HALO
    half = D_MODEL // 2
    n_lt = CONV_WIDTH // LANES

    x = x_ref[0, rows, :]
    h = _rms_mod(x, g_ref[...], mod_ref[0, 1:2, :], mod_ref[0, 0:1, :]).astype(BF16)

    zc = jnp.dot(h, wc_ref[...], preferred_element_type=F32)
    u = zc[:, :CONV_WIDTH] * _sigmoid(zc[:, CONV_WIDTH:])
    for k in range(n_lt):
        ubuf[k, halo:halo + ts, :] = u[:, k * LANES:(k + 1) * LANES]

    y_in = jnp.concatenate([y_ref[0, k, rows, :] for k in range(y_ref.shape[1])], axis=1)

    def dot_h(w_ref):
        return jnp.dot(h, w_ref[...], preferred_element_type=F32)

    def dot_y(k):
        return jnp.dot(y_in, glu_ref[:, k * half:(k + 1) * half],
                       preferred_element_type=F32)

    gc0, gs0, za0, zb0, gc1, gs1, za1, zb1 = range(N_SIDE)

    def put(k, v):
        sbuf[k] = v

    dyn_zero = jnp.minimum(pl.program_id(1), 0)
    never = lax.broadcasted_iota(jnp.int32, (SUBLANES, LANES), 0) < dyn_zero

    def anchored(term, k):
        fence = sbuf[dyn_zero, k * SUBLANES:(k + 1) * SUBLANES, 0:LANES]
        head = jnp.where(never, fence, term[0:SUBLANES, :])
        return jnp.concatenate([head, term[SUBLANES:, :]], axis=0)

    side_jobs = [
        lambda: put(gc0, dot_h(wgc0_ref)), lambda: put(gs0, dot_h(wgs0_ref)),
        lambda: put(za0, dot_y(0)), lambda: put(zb0, dot_y(2)),
        lambda: put(gc1, dot_h(wgc1_ref)), lambda: put(gs1, dot_h(wgs1_ref)),
        lambda: put(za1, dot_y(1)), lambda: put(zb1, dot_y(3)),
    ]
    first = halo - (CONV_KERNEL - 1)
    n_rt = ts // CONV_ROWS
    for i in range(n_rt):
        for k in range(n_lt):
            lanes = slice(k * LANES, (k + 1) * LANES)
            acc = None
            for j in range(CONV_KERNEL):
                row0 = first + j + i * CONV_ROWS
                term = cw_ref[j:j + 1, lanes] * ubuf[k, row0:row0 + CONV_ROWS, :]
                if j == 0 and i > 0:
                    term = anchored(term, k)
                acc = term if acc is None else acc + term
            cbuf[i * CONV_ROWS:(i + 1) * CONV_ROWS, lanes] = acc + cb_ref[:, lanes]
        if (i + 1) * len(side_jobs) % n_rt == 0:
            side_jobs[(i + 1) * len(side_jobs) // n_rt - 1]()
    for k in range(n_lt):
        ubuf[k, 0:halo, :] = ubuf[k, ts:ts + halo, :]
    acc = cbuf[...]
    mu = jnp.mean(acc, axis=-1, keepdims=True)
    cen = acc - mu
    var = jnp.mean(cen * cen, axis=-1, keepdims=True)
    yn = cen * lax.rsqrt(var + EPS) * lng_ref[...] + lnb_ref[...]
    yn = yn * _sigmoid(yn)
    y_conv = jnp.dot(yn.astype(BF16), proj_ref[...], preferred_element_type=F32)

    def merge(conv_half, g_conv, g_ssm, z_a, z_b):
        y_ssm = sbuf[z_a] * _sigmoid(sbuf[z_b])
        return _sigmoid(sbuf[g_conv]) * conv_half + _sigmoid(sbuf[g_ssm]) * y_ssm

    merged = jnp.concatenate(
        [merge(y_conv[:, :half], gc0, gs0, za0, zb0),
         merge(y_conv[:, half:], gc1, gs1, za1, zb1)], axis=1)
    upd = jnp.dot(merged.astype(BF16), wout_ref[...], preferred_element_type=F32)
    o_ref[0, rows, :] = x + mod_ref[0, 2:3, :] * upd


def _const_spec(shape, index=None):
    index = (0,) * len(shape) if index is None else index
    return pl.BlockSpec(shape, lambda b, s: index, pipeline_mode=pl.Buffered(1))


def _mixer_call(x, y, mod, g, w_in_b, cw, cb, lng, lnb, proj, glu, wout, ffn_weights, ts):
    bsz, seq, _ = x.shape
    n_s = seq // ts
    gate0 = (2 * CONV_WIDTH + SSM_WIDTH) // GATE_BLOCK
    gate_spec = lambda i: _const_spec((D_MODEL, GATE_BLOCK), (0, gate0 + i))
    wf_specs = [pl.BlockSpec((w.shape[0] // (bsz * n_s), w.shape[1]),
                             lambda b, s: (b * n_s + s, 0)) for w in ffn_weights]
    return pl.pallas_call(
        _mixer_kernel,
        grid=(bsz, seq // ts),
        in_specs=[
            pl.BlockSpec((1, ts, D_MODEL), lambda b, s: (b, s, 0)),
            pl.BlockSpec((1, y.shape[1], ts, y.shape[3]), lambda b, s: (b, 0, s, 0)),
            pl.BlockSpec((1, 6, D_MODEL), lambda b, s: (b, 0, 0)),
            _const_spec((1, D_MODEL)),
            _const_spec((D_MODEL, 2 * CONV_WIDTH)),
            gate_spec(0), gate_spec(1), gate_spec(2), gate_spec(3),
            _const_spec((CONV_KERNEL, CONV_WIDTH)),
            _const_spec((1, CONV_WIDTH)),
            _const_spec((1, CONV_WIDTH)),
            _const_spec((1, CONV_WIDTH)),
            _const_spec((CONV_WIDTH, D_MODEL)),
            _const_spec((SSM_WIDTH, 2 * D_MODEL)),
            _const_spec((D_MODEL, D_MODEL)),
        ] + wf_specs,
        out_specs=[pl.BlockSpec((1, ts, D_MODEL), lambda b, s: (b, s, 0))] + wf_specs,
        out_shape=[jax.ShapeDtypeStruct(x.shape, F32)] + [
            jax.ShapeDtypeStruct(w.shape, BF16) for w in ffn_weights],
        scratch_shapes=[
            pltpu.VMEM((CONV_WIDTH // LANES, MIX_ROWS + CONV_HALO, LANES), F32),
            pltpu.VMEM((MIX_ROWS, CONV_WIDTH), F32),
            pltpu.VMEM((N_SIDE, MIX_ROWS, D_MODEL // 2), F32)],
        compiler_params=pltpu.CompilerParams(
            dimension_semantics=("arbitrary", "arbitrary"),
            vmem_limit_bytes=VMEM_LIMIT),
        name="mixer",
    )(x, y, mod, g, w_in_b, w_in_b, w_in_b, w_in_b, w_in_b, cw, cb, lng, lnb, proj, glu,
      wout, *ffn_weights)


def _ffn_kernel(x_ref, mod_ref, g_ref, wg_ref, wu_ref, wo_ref, fg_ref, o_ref, act_s):
    x = x_ref[0]
    h = _rms_mod(x, g_ref[...], mod_ref[0, 4:5, :], mod_ref[0, 3:4, :]).astype(BF16)
    for c0 in range(0, FFN_HIDDEN, FFN_CHUNK):
        cols = slice(c0, c0 + FFN_CHUNK)
        fg = jnp.dot(h, wg_ref[:, cols], preferred_element_type=F32)
        fu = jnp.dot(h, wu_ref[:, cols], preferred_element_type=F32)
        act_s[:, cols] = (fg * _sigmoid(fg) * fu).astype(BF16)
    upd = jnp.dot(act_s[...], wo_ref[...], preferred_element_type=F32)
    x2 = x + mod_ref[0, 5:6, :] * upd
    ms = jnp.mean(x2 * x2, axis=-1, keepdims=True)
    o_ref[0] = x2 * lax.rsqrt(ms + EPS) * fg_ref[...]


def _ffn_call(x, mod, g, w_ffn_in_b, wo, final_g, ts):
    bsz, seq, _ = x.shape
    return pl.pallas_call(
        _ffn_kernel,
        grid=(bsz, seq // ts),
        in_specs=[
            pl.BlockSpec((1, ts, D_MODEL), lambda b, s: (b, s, 0)),
            pl.BlockSpec((1, 6, D_MODEL), lambda b, s: (b, 0, 0)),
            _const_spec((1, D_MODEL)),
            _const_spec((D_MODEL, FFN_HIDDEN), (0, 0)),
            _const_spec((D_MODEL, FFN_HIDDEN), (0, 1)),
            _const_spec((FFN_HIDDEN, D_MODEL)),
            _const_spec((1, D_MODEL)),
        ],
        out_specs=pl.BlockSpec((1, ts, D_MODEL), lambda b, s: (b, s, 0)),
        out_shape=jax.ShapeDtypeStruct(x.shape, F32),
        scratch_shapes=[pltpu.VMEM((ts, FFN_HIDDEN), BF16)],
        compiler_params=pltpu.CompilerParams(
            dimension_semantics=("arbitrary", "arbitrary"),
            vmem_limit_bytes=VMEM_LIMIT),
        name="ffn",
    )(x, mod, g, w_ffn_in_b, w_ffn_in_b, wo, final_g)


def kernel(x, c, w_ada, b_ada, norm1_g, w_in, conv_w, conv_b, conv_ln_g, conv_ln_b,
           conv_proj, ssm_a_re, ssm_a_im, ssm_b_re, ssm_b_im, ssm_c_re, ssm_c_im,
           ssm_d, ssm_log_dt, ssm_glu, w_out, norm2_g, w_ffn_in, w_ffn_out, final_g):
    bsz, seq, _ = x.shape
    assert w_ada.shape[0] == 1, "single layer only"

    mod, ops, (w_in_b, proj_b, glu_b, w_out_b) = _prep_call(
        c, w_ada[0], b_ada[0], ssm_a_re[0], ssm_a_im[0], ssm_b_re[0], ssm_b_im[0],
        ssm_c_re[0], ssm_c_im[0], ssm_d[0], ssm_log_dt[0],
        (w_in[0], conv_proj[0], ssm_glu[0], w_out[0]))
    mod = mod.reshape(bsz, -1, D_MODEL)

    g1 = norm1_g[0].reshape(1, -1)

    u_t = _ssm_in_call(x, mod, g1, w_in[0], 128)
    y, _ = _ssm_call(u_t, *ops, ())

    x1, w_ffn_in_b, w_ffn_out_b = _mixer_call(
        x, y, mod, g1, w_in_b, conv_w[0], conv_b[0].reshape(1, -1),
        conv_ln_g[0].reshape(1, -1), conv_ln_b[0].reshape(1, -1),
        proj_b, glu_b, w_out_b, (w_ffn_in[0], w_ffn_out[0]), 1024)

    return _ffn_call(x1, mod, norm2_g[0].reshape(1, -1), w_ffn_in_b, w_ffn_out_b,
                     final_g.reshape(1, -1), 1024)
```

```python
import jax
import jax.numpy as jnp
from jax import lax
from jax.experimental import pallas as pl
from jax.experimental.pallas import tpu as pltpu

F32 = jnp.float32
BF16 = jnp.bfloat16

D_MODEL = 1024
CONV_WIDTH = 512
CONV_KERNEL = 31
SSM_WIDTH = 512
SSM_GROUP = 16
SSM_GROUPS = 32
SSM_STATE = 64
FFN_HIDDEN = 2816
EPS = 1e-6

CHUNK = 16
PAIR_CH = 2 * SSM_GROUP
PAIR_ST = 2 * SSM_STATE
PAIR_ROWS = CHUNK * PAIR_CH
N_PAIRS = SSM_GROUPS // 2
PAIRS_PER_STEP = 8
CONV_HALO = 32
GATE_BLOCK = 512
LANES = 128
SUBLANES = 8
CONV_ROWS = 64
FFN_CHUNK = 256
N_SIDE = 8
MIX_ROWS = 512

VMEM_LIMIT = 56 * 1024 * 1024


def _rms_mod(x, g, scale, shift):
    ms = jnp.mean(x * x, axis=-1, keepdims=True)
    return (x * lax.rsqrt(ms + EPS)) * (g * (1.0 + scale)) + shift


def _sigmoid(v):
    return jax.nn.sigmoid(v)


def _split_bf16(v):
    hi = v.astype(BF16)
    return hi, (v - hi.astype(F32)).astype(BF16)


def _cast_rows(src_refs, dst_refs):
    for src, dst in zip(src_refs, dst_refs):
        dst[...] = src[...].astype(BF16)


def _prep_kernel(c_ref, wada_ref, bada_ref, ar_ref, ai_ref, ldt_ref, cre_ref, cim_ref,
                 bre_ref, bim_ref, dl_ref, mod_ref, kt_ref, bst_ref, cst_ref, lre_ref,
                 lim_ref, cl_s):
    c = c_ref[...]
    ca = (c * _sigmoid(c)).astype(BF16)
    mod_ref[...] = jnp.dot(ca, wada_ref[...].astype(BF16),
                           preferred_element_type=F32) + bada_ref[...]

    ar, ai = ar_ref[0], ai_ref[0]
    dt = jnp.exp(ldt_ref[0])
    dar, dai = dt * ar, dt * ai
    mag, cos1, sin1 = jnp.exp(dar), jnp.cos(dai), jnp.sin(dai)
    n_re, n_im = mag * cos1 - 1.0, mag * sin1
    den = ar * ar + ai * ai
    q_re = (n_re * ar + n_im * ai) / den
    q_im = (n_im * ar - n_re * ai) / den

    row = lax.broadcasted_iota(jnp.int32, (PAIR_CH, PAIR_ST), 0)
    lane = lax.broadcasted_iota(jnp.int32, (PAIR_CH, PAIR_ST), 1)
    same_group = (row // SSM_GROUP) == (lane // SSM_STATE)

    def block_diag(ref):
        v = ref[0]
        return jnp.where(same_group, jnp.concatenate([v, v], axis=1), 0.0)

    c_re, c_im = block_diag(cre_ref), block_diag(cim_ref)
    b_re, b_im = block_diag(bre_ref), block_diag(bim_ref)
    bb_re = q_re * b_re - q_im * b_im
    bb_im = q_re * b_im + q_im * b_re

    powers = {0: (jnp.ones_like(dar), jnp.zeros_like(dar)), 1: (mag * cos1, mag * sin1)}

    def power(k):
        if k not in powers:
            (p_re, p_im), (s_re, s_im) = power(k - 1), powers[1]
            powers[k] = (p_re * s_re - p_im * s_im, p_re * s_im + p_im * s_re)
        return powers[k]

    def cmul_cat(x_re, x_im, p, conj_out):
        p_re, p_im = p
        o_re = x_re * p_re - x_im * p_im
        o_im = x_re * p_im + x_im * p_re
        return jnp.concatenate([o_re, -o_im if conj_out else o_im], axis=1)

    for t in range(CHUNK):
        rs = slice(t * PAIR_CH, (t + 1) * PAIR_CH)
        cl_s[rs, :] = cmul_cat(c_re, c_im, power(t), True)
        cst_ref[0, rs, :] = cmul_cat(c_re, c_im, power(t + 1), True).astype(BF16)
        bst_ref[0, rs, :] = cmul_cat(bb_re, bb_im, power(CHUNK - 1 - t),
                                     False).astype(BF16)
    lre_ref[0], lim_ref[0] = power(CHUNK)

    cl_hi, cl_lo = _split_bf16(cl_s[...])
    bb = jnp.concatenate([bb_re, bb_im], axis=1)
    bb = jnp.concatenate([bb, jnp.zeros((LANES - PAIR_CH, 2 * PAIR_ST), F32)], axis=0)
    bb_hi, bb_lo = _split_bf16(bb)

    def dot_nt(a, b):
        return lax.dot_general(a, b, (((1,), (1,)), ((), ())), preferred_element_type=F32)

    kcol = dot_nt(cl_hi, bb_hi) + (dot_nt(cl_hi, bb_lo) + dot_nt(cl_lo, bb_hi))

    per_tile = LANES // PAIR_CH
    ri = lax.broadcasted_iota(jnp.int32, (PAIR_ROWS, LANES), 0)
    ci = lax.broadcasted_iota(jnp.int32, (PAIR_ROWS, LANES), 1)
    for tile in range(PAIR_ROWS // LANES):
        acc = jnp.where(ri == ci + tile * LANES,
                        dl_ref[0, :, tile * LANES:(tile + 1) * LANES], 0.0)
        for q in range(per_tile):
            tau = tile * per_tile + q
            piece = kcol if q == 0 else pltpu.roll(kcol, q * PAIR_CH, axis=1)
            if tau:
                piece = jnp.concatenate(
                    [jnp.zeros((tau * PAIR_CH, LANES), F32),
                     piece[:PAIR_ROWS - tau * PAIR_CH, :]], axis=0)
            acc = acc + piece
        kt_ref[0, :, tile * LANES:(tile + 1) * LANES] = acc.astype(BF16)


def _prep_call(c, w_ada, b_ada, a_re, a_im, b_re, b_im, c_re, c_im, d, log_dt):
    np_, pc, ps, st = N_PAIRS, PAIR_CH, PAIR_ST, SSM_STATE
    bsz = c.shape[0]
    mod_cols = w_ada.shape[1]
    mod_blk = mod_cols // np_
    row = lambda a: a.reshape(np_, 1, ps)
    ldt = jnp.broadcast_to(log_dt[:, None], (SSM_GROUPS, st))
    d_lane = jnp.tile(d.reshape(np_, 1, pc), (1, 1, CHUNK))
    bt = lambda a: jnp.transpose(a, (0, 2, 1)).reshape(np_, pc, st)
    spec = lambda *shape: pl.BlockSpec((1,) + shape, lambda p: (p, 0, 0))
    col_spec = lambda rows: pl.BlockSpec((rows, mod_blk), lambda p: (0, p))
    return pl.pallas_call(
        _prep_kernel,
        grid=(np_,),
        in_specs=[pl.BlockSpec((bsz, D_MODEL), lambda p: (0, 0)),
                  col_spec(D_MODEL), col_spec(1),
                  spec(1, ps), spec(1, ps), spec(1, ps), spec(pc, st), spec(pc, st),
                  spec(pc, st), spec(pc, st), spec(1, PAIR_ROWS)],
        out_specs=[col_spec(bsz),
                   spec(PAIR_ROWS, PAIR_ROWS), spec(PAIR_ROWS, 2 * ps),
                   spec(PAIR_ROWS, 2 * ps), spec(1, ps), spec(1, ps)],
        out_shape=[jax.ShapeDtypeStruct((bsz, mod_cols), F32),
                   jax.ShapeDtypeStruct((np_, PAIR_ROWS, PAIR_ROWS), BF16),
                   jax.ShapeDtypeStruct((np_, PAIR_ROWS, 2 * ps), BF16),
                   jax.ShapeDtypeStruct((np_, PAIR_ROWS, 2 * ps), BF16),
                   jax.ShapeDtypeStruct((np_, 1, ps), F32),
                   jax.ShapeDtypeStruct((np_, 1, ps), F32)],
        scratch_shapes=[pltpu.VMEM((PAIR_ROWS, 2 * ps), F32)],
        name="prep",
    )(c, w_ada, b_ada.reshape(1, -1), row(a_re), row(a_im), row(ldt),
      c_re.reshape(np_, pc, st), c_im.reshape(np_, pc, st), bt(b_re), bt(b_im), d_lane)


def _ssm_in_kernel(x_hbm, mod_ref, g_ref, w_ref, o_ref, xbuf, sems, wt_s):
    n_chunks = o_ref.shape[-1]
    n_s = pl.num_programs(1)
    flat = pl.program_id(0) * n_s + pl.program_id(1)
    slot = flat % 2

    def gather(block, buf_slot):
        bb, c0 = block // n_s, (block % n_s) * n_chunks
        return [pltpu.make_async_copy(
            x_hbm.at[bb, pl.ds(c0, n_chunks), tau, :],
            xbuf.at[buf_slot, pl.ds(tau * n_chunks, n_chunks), :],
            sems.at[buf_slot]) for tau in range(CHUNK)]

    @pl.when(flat == 0)
    def _():
        for cp in gather(flat, slot):
            cp.start()
        wt_s[...] = w_ref[...].T.astype(BF16)

    @pl.when(flat + 1 < pl.num_programs(0) * n_s)
    def _():
        for cp in gather(flat + 1, 1 - slot):
            cp.start()

    for cp in gather(flat, slot):
        cp.wait()

    h = _rms_mod(xbuf[slot], g_ref[...], mod_ref[0, 1:2, :], mod_ref[0, 0:1, :])
    u_t = lax.dot_general(wt_s[...], h.astype(BF16), (((1,), (1,)), ((), ())),
                          preferred_element_type=F32)
    for tau in range(CHUNK):
        o_ref[0, tau] = u_t[:, tau * n_chunks:(tau + 1) * n_chunks].astype(o_ref.dtype)


def _ssm_in_call(x, mod, g, w_in, chunks_per_step):
    bsz, seq, _ = x.shape
    ts = chunks_per_step * CHUNK
    ssm_col_block = 2 * CONV_WIDTH // SSM_WIDTH
    x_chunked = x.reshape(bsz, seq // CHUNK, CHUNK, D_MODEL)
    return pl.pallas_call(
        _ssm_in_kernel,
        grid=(bsz, seq // ts),
        in_specs=[
            pl.BlockSpec(memory_space=pl.ANY),
            pl.BlockSpec((1, 6, D_MODEL), lambda b, s: (b, 0, 0)),
            pl.BlockSpec((1, D_MODEL), lambda b, s: (0, 0)),
            pl.BlockSpec((D_MODEL, SSM_WIDTH), lambda b, s: (0, ssm_col_block)),
        ],
        out_specs=pl.BlockSpec((1, CHUNK, SSM_WIDTH, chunks_per_step),
                               lambda b, s: (b, 0, 0, s)),
        out_shape=jax.ShapeDtypeStruct((bsz, CHUNK, SSM_WIDTH, seq // CHUNK), BF16),
        scratch_shapes=[pltpu.VMEM((2, ts, D_MODEL), F32),
                        pltpu.SemaphoreType.DMA((2,)),
                        pltpu.VMEM((SSM_WIDTH, D_MODEL), BF16)],
        compiler_params=pltpu.CompilerParams(
            dimension_semantics=("arbitrary", "arbitrary"),
            vmem_limit_bytes=VMEM_LIMIT),
        name="ssm_in",
    )(x_chunked, mod, g, w_in)


def _ssm_kernel(u_ref, kt_ref, bst_ref, cst_ref, lre_ref, lim_ref, *rest):
    n_w = (len(rest) - 7) // 2
    w_in_refs, o_ref, w_out_refs = rest[:n_w], rest[n_w], rest[n_w + 1:2 * n_w + 1]
    sre, sim, xre, xim, ys, yn = rest[2 * n_w + 1:]
    n_chunks = u_ref.shape[-1]
    pw = PAIR_ST

    _cast_rows(w_in_refs, w_out_refs)

    def load_pair(j):
        r = u_ref[0, :, j * PAIR_CH:(j + 1) * PAIR_CH, :]
        return r.reshape(PAIR_ROWS, n_chunks)

    for j in range(PAIRS_PER_STEP):
        s = lax.dot_general(load_pair(j), bst_ref[j], (((0,), (0,)), ((), ())),
                            preferred_element_type=F32)
        sre[:, j * pw:(j + 1) * pw] = s[:, :pw]
        sim[:, j * pw:(j + 1) * pw] = s[:, pw:]

    lre = lre_ref[0]
    lim = lim_ref[0]

    def body(c, carry):
        xr, xi = carry
        xre[pl.ds(c, 1), :] = xr
        xim[pl.ds(c, 1), :] = xi
        nr = lre * xr - lim * xi + sre[pl.ds(c, 1), :]
        ni = lre * xi + lim * xr + sim[pl.ds(c, 1), :]
        return nr, ni

    zero = jnp.zeros((1, PAIRS_PER_STEP * pw), F32)
    lax.fori_loop(0, n_chunks, body, (zero, zero), unroll=8)

    for j in range(PAIRS_PER_STEP):
        y_intra = jnp.dot(kt_ref[j], load_pair(j), preferred_element_type=F32)
        xp = jnp.concatenate([xre[:, j * pw:(j + 1) * pw],
                              xim[:, j * pw:(j + 1) * pw]], axis=1).astype(BF16)
        y_state = lax.dot_general(cst_ref[j], xp, (((1,), (1,)), ((), ())),
                                  preferred_element_type=F32)
        y = jax.nn.gelu(y_intra + y_state)
        ys[:, j * PAIR_CH:(j + 1) * PAIR_CH, :] = y.reshape(CHUNK, PAIR_CH, n_chunks)

    for k in range(yn.shape[0]):
        lanes = slice(k * LANES, (k + 1) * LANES)
        for t in range(CHUNK):
            yn[k, pl.ds(t, n_chunks, stride=CHUNK), :] = ys[t, lanes, :].T
        o_ref[0, 0, :, lanes] = yn[k].astype(o_ref.dtype)


def _ssm_call(u_t, kt, bst, cst, lam_re, lam_im, weights):
    bsz, _, _, n_chunks = u_t.shape
    seq = n_chunks * CHUNK
    pps = PAIRS_PER_STEP
    ch = pps * PAIR_CH
    state_w = pps * PAIR_ST
    steps = N_PAIRS // pps
    lam_re = lam_re.reshape(steps, 1, state_w)
    lam_im = lam_im.reshape(steps, 1, state_w)
    n_steps = steps * bsz
    w_specs = [pl.BlockSpec((w.shape[0] // n_steps, w.shape[1]),
                            lambda p, b: (p * bsz + b, 0)) for w in weights]
    outs = pl.pallas_call(
        _ssm_kernel,
        grid=(steps, bsz),
        in_specs=[
            pl.BlockSpec((1, CHUNK, ch, n_chunks), lambda p, b: (b, 0, p, 0)),
            pl.BlockSpec((pps, PAIR_ROWS, PAIR_ROWS), lambda p, b: (p, 0, 0)),
            pl.BlockSpec((pps, PAIR_ROWS, 2 * PAIR_ST), lambda p, b: (p, 0, 0)),
            pl.BlockSpec((pps, PAIR_ROWS, 2 * PAIR_ST), lambda p, b: (p, 0, 0)),
            pl.BlockSpec((1, 1, state_w), lambda p, b: (p, 0, 0)),
            pl.BlockSpec((1, 1, state_w), lambda p, b: (p, 0, 0)),
        ] + w_specs,
        out_specs=[pl.BlockSpec((1, 1, seq, ch), lambda p, b: (b, p, 0, 0))] + w_specs,
        out_shape=[jax.ShapeDtypeStruct((bsz, steps, seq, ch), BF16)] + [
            jax.ShapeDtypeStruct(w.shape, BF16) for w in weights],
        scratch_shapes=[pltpu.VMEM((n_chunks, state_w), F32)] * 4 + [
            pltpu.VMEM((CHUNK, ch, n_chunks), F32),
            pltpu.VMEM((ch // LANES, seq, LANES), F32)],
        compiler_params=pltpu.CompilerParams(
            dimension_semantics=("arbitrary", "arbitrary"),
            vmem_limit_bytes=VMEM_LIMIT),
        name="ssm",
    )(u_t, kt, bst, cst, lam_re, lam_im, *weights)
    return outs[0], outs[1:]


def _mixer_kernel(x_ref, y_ref, mod_ref, g_ref, wc_ref, wgc0_ref, wgc1_ref, wgs0_ref,
                  wgs1_ref, cw_ref, cb_ref, lng_ref, lnb_ref, proj_ref, glu_ref,
                  wout_ref, wf_ref, o_ref, wf_o_ref, ubuf, cbuf, sbuf):
    _cast_rows((wf_ref,), (wf_o_ref,))

    @pl.when(pl.program_id(1) == 0)
    def _():
        ubuf[:, 0:CONV_HALO, :] = jnp.zeros((CONV_WIDTH // LANES, CONV_HALO, LANES), F32)

    for r0 in range(0, x_ref.shape[1], MIX_ROWS):
        _mixer_rows(r0, x_ref, y_ref, mod_ref, g_ref, wc_ref, wgc0_ref, wgc1_ref,
                    wgs0_ref, wgs1_ref, cw_ref, cb_ref, lng_ref, lnb_ref, proj_ref,
                    glu_ref, wout_ref, o_ref, ubuf, cbuf, sbuf)


def _mixer_rows(r0, x_ref, y_ref, mod_ref, g_ref, wc_ref, wgc0_ref, wgc1_ref, wgs0_ref,
                wgs1_ref, cw_ref, cb_ref, lng_ref, lnb_ref, proj_ref, glu_ref,
                wout_ref, o_ref, ubuf, cbuf, sbuf):
    ts = MIX_ROWS
    rows = slice(r0, r0 + ts)
    halo = CONV_HALO
    half = D_MODEL // 2
    n_lt = CONV_WIDTH // LANES

    x = x_ref[0, rows, :]
    h = _rms_mod(x, g_ref[...], mod_ref[0, 1:2, :], mod_ref[0, 0:1, :]).astype(BF16)

    zc = jnp.dot(h, wc_ref[...], preferred_element_type=F32)
    u = zc[:, :CONV_WIDTH] * _sigmoid(zc[:, CONV_WIDTH:])
    for k in range(n_lt):
        ubuf[k, halo:halo + ts, :] = u[:, k * LANES:(k + 1) * LANES]

    y_in = jnp.concatenate([y_ref[0, k, rows, :] for k in range(y_ref.shape[1])], axis=1)

    def dot_h(w_ref):
        return jnp.dot(h, w_ref[...], preferred_element_type=F32)

    def dot_y(k):
        return jnp.dot(y_in, glu_ref[:, k * half:(k + 1) * half],
                       preferred_element_type=F32)

    gc0, gs0, za0, zb0, gc1, gs1, za1, zb1 = range(N_SIDE)

    def put(k, v):
        sbuf[k] = v

    dyn_zero = jnp.minimum(pl.program_id(1), 0)
    never = lax.broadcasted_iota(jnp.int32, (SUBLANES, LANES), 0) < dyn_zero

    def anchored(term, k):
        fence = sbuf[dyn_zero, k * SUBLANES:(k + 1) * SUBLANES, 0:LANES]
        head = jnp.where(never, fence, term[0:SUBLANES, :])
        return jnp.concatenate([head, term[SUBLANES:, :]], axis=0)

    side_jobs = [
        lambda: put(gc0, dot_h(wgc0_ref)), lambda: put(gs0, dot_h(wgs0_ref)),
        lambda: put(za0, dot_y(0)), lambda: put(zb0, dot_y(2)),
        lambda: put(gc1, dot_h(wgc1_ref)), lambda: put(gs1, dot_h(wgs1_ref)),
        lambda: put(za1, dot_y(1)), lambda: put(zb1, dot_y(3)),
    ]
    first = halo - (CONV_KERNEL - 1)
    n_rt = ts // CONV_ROWS
    for i in range(n_rt):
        for k in range(n_lt):
            lanes = slice(k * LANES, (k + 1) * LANES)
            acc = None
            for j in range(CONV_KERNEL):
                row0 = first + j + i * CONV_ROWS
                term = cw_ref[j:j + 1, lanes] * ubuf[k, row0:row0 + CONV_ROWS, :]
                if j == 0 and i > 0:
                    term = anchored(term, k)
                acc = term if acc is None else acc + term
            cbuf[i * CONV_ROWS:(i + 1) * CONV_ROWS, lanes] = acc + cb_ref[:, lanes]
        if (i + 1) * len(side_jobs) % n_rt == 0:
            side_jobs[(i + 1) * len(side_jobs) // n_rt - 1]()
    for k in range(n_lt):
        ubuf[k, 0:halo, :] = ubuf[k, ts:ts + halo, :]
    acc = cbuf[...]
    mu = jnp.mean(acc, axis=-1, keepdims=True)
    cen = acc - mu
    var = jnp.mean(cen * cen, axis=-1, keepdims=True)
    yn = cen * lax.rsqrt(var + EPS) * lng_ref[...] + lnb_ref[...]
    yn = yn * _sigmoid(yn)
    y_conv = jnp.dot(yn.astype(BF16), proj_ref[...], preferred_element_type=F32)

    def merge(conv_half, g_conv, g_ssm, z_a, z_b):
        y_ssm = sbuf[z_a] * _sigmoid(sbuf[z_b])
        return _sigmoid(sbuf[g_conv]) * conv_half + _sigmoid(sbuf[g_ssm]) * y_ssm

    merged = jnp.concatenate(
        [merge(y_conv[:, :half], gc0, gs0, za0, zb0),
         merge(y_conv[:, half:], gc1, gs1, za1, zb1)], axis=1)
    upd = jnp.dot(merged.astype(BF16), wout_ref[...], preferred_element_type=F32)
    o_ref[0, rows, :] = x + mod_ref[0, 2:3, :] * upd


def _const_spec(shape, index=None):
    index = (0,) * len(shape) if index is None else index
    return pl.BlockSpec(shape, lambda b, s: index, pipeline_mode=pl.Buffered(1))


def _mixer_call(x, y, mod, g, w_in_b, cw, cb, lng, lnb, proj, glu, wout, w_ffn_in, ts):
    bsz, seq, _ = x.shape
    n_s = seq // ts
    gate0 = (2 * CONV_WIDTH + SSM_WIDTH) // GATE_BLOCK
    gate_spec = lambda i: _const_spec((D_MODEL, GATE_BLOCK), (0, gate0 + i))
    wf_rows, wf_cols = w_ffn_in.shape
    wf_spec = pl.BlockSpec((wf_rows // (bsz * n_s), wf_cols), lambda b, s: (b * n_s + s, 0))
    return pl.pallas_call(
        _mixer_kernel,
        grid=(bsz, seq // ts),
        in_specs=[
            pl.BlockSpec((1, ts, D_MODEL), lambda b, s: (b, s, 0)),
            pl.BlockSpec((1, y.shape[1], ts, y.shape[3]), lambda b, s: (b, 0, s, 0)),
            pl.BlockSpec((1, 6, D_MODEL), lambda b, s: (b, 0, 0)),
            _const_spec((1, D_MODEL)),
            _const_spec((D_MODEL, 2 * CONV_WIDTH)),
            gate_spec(0), gate_spec(1), gate_spec(2), gate_spec(3),
            _const_spec((CONV_KERNEL, CONV_WIDTH)),
            _const_spec((1, CONV_WIDTH)),
            _const_spec((1, CONV_WIDTH)),
            _const_spec((1, CONV_WIDTH)),
            _const_spec((CONV_WIDTH, D_MODEL)),
            _const_spec((SSM_WIDTH, 2 * D_MODEL)),
            _const_spec((D_MODEL, D_MODEL)),
            wf_spec,
        ],
        out_specs=[pl.BlockSpec((1, ts, D_MODEL), lambda b, s: (b, s, 0)), wf_spec],
        out_shape=[jax.ShapeDtypeStruct(x.shape, F32),
                   jax.ShapeDtypeStruct(w_ffn_in.shape, BF16)],
        scratch_shapes=[
            pltpu.VMEM((CONV_WIDTH // LANES, MIX_ROWS + CONV_HALO, LANES), F32),
            pltpu.VMEM((MIX_ROWS, CONV_WIDTH), F32),
            pltpu.VMEM((N_SIDE, MIX_ROWS, D_MODEL // 2), F32)],
        compiler_params=pltpu.CompilerParams(
            dimension_semantics=("arbitrary", "arbitrary"),
            vmem_limit_bytes=VMEM_LIMIT),
        name="mixer",
    )(x, y, mod, g, w_in_b, w_in_b, w_in_b, w_in_b, w_in_b, cw, cb, lng, lnb, proj, glu,
      wout, w_ffn_in)


def _ffn_kernel(x_ref, mod_ref, g_ref, wg_ref, wu_ref, wo_ref, fg_ref, o_ref, act_s):
    x = x_ref[0]
    h = _rms_mod(x, g_ref[...], mod_ref[0, 4:5, :], mod_ref[0, 3:4, :]).astype(BF16)
    for c0 in range(0, FFN_HIDDEN, FFN_CHUNK):
        cols = slice(c0, c0 + FFN_CHUNK)
        fg = jnp.dot(h, wg_ref[:, cols], preferred_element_type=F32)
        fu = jnp.dot(h, wu_ref[:, cols], preferred_element_type=F32)
        act_s[:, cols] = (fg * _sigmoid(fg) * fu).astype(BF16)
    upd = jnp.dot(act_s[...], wo_ref[...], preferred_element_type=F32)
    x2 = x + mod_ref[0, 5:6, :] * upd
    ms = jnp.mean(x2 * x2, axis=-1, keepdims=True)
    o_ref[0] = x2 * lax.rsqrt(ms + EPS) * fg_ref[...]


def _ffn_call(x, mod, g, w_ffn_in_b, wo, final_g, ts):
    bsz, seq, _ = x.shape
    return pl.pallas_call(
        _ffn_kernel,
        grid=(bsz, seq // ts),
        in_specs=[
            pl.BlockSpec((1, ts, D_MODEL), lambda b, s: (b, s, 0)),
            pl.BlockSpec((1, 6, D_MODEL), lambda b, s: (b, 0, 0)),
            _const_spec((1, D_MODEL)),
            _const_spec((D_MODEL, FFN_HIDDEN), (0, 0)),
            _const_spec((D_MODEL, FFN_HIDDEN), (0, 1)),
            _const_spec((FFN_HIDDEN, D_MODEL)),
            _const_spec((1, D_MODEL)),
        ],
        out_specs=pl.BlockSpec((1, ts, D_MODEL), lambda b, s: (b, s, 0)),
        out_shape=jax.ShapeDtypeStruct(x.shape, F32),
        scratch_shapes=[pltpu.VMEM((ts, FFN_HIDDEN), BF16)],
        compiler_params=pltpu.CompilerParams(
            dimension_semantics=("arbitrary", "arbitrary"),
            vmem_limit_bytes=VMEM_LIMIT),
        name="ffn",
    )(x, mod, g, w_ffn_in_b, w_ffn_in_b, wo, final_g)


def kernel(x, c, w_ada, b_ada, norm1_g, w_in, conv_w, conv_b, conv_ln_g, conv_ln_b,
           conv_proj, ssm_a_re, ssm_a_im, ssm_b_re, ssm_b_im, ssm_c_re, ssm_c_im,
           ssm_d, ssm_log_dt, ssm_glu, w_out, norm2_g, w_ffn_in, w_ffn_out, final_g):
    bsz, seq, _ = x.shape
    assert w_ada.shape[0] == 1, "single layer only"

    mod, *ops = _prep_call(c, w_ada[0], b_ada[0], ssm_a_re[0], ssm_a_im[0], ssm_b_re[0],
                           ssm_b_im[0], ssm_c_re[0], ssm_c_im[0], ssm_d[0],
                           ssm_log_dt[0])
    mod = mod.reshape(bsz, -1, D_MODEL)

    g1 = norm1_g[0].reshape(1, -1)

    u_t = _ssm_in_call(x, mod, g1, w_in[0], 128)
    y, (w_in_b, proj_b, glu_b, w_out_b, w_ffn_out_b) = _ssm_call(
        u_t, *ops, (w_in[0], conv_proj[0], ssm_glu[0], w_out[0], w_ffn_out[0]))

    x1, w_ffn_in_b = _mixer_call(
        x, y, mod, g1, w_in_b, conv_w[0], conv_b[0].reshape(1, -1),
        conv_ln_g[0].reshape(1, -1), conv_ln_b[0].reshape(1, -1),
        proj_b, glu_b, w_out_b, w_ffn_in[0], 1024)

    return _ffn_call(x1, mod, norm2_g[0].reshape(1, -1), w_ffn_in_b, w_ffn_out_b,
                     final_g.reshape(1, -1), 1024)
```

```python
import jax
import jax.numpy as jnp
from jax import lax
from jax.experimental import pallas as pl
from jax.experimental.pallas import tpu as pltpu

F32 = jnp.float32
BF16 = jnp.bfloat16

D_MODEL = 1024
CONV_WIDTH = 512
CONV_KERNEL = 31
SSM_WIDTH = 512
SSM_GROUP = 16
SSM_GROUPS = 32
SSM_STATE = 64
FFN_HIDDEN = 2816
EPS = 1e-6

CHUNK = 16
PAIR_CH = 2 * SSM_GROUP
PAIR_ST = 2 * SSM_STATE
PAIR_ROWS = CHUNK * PAIR_CH
N_PAIRS = SSM_GROUPS // 2
PAIRS_PER_STEP = 8
PREP_PAIRS = 2
CONV_HALO = 32
GATE_BLOCK = 512
LANES = 128
SUBLANES = 8
CONV_ROWS = 64
FFN_CHUNK = 256
N_SIDE = 8
MIX_ROWS = 512

V7X_VMEM_BYTES = 64 * 1024 * 1024
VMEM_LIMIT = V7X_VMEM_BYTES * 7 // 8


def _rms_mod(x, g, scale, shift):
    ms = jnp.mean(x * x, axis=-1, keepdims=True)
    return (x * lax.rsqrt(ms + EPS)) * (g * (1.0 + scale)) + shift


def _sigmoid(v):
    return jax.nn.sigmoid(v)


def _split_bf16(v):
    hi = v.astype(BF16)
    return hi, (v - hi.astype(F32)).astype(BF16)


def _cast_rows(src_refs, dst_refs):
    for src, dst in zip(src_refs, dst_refs):
        dst[...] = src[...].astype(BF16)


def _prep_kernel(c_ref, wada_ref, bada_ref, ar_ref, ai_ref, ldt_ref, cre_ref, cim_ref,
                 bre_ref, bim_ref, dl_ref, mod_ref, kt_ref, bst_ref, cst_ref, lre_ref,
                 lim_ref, cl_s):
    c = c_ref[...]
    ca = (c * _sigmoid(c)).astype(BF16)
    mod_ref[...] = jnp.dot(ca, wada_ref[...].astype(BF16),
                           preferred_element_type=F32) + bada_ref[...]
    for j in range(ar_ref.shape[0]):
        _pair_operators(j, ar_ref, ai_ref, ldt_ref, cre_ref, cim_ref, bre_ref, bim_ref,
                        dl_ref, kt_ref, bst_ref, cst_ref, lre_ref, lim_ref, cl_s)


def _pair_operators(pair, ar_ref, ai_ref, ldt_ref, cre_ref, cim_ref, bre_ref, bim_ref,
                    dl_ref, kt_ref, bst_ref, cst_ref, lre_ref, lim_ref, cl_s):
    ar, ai = ar_ref[pair], ai_ref[pair]
    dt = jnp.exp(ldt_ref[pair])
    dar, dai = dt * ar, dt * ai
    mag, cos1, sin1 = jnp.exp(dar), jnp.cos(dai), jnp.sin(dai)
    n_re, n_im = mag * cos1 - 1.0, mag * sin1
    den = ar * ar + ai * ai
    q_re = (n_re * ar + n_im * ai) / den
    q_im = (n_im * ar - n_re * ai) / den

    row = lax.broadcasted_iota(jnp.int32, (PAIR_CH, PAIR_ST), 0)
    lane = lax.broadcasted_iota(jnp.int32, (PAIR_CH, PAIR_ST), 1)
    same_group = (row // SSM_GROUP) == (lane // SSM_STATE)

    def block_diag(ref):
        v = ref[pair]
        return jnp.where(same_group, jnp.concatenate([v, v], axis=1), 0.0)

    c_re, c_im = block_diag(cre_ref), block_diag(cim_ref)
    b_re, b_im = block_diag(bre_ref), block_diag(bim_ref)
    bb_re = q_re * b_re - q_im * b_im
    bb_im = q_re * b_im + q_im * b_re

    powers = {0: (jnp.ones_like(dar), jnp.zeros_like(dar)), 1: (mag * cos1, mag * sin1)}

    def power(k):
        if k not in powers:
            (p_re, p_im), (s_re, s_im) = power(k - 1), powers[1]
            powers[k] = (p_re * s_re - p_im * s_im, p_re * s_im + p_im * s_re)
        return powers[k]

    def cmul_cat(x_re, x_im, p, conj_out):
        p_re, p_im = p
        o_re = x_re * p_re - x_im * p_im
        o_im = x_re * p_im + x_im * p_re
        return jnp.concatenate([o_re, -o_im if conj_out else o_im], axis=1)

    for t in range(CHUNK):
        rs = slice(t * PAIR_CH, (t + 1) * PAIR_CH)
        cl_s[rs, :] = cmul_cat(c_re, c_im, power(t), True)
        cst_ref[pair, rs, :] =cmul_cat(c_re, c_im, power(t + 1), True).astype(BF16)
        bst_ref[pair, rs, :] = cmul_cat(bb_re, bb_im, power(CHUNK - 1 - t),
                                        False).astype(BF16)
    lre_ref[pair], lim_ref[pair] = power(CHUNK)

    cl_hi, cl_lo = _split_bf16(cl_s[...])
    bb = jnp.concatenate([bb_re, bb_im], axis=1)
    bb = jnp.concatenate([bb, jnp.zeros((LANES - PAIR_CH, 2 * PAIR_ST), F32)], axis=0)
    bb_hi, bb_lo = _split_bf16(bb)

    def dot_nt(a, b):
        return lax.dot_general(a, b, (((1,), (1,)), ((), ())), preferred_element_type=F32)

    kcol = dot_nt(cl_hi, bb_hi) + (dot_nt(cl_hi, bb_lo) + dot_nt(cl_lo, bb_hi))

    per_tile = LANES // PAIR_CH
    ri = lax.broadcasted_iota(jnp.int32, (PAIR_ROWS, LANES), 0)
    ci = lax.broadcasted_iota(jnp.int32, (PAIR_ROWS, LANES), 1)
    for tile in range(PAIR_ROWS // LANES):
        acc = jnp.where(ri == ci + tile * LANES,
                        dl_ref[pair, :, tile * LANES:(tile + 1) * LANES], 0.0)
        for q in range(per_tile):
            tau = tile * per_tile + q
            piece = kcol if q == 0 else pltpu.roll(kcol, q * PAIR_CH, axis=1)
            if tau:
                piece = jnp.concatenate(
                    [jnp.zeros((tau * PAIR_CH, LANES), F32),
                     piece[:PAIR_ROWS - tau * PAIR_CH, :]], axis=0)
            acc = acc + piece
        kt_ref[pair, :, tile * LANES:(tile + 1) * LANES] = acc.astype(BF16)


def _prep_call(c, w_ada, b_ada, a_re, a_im, b_re, b_im, c_re, c_im, d, log_dt):
    np_, pc, ps, st = N_PAIRS, PAIR_CH, PAIR_ST, SSM_STATE
    bsz = c.shape[0]
    mod_cols = w_ada.shape[1]
    steps = np_ // PREP_PAIRS
    mod_blk = mod_cols // steps
    row = lambda a: a.reshape(np_, 1, ps)
    ldt = jnp.broadcast_to(log_dt[:, None], (SSM_GROUPS, st))
    d_lane = jnp.tile(d.reshape(np_, 1, pc), (1, 1, CHUNK))
    bt = lambda a: jnp.transpose(a, (0, 2, 1)).reshape(np_, pc, st)
    spec = lambda *shape: pl.BlockSpec((PREP_PAIRS,) + shape, lambda p: (p, 0, 0))
    col_spec = lambda rows: pl.BlockSpec((rows, mod_blk), lambda p: (0, p))
    return pl.pallas_call(
        _prep_kernel,
        grid=(steps,),
        in_specs=[pl.BlockSpec((bsz, D_MODEL), lambda p: (0, 0)),
                  col_spec(D_MODEL), col_spec(1),
                  spec(1, ps), spec(1, ps), spec(1, ps), spec(pc, st), spec(pc, st),
                  spec(pc, st), spec(pc, st), spec(1, PAIR_ROWS)],
        out_specs=[col_spec(bsz),
                   spec(PAIR_ROWS, PAIR_ROWS), spec(PAIR_ROWS, 2 * ps),
                   spec(PAIR_ROWS, 2 * ps), spec(1, ps), spec(1, ps)],
        out_shape=[jax.ShapeDtypeStruct((bsz, mod_cols), F32),
                   jax.ShapeDtypeStruct((np_, PAIR_ROWS, PAIR_ROWS), BF16),
                   jax.ShapeDtypeStruct((np_, PAIR_ROWS, 2 * ps), BF16),
                   jax.ShapeDtypeStruct((np_, PAIR_ROWS, 2 * ps), BF16),
                   jax.ShapeDtypeStruct((np_, 1, ps), F32),
                   jax.ShapeDtypeStruct((np_, 1, ps), F32)],
        scratch_shapes=[pltpu.VMEM((PAIR_ROWS, 2 * ps), F32)],
        name="prep",
    )(c, w_ada, b_ada.reshape(1, -1), row(a_re), row(a_im), row(ldt),
      c_re.reshape(np_, pc, st), c_im.reshape(np_, pc, st), bt(b_re), bt(b_im), d_lane)


def _ssm_in_kernel(x_hbm, mod_ref, g_ref, w_ref, o_ref, xbuf, sems, wt_s):
    n_chunks = o_ref.shape[-1]
    n_s = pl.num_programs(1)
    flat = pl.program_id(0) * n_s + pl.program_id(1)
    slot = flat % 2

    def gather(block, buf_slot):
        bb, c0 = block // n_s, (block % n_s) * n_chunks
        return [pltpu.make_async_copy(
            x_hbm.at[bb, pl.ds(c0, n_chunks), tau, :],
            xbuf.at[buf_slot, pl.ds(tau * n_chunks, n_chunks), :],
            sems.at[buf_slot]) for tau in range(CHUNK)]

    @pl.when(flat == 0)
    def _():
        for cp in gather(flat, slot):
            cp.start()
        wt_s[...] = w_ref[...].T.astype(BF16)

    @pl.when(flat + 1 < pl.num_programs(0) * n_s)
    def _():
        for cp in gather(flat + 1, 1 - slot):
            cp.start()

    for cp in gather(flat, slot):
        cp.wait()

    h = _rms_mod(xbuf[slot], g_ref[...], mod_ref[0, 1:2, :], mod_ref[0, 0:1, :])
    u_t = lax.dot_general(wt_s[...], h.astype(BF16), (((1,), (1,)), ((), ())),
                          preferred_element_type=F32)
    for tau in range(CHUNK):
        o_ref[0, tau] = u_t[:, tau * n_chunks:(tau + 1) * n_chunks].astype(o_ref.dtype)


def _ssm_in_call(x, mod, g, w_in, chunks_per_step):
    bsz, seq, _ = x.shape
    ts = chunks_per_step * CHUNK
    ssm_col_block = 2 * CONV_WIDTH // SSM_WIDTH
    x_chunked = x.reshape(bsz, seq // CHUNK, CHUNK, D_MODEL)
    return pl.pallas_call(
        _ssm_in_kernel,
        grid=(bsz, seq // ts),
        in_specs=[
            pl.BlockSpec(memory_space=pl.ANY),
            pl.BlockSpec((1, 6, D_MODEL), lambda b, s: (b, 0, 0)),
            pl.BlockSpec((1, D_MODEL), lambda b, s: (0, 0)),
            pl.BlockSpec((D_MODEL, SSM_WIDTH), lambda b, s: (0, ssm_col_block)),
        ],
        out_specs=pl.BlockSpec((1, CHUNK, SSM_WIDTH, chunks_per_step),
                               lambda b, s: (b, 0, 0, s)),
        out_shape=jax.ShapeDtypeStruct((bsz, CHUNK, SSM_WIDTH, seq // CHUNK), BF16),
        scratch_shapes=[pltpu.VMEM((2, ts, D_MODEL), F32),
                        pltpu.SemaphoreType.DMA((2,)),
                        pltpu.VMEM((SSM_WIDTH, D_MODEL), BF16)],
        compiler_params=pltpu.CompilerParams(
            dimension_semantics=("arbitrary", "arbitrary"),
            vmem_limit_bytes=VMEM_LIMIT),
        name="ssm_in",
    )(x_chunked, mod, g, w_in)


def _ssm_kernel(u_ref, kt_ref, bst_ref, cst_ref, lre_ref, lim_ref, *rest):
    n_w = (len(rest) - 7) // 2
    w_in_refs, o_ref, w_out_refs = rest[:n_w], rest[n_w], rest[n_w + 1:2 * n_w + 1]
    sre, sim, xre, xim, ys, yn = rest[2 * n_w + 1:]
    n_chunks = u_ref.shape[-1]
    pw = PAIR_ST

    _cast_rows(w_in_refs, w_out_refs)

    def load_pair(j):
        r = u_ref[0, :, j * PAIR_CH:(j + 1) * PAIR_CH, :]
        return r.reshape(PAIR_ROWS, n_chunks)

    for j in range(PAIRS_PER_STEP):
        s = lax.dot_general(load_pair(j), bst_ref[j], (((0,), (0,)), ((), ())),
                            preferred_element_type=F32)
        sre[:, j * pw:(j + 1) * pw] = s[:, :pw]
        sim[:, j * pw:(j + 1) * pw] = s[:, pw:]

    lre = lre_ref[0]
    lim = lim_ref[0]

    def body(c, carry):
        xr, xi = carry
        xre[pl.ds(c, 1), :] = xr
        xim[pl.ds(c, 1), :] = xi
        nr = lre * xr - lim * xi + sre[pl.ds(c, 1), :]
        ni = lre * xi + lim * xr + sim[pl.ds(c, 1), :]
        return nr, ni

    zero = jnp.zeros((1, PAIRS_PER_STEP * pw), F32)
    lax.fori_loop(0, n_chunks, body, (zero, zero), unroll=8)

    for j in range(PAIRS_PER_STEP):
        y_intra = jnp.dot(kt_ref[j], load_pair(j), preferred_element_type=F32)
        xp = jnp.concatenate([xre[:, j * pw:(j + 1) * pw],
                              xim[:, j * pw:(j + 1) * pw]], axis=1).astype(BF16)
        y_state = lax.dot_general(cst_ref[j], xp, (((1,), (1,)), ((), ())),
                                  preferred_element_type=F32)
        y = jax.nn.gelu(y_intra + y_state)
        ys[:, j * PAIR_CH:(j + 1) * PAIR_CH, :] = y.reshape(CHUNK, PAIR_CH, n_chunks)

    for k in range(yn.shape[0]):
        lanes = slice(k * LANES, (k + 1) * LANES)
        for t in range(CHUNK):
            yn[k, pl.ds(t, n_chunks, stride=CHUNK), :] = ys[t, lanes, :].T
        o_ref[0, 0, :, lanes] = yn[k].astype(o_ref.dtype)


def _ssm_call(u_t, kt, bst, cst, lam_re, lam_im, weights):
    bsz, _, _, n_chunks = u_t.shape
    seq = n_chunks * CHUNK
    pps = PAIRS_PER_STEP
    ch = pps * PAIR_CH
    state_w = pps * PAIR_ST
    steps = N_PAIRS // pps
    lam_re = lam_re.reshape(steps, 1, state_w)
    lam_im = lam_im.reshape(steps, 1, state_w)
    n_steps = steps * bsz
    w_specs = [pl.BlockSpec((w.shape[0] // n_steps, w.shape[1]),
                            lambda p, b: (p * bsz + b, 0)) for w in weights]
    outs = pl.pallas_call(
        _ssm_kernel,
        grid=(steps, bsz),
        in_specs=[
            pl.BlockSpec((1, CHUNK, ch, n_chunks), lambda p, b: (b, 0, p, 0)),
            pl.BlockSpec((pps, PAIR_ROWS, PAIR_ROWS), lambda p, b: (p, 0, 0)),
            pl.BlockSpec((pps, PAIR_ROWS, 2 * PAIR_ST), lambda p, b: (p, 0, 0)),
            pl.BlockSpec((pps, PAIR_ROWS, 2 * PAIR_ST), lambda p, b: (p, 0, 0)),
            pl.BlockSpec((1, 1, state_w), lambda p, b: (p, 0, 0)),
            pl.BlockSpec((1, 1, state_w), lambda p, b: (p, 0, 0)),
        ] + w_specs,
        out_specs=[pl.BlockSpec((1, 1, seq, ch), lambda p, b: (b, p, 0, 0))] + w_specs,
        out_shape=[jax.ShapeDtypeStruct((bsz, steps, seq, ch), BF16)] + [
            jax.ShapeDtypeStruct(w.shape, BF16) for w in weights],
        scratch_shapes=[pltpu.VMEM((n_chunks, state_w), F32)] * 4 + [
            pltpu.VMEM((CHUNK, ch, n_chunks), F32),
            pltpu.VMEM((ch // LANES, seq, LANES), F32)],
        compiler_params=pltpu.CompilerParams(
            dimension_semantics=("arbitrary", "arbitrary"),
            vmem_limit_bytes=VMEM_LIMIT),
        name="ssm",
    )(u_t, kt, bst, cst, lam_re, lam_im, *weights)
    return outs[0], outs[1:]


def _mixer_kernel(x_ref, y_ref, mod_ref, g_ref, wc_ref, wgc0_ref, wgc1_ref, wgs0_ref,
                  wgs1_ref, cw_ref, cb_ref, lng_ref, lnb_ref, proj_ref, glu_ref,
                  wout_ref, wf_ref, o_ref, wf_o_ref, ubuf, cbuf, sbuf):
    _cast_rows((wf_ref,), (wf_o_ref,))

    @pl.when(pl.program_id(1) == 0)
    def _():
        ubuf[:, 0:CONV_HALO, :] = jnp.zeros((CONV_WIDTH // LANES, CONV_HALO, LANES), F32)

    for r0 in range(0, x_ref.shape[1], MIX_ROWS):
        _mixer_rows(r0, x_ref, y_ref, mod_ref, g_ref, wc_ref, wgc0_ref, wgc1_ref,
                    wgs0_ref, wgs1_ref, cw_ref, cb_ref, lng_ref, lnb_ref, proj_ref,
                    glu_ref, wout_ref, o_ref, ubuf, cbuf, sbuf)


def _mixer_rows(r0, x_ref, y_ref, mod_ref, g_ref, wc_ref, wgc0_ref, wgc1_ref, wgs0_ref,
                wgs1_ref, cw_ref, cb_ref, lng_ref, lnb_ref, proj_ref, glu_ref,
                wout_ref, o_ref, ubuf, cbuf, sbuf):
    ts = MIX_ROWS
    rows = slice(r0, r0 + ts)
    halo = CONV_HALO
    half = D_MODEL // 2
    n_lt = CONV_WIDTH // LANES

    x = x_ref[0, rows, :]
    h = _rms_mod(x, g_ref[...], mod_ref[0, 1:2, :], mod_ref[0, 0:1, :]).astype(BF16)

    zc = jnp.dot(h, wc_ref[...], preferred_element_type=F32)
    u = zc[:, :CONV_WIDTH] * _sigmoid(zc[:, CONV_WIDTH:])
    for k in range(n_lt):
        ubuf[k, halo:halo + ts, :] = u[:, k * LANES:(k + 1) * LANES]

    y_in = jnp.concatenate([y_ref[0, k, rows, :] for k in range(y_ref.shape[1])], axis=1)

    def dot_h(w_ref):
        return jnp.dot(h, w_ref[...], preferred_element_type=F32)

    def dot_y(k):
        return jnp.dot(y_in, glu_ref[:, k * half:(k + 1) * half],
                       preferred_element_type=F32)

    gc0, gs0, za0, zb0, gc1, gs1, za1, zb1 = range(N_SIDE)

    def put(k, v):
        sbuf[k] = v

    dyn_zero = jnp.minimum(pl.program_id(1), 0)
    never = lax.broadcasted_iota(jnp.int32, (SUBLANES, LANES), 0) < dyn_zero

    def anchored(term, k):
        fence = sbuf[dyn_zero, k * SUBLANES:(k + 1) * SUBLANES, 0:LANES]
        head = jnp.where(never, fence, term[0:SUBLANES, :])
        return jnp.concatenate([head, term[SUBLANES:, :]], axis=0)

    side_jobs = [
        lambda: put(gc0, dot_h(wgc0_ref)), lambda: put(gs0, dot_h(wgs0_ref)),
        lambda: put(za0, dot_y(0)), lambda: put(zb0, dot_y(2)),
        lambda: put(gc1, dot_h(wgc1_ref)), lambda: put(gs1, dot_h(wgs1_ref)),
        lambda: put(za1, dot_y(1)), lambda: put(zb1, dot_y(3)),
    ]
    first = halo - (CONV_KERNEL - 1)
    n_rt = ts // CONV_ROWS
    for i in range(n_rt):
        for k in range(n_lt):
            lanes = slice(k * LANES, (k + 1) * LANES)
            acc = None
            for j in range(CONV_KERNEL):
                row0 = first + j + i * CONV_ROWS
                term = cw_ref[j:j + 1, lanes] * ubuf[k, row0:row0 + CONV_ROWS, :]
                if j == 0 and i > 0:
                    term = anchored(term, k)
                acc = term if acc is None else acc + term
            cbuf[i * CONV_ROWS:(i + 1) * CONV_ROWS, lanes] = acc + cb_ref[:, lanes]
        if (i + 1) * len(side_jobs) % n_rt == 0:
            side_jobs[(i + 1) * len(side_jobs) // n_rt - 1]()
    for k in range(n_lt):
        ubuf[k, 0:halo, :] = ubuf[k, ts:ts + halo, :]
    acc = cbuf[...]
    mu = jnp.mean(acc, axis=-1, keepdims=True)
    cen = acc - mu
    var = jnp.mean(cen * cen, axis=-1, keepdims=True)
    yn = cen * lax.rsqrt(var + EPS) * lng_ref[...] + lnb_ref[...]
    yn = yn * _sigmoid(yn)
    y_conv = jnp.dot(yn.astype(BF16), proj_ref[...], preferred_element_type=F32)

    def merge(conv_half, g_conv, g_ssm, z_a, z_b):
        y_ssm = sbuf[z_a] * _sigmoid(sbuf[z_b])
        return _sigmoid(sbuf[g_conv]) * conv_half + _sigmoid(sbuf[g_ssm]) * y_ssm

    merged = jnp.concatenate(
        [merge(y_conv[:, :half], gc0, gs0, za0, zb0),
         merge(y_conv[:, half:], gc1, gs1, za1, zb1)], axis=1)
    upd = jnp.dot(merged.astype(BF16), wout_ref[...], preferred_element_type=F32)
    o_ref[0, rows, :] = x + mod_ref[0, 2:3, :] * upd


def _const_spec(shape, index=None):
    index = (0,) * len(shape) if index is None else index
    return pl.BlockSpec(shape, lambda b, s: index, pipeline_mode=pl.Buffered(1))


def _mixer_call(x, y, mod, g, w_in_b, cw, cb, lng, lnb, proj, glu, wout, w_ffn_in, ts):
    bsz, seq, _ = x.shape
    n_s = seq // ts
    gate0 = (2 * CONV_WIDTH + SSM_WIDTH) // GATE_BLOCK
    gate_spec = lambda i: _const_spec((D_MODEL, GATE_BLOCK), (0, gate0 + i))
    wf_rows, wf_cols = w_ffn_in.shape
    wf_spec = pl.BlockSpec((wf_rows // (bsz * n_s), wf_cols), lambda b, s: (b * n_s + s, 0))
    return pl.pallas_call(
        _mixer_kernel,
        grid=(bsz, seq // ts),
        in_specs=[
            pl.BlockSpec((1, ts, D_MODEL), lambda b, s: (b, s, 0)),
            pl.BlockSpec((1, y.shape[1], ts, y.shape[3]), lambda b, s: (b, 0, s, 0)),
            pl.BlockSpec((1, 6, D_MODEL), lambda b, s: (b, 0, 0)),
            _const_spec((1, D_MODEL)),
            _const_spec((D_MODEL, 2 * CONV_WIDTH)),
            gate_spec(0), gate_spec(1), gate_spec(2), gate_spec(3),
            _const_spec((CONV_KERNEL, CONV_WIDTH)),
            _const_spec((1, CONV_WIDTH)),
            _const_spec((1, CONV_WIDTH)),
            _const_spec((1, CONV_WIDTH)),
            _const_spec((CONV_WIDTH, D_MODEL)),
            _const_spec((SSM_WIDTH, 2 * D_MODEL)),
            _const_spec((D_MODEL, D_MODEL)),
            wf_spec,
        ],
        out_specs=[pl.BlockSpec((1, ts, D_MODEL), lambda b, s: (b, s, 0)), wf_spec],
        out_shape=[jax.ShapeDtypeStruct(x.shape, F32),
                   jax.ShapeDtypeStruct(w_ffn_in.shape, BF16)],
        scratch_shapes=[
            pltpu.VMEM((CONV_WIDTH // LANES, MIX_ROWS + CONV_HALO, LANES), F32),
            pltpu.VMEM((MIX_ROWS, CONV_WIDTH), F32),
            pltpu.VMEM((N_SIDE, MIX_ROWS, D_MODEL // 2), F32)],
        compiler_params=pltpu.CompilerParams(
            dimension_semantics=("arbitrary", "arbitrary"),
            vmem_limit_bytes=VMEM_LIMIT),
        name="mixer",
    )(x, y, mod, g, w_in_b, w_in_b, w_in_b, w_in_b, w_in_b, cw, cb, lng, lnb, proj, glu,
      wout, w_ffn_in)


def _ffn_kernel(x_ref, mod_ref, g_ref, wg_ref, wu_ref, wo_ref, fg_ref, o_ref, act_s):
    x = x_ref[0]
    h = _rms_mod(x, g_ref[...], mod_ref[0, 4:5, :], mod_ref[0, 3:4, :]).astype(BF16)
    for c0 in range(0, FFN_HIDDEN, FFN_CHUNK):
        cols = slice(c0, c0 + FFN_CHUNK)
        fg = jnp.dot(h, wg_ref[:, cols], preferred_element_type=F32)
        fu = jnp.dot(h, wu_ref[:, cols], preferred_element_type=F32)
        act_s[:, cols] = (fg * _sigmoid(fg) * fu).astype(BF16)
    upd = jnp.dot(act_s[...], wo_ref[...], preferred_element_type=F32)
    x2 = x + mod_ref[0, 5:6, :] * upd
    ms = jnp.mean(x2 * x2, axis=-1, keepdims=True)
    o_ref[0] = x2 * lax.rsqrt(ms + EPS) * fg_ref[...]


def _ffn_call(x, mod, g, w_ffn_in_b, wo, final_g, ts):
    bsz, seq, _ = x.shape
    return pl.pallas_call(
        _ffn_kernel,
        grid=(bsz, seq // ts),
        in_specs=[
            pl.BlockSpec((1, ts, D_MODEL), lambda b, s: (b, s, 0)),
            pl.BlockSpec((1, 6, D_MODEL), lambda b, s: (b, 0, 0)),
            _const_spec((1, D_MODEL)),
            _const_spec((D_MODEL, FFN_HIDDEN), (0, 0)),
            _const_spec((D_MODEL, FFN_HIDDEN), (0, 1)),
            _const_spec((FFN_HIDDEN, D_MODEL)),
            _const_spec((1, D_MODEL)),
        ],
        out_specs=pl.BlockSpec((1, ts, D_MODEL), lambda b, s: (b, s, 0)),
        out_shape=jax.ShapeDtypeStruct(x.shape, F32),
        scratch_shapes=[pltpu.VMEM((ts, FFN_HIDDEN), BF16)],
        compiler_params=pltpu.CompilerParams(
            dimension_semantics=("arbitrary", "arbitrary"),
            vmem_limit_bytes=VMEM_LIMIT),
        name="ffn",
    )(x, mod, g, w_ffn_in_b, w_ffn_in_b, wo, final_g)


def kernel(x, c, w_ada, b_ada, norm1_g, w_in, conv_w, conv_b, conv_ln_g, conv_ln_b,
           conv_proj, ssm_a_re, ssm_a_im, ssm_b_re, ssm_b_im, ssm_c_re, ssm_c_im,
           ssm_d, ssm_log_dt, ssm_glu, w_out, norm2_g, w_ffn_in, w_ffn_out, final_g):
    bsz, seq, _ = x.shape
    assert w_ada.shape[0] == 1, "single layer only"

    mod, *ops = _prep_call(c, w_ada[0], b_ada[0], ssm_a_re[0], ssm_a_im[0], ssm_b_re[0],
                           ssm_b_im[0], ssm_c_re[0], ssm_c_im[0], ssm_d[0],
                           ssm_log_dt[0])
    mod = mod.reshape(bsz, -1, D_MODEL)

    g1 = norm1_g[0].reshape(1, -1)

    u_t = _ssm_in_call(x, mod, g1, w_in[0], 128)
    y, (w_in_b, proj_b, glu_b, w_out_b, w_ffn_out_b) = _ssm_call(
        u_t, *ops, (w_in[0], conv_proj[0], ssm_glu[0], w_out[0], w_ffn_out[0]))

    x1, w_ffn_in_b = _mixer_call(
        x, y, mod, g1, w_in_b, conv_w[0], conv_b[0].reshape(1, -1),
        conv_ln_g[0].reshape(1, -1), conv_ln_b[0].reshape(1, -1),
        proj_b, glu_b, w_out_b, w_ffn_in[0], 1024)

    return _ffn_call(x1, mod, norm2_g[0].reshape(1, -1), w_ffn_in_b, w_ffn_out_b,
                     final_g.reshape(1, -1), 1024)
```

```python
import jax
import jax.numpy as jnp
from jax import lax
from jax.experimental import pallas as pl
from jax.experimental.pallas import tpu as pltpu

F32 = jnp.float32
BF16 = jnp.bfloat16

D_MODEL = 1024
CONV_WIDTH = 512
CONV_KERNEL = 31
SSM_WIDTH = 512
SSM_GROUP = 16
SSM_GROUPS = 32
SSM_STATE = 64
FFN_HIDDEN = 2816
EPS = 1e-6

CHUNK = 16
PAIR_CH = 2 * SSM_GROUP
PAIR_ST = 2 * SSM_STATE
PAIR_ROWS = CHUNK * PAIR_CH
N_PAIRS = SSM_GROUPS // 2
PAIRS_PER_STEP = 8
PREP_PAIRS = 2
GATHER_SPLIT = 4
CONV_HALO = 32
GATE_BLOCK = 512
LANES = 128
SUBLANES = 8
CONV_ROWS = 64
FFN_CHUNK = 256
N_SIDE = 8
MIX_ROWS = 512

V7X_VMEM_BYTES = 64 * 1024 * 1024
VMEM_LIMIT = V7X_VMEM_BYTES * 7 // 8


def _rms_mod(x, g, scale, shift):
    ms = jnp.mean(x * x, axis=-1, keepdims=True)
    return (x * lax.rsqrt(ms + EPS)) * (g * (1.0 + scale)) + shift


def _sigmoid(v):
    return jax.nn.sigmoid(v)


def _split_bf16(v):
    hi = v.astype(BF16)
    return hi, (v - hi.astype(F32)).astype(BF16)


def _cast_rows(src_refs, dst_refs):
    for src, dst in zip(src_refs, dst_refs):
        dst[...] = src[...].astype(BF16)


def _prep_kernel(c_ref, wada_ref, bada_ref, ar_ref, ai_ref, ldt_ref, cre_ref, cim_ref,
                 bre_ref, bim_ref, dl_ref, mod_ref, kt_ref, bst_ref, cst_ref, lre_ref,
                 lim_ref, cl_s):
    c = c_ref[...]
    ca = (c * _sigmoid(c)).astype(BF16)
    mod_ref[...] = jnp.dot(ca, wada_ref[...].astype(BF16),
                           preferred_element_type=F32) + bada_ref[...]
    for j in range(ar_ref.shape[0]):
        _pair_operators(j, ar_ref, ai_ref, ldt_ref, cre_ref, cim_ref, bre_ref, bim_ref,
                        dl_ref, kt_ref, bst_ref, cst_ref, lre_ref, lim_ref, cl_s)


def _pair_operators(pair, ar_ref, ai_ref, ldt_ref, cre_ref, cim_ref, bre_ref, bim_ref,
                    dl_ref, kt_ref, bst_ref, cst_ref, lre_ref, lim_ref, cl_s):
    ar, ai = ar_ref[pair], ai_ref[pair]
    dt = jnp.exp(ldt_ref[pair])
    dar, dai = dt * ar, dt * ai
    mag, cos1, sin1 = jnp.exp(dar), jnp.cos(dai), jnp.sin(dai)
    n_re, n_im = mag * cos1 - 1.0, mag * sin1
    den = ar * ar + ai * ai
    q_re = (n_re * ar + n_im * ai) / den
    q_im = (n_im * ar - n_re * ai) / den

    row = lax.broadcasted_iota(jnp.int32, (PAIR_CH, PAIR_ST), 0)
    lane = lax.broadcasted_iota(jnp.int32, (PAIR_CH, PAIR_ST), 1)
    same_group = (row // SSM_GROUP) == (lane // SSM_STATE)

    def block_diag(ref):
        v = ref[pair]
        return jnp.where(same_group, jnp.concatenate([v, v], axis=1), 0.0)

    c_re, c_im = block_diag(cre_ref), block_diag(cim_ref)
    b_re, b_im = block_diag(bre_ref), block_diag(bim_ref)
    bb_re = q_re * b_re - q_im * b_im
    bb_im = q_re * b_im + q_im * b_re

    powers = {0: (jnp.ones_like(dar), jnp.zeros_like(dar)), 1: (mag * cos1, mag * sin1)}

    def power(k):
        if k not in powers:
            (p_re, p_im), (s_re, s_im) = power(k - 1), powers[1]
            powers[k] = (p_re * s_re - p_im * s_im, p_re * s_im + p_im * s_re)
        return powers[k]

    def cmul_cat(x_re, x_im, p, conj_out):
        p_re, p_im = p
        o_re = x_re * p_re - x_im * p_im
        o_im = x_re * p_im + x_im * p_re
        return jnp.concatenate([o_re, -o_im if conj_out else o_im], axis=1)

    for t in range(CHUNK):
        rs = slice(t * PAIR_CH, (t + 1) * PAIR_CH)
        cl_s[rs, :] = cmul_cat(c_re, c_im, power(t), True)
        cst_ref[pair, rs, :] =cmul_cat(c_re, c_im, power(t + 1), True).astype(BF16)
        bst_ref[pair, rs, :] = cmul_cat(bb_re, bb_im, power(CHUNK - 1 - t),
                                        False).astype(BF16)
    lre_ref[pair], lim_ref[pair] = power(CHUNK)

    cl_hi, cl_lo = _split_bf16(cl_s[...])
    bb = jnp.concatenate([bb_re, bb_im], axis=1)
    bb = jnp.concatenate([bb, jnp.zeros((LANES - PAIR_CH, 2 * PAIR_ST), F32)], axis=0)
    bb_hi, bb_lo = _split_bf16(bb)

    def dot_nt(a, b):
        return lax.dot_general(a, b, (((1,), (1,)), ((), ())), preferred_element_type=F32)

    kcol = dot_nt(cl_hi, bb_hi) + (dot_nt(cl_hi, bb_lo) + dot_nt(cl_lo, bb_hi))

    per_tile = LANES // PAIR_CH
    ri = lax.broadcasted_iota(jnp.int32, (PAIR_ROWS, LANES), 0)
    ci = lax.broadcasted_iota(jnp.int32, (PAIR_ROWS, LANES), 1)
    for tile in range(PAIR_ROWS // LANES):
        acc = jnp.where(ri == ci + tile * LANES,
                        dl_ref[pair, :, tile * LANES:(tile + 1) * LANES], 0.0)
        for q in range(per_tile):
            tau = tile * per_tile + q
            piece = kcol if q == 0 else pltpu.roll(kcol, q * PAIR_CH, axis=1)
            if tau:
                piece = jnp.concatenate(
                    [jnp.zeros((tau * PAIR_CH, LANES), F32),
                     piece[:PAIR_ROWS - tau * PAIR_CH, :]], axis=0)
            acc = acc + piece
        kt_ref[pair, :, tile * LANES:(tile + 1) * LANES] = acc.astype(BF16)


def _prep_call(c, w_ada, b_ada, a_re, a_im, b_re, b_im, c_re, c_im, d, log_dt):
    np_, pc, ps, st = N_PAIRS, PAIR_CH, PAIR_ST, SSM_STATE
    bsz = c.shape[0]
    mod_cols = w_ada.shape[1]
    steps = np_ // PREP_PAIRS
    mod_blk = mod_cols // steps
    row = lambda a: a.reshape(np_, 1, ps)
    ldt = jnp.broadcast_to(log_dt[:, None], (SSM_GROUPS, st))
    d_lane = jnp.tile(d.reshape(np_, 1, pc), (1, 1, CHUNK))
    bt = lambda a: jnp.transpose(a, (0, 2, 1)).reshape(np_, pc, st)
    spec = lambda *shape: pl.BlockSpec((PREP_PAIRS,) + shape, lambda p: (p, 0, 0))
    col_spec = lambda rows: pl.BlockSpec((rows, mod_blk), lambda p: (0, p))
    return pl.pallas_call(
        _prep_kernel,
        grid=(steps,),
        in_specs=[pl.BlockSpec((bsz, D_MODEL), lambda p: (0, 0)),
                  col_spec(D_MODEL), col_spec(1),
                  spec(1, ps), spec(1, ps), spec(1, ps), spec(pc, st), spec(pc, st),
                  spec(pc, st), spec(pc, st), spec(1, PAIR_ROWS)],
        out_specs=[col_spec(bsz),
                   spec(PAIR_ROWS, PAIR_ROWS), spec(PAIR_ROWS, 2 * ps),
                   spec(PAIR_ROWS, 2 * ps), spec(1, ps), spec(1, ps)],
        out_shape=[jax.ShapeDtypeStruct((bsz, mod_cols), F32),
                   jax.ShapeDtypeStruct((np_, PAIR_ROWS, PAIR_ROWS), BF16),
                   jax.ShapeDtypeStruct((np_, PAIR_ROWS, 2 * ps), BF16),
                   jax.ShapeDtypeStruct((np_, PAIR_ROWS, 2 * ps), BF16),
                   jax.ShapeDtypeStruct((np_, 1, ps), F32),
                   jax.ShapeDtypeStruct((np_, 1, ps), F32)],
        scratch_shapes=[pltpu.VMEM((PAIR_ROWS, 2 * ps), F32)],
        name="prep",
    )(c, w_ada, b_ada.reshape(1, -1), row(a_re), row(a_im), row(ldt),
      c_re.reshape(np_, pc, st), c_im.reshape(np_, pc, st), bt(b_re), bt(b_im), d_lane)


def _ssm_in_kernel(x_hbm, mod_ref, g_ref, w_ref, o_ref, xbuf, sems, wt_s):
    n_chunks = o_ref.shape[-1]
    n_s = pl.num_programs(1)
    flat = pl.program_id(0) * n_s + pl.program_id(1)
    slot = flat % 2

    def gather(block, buf_slot):
        bb, c0 = block // n_s, (block % n_s) * n_chunks
        part = n_chunks // GATHER_SPLIT
        return [pltpu.make_async_copy(
            x_hbm.at[bb, pl.ds(c0 + k * part, part), tau, :],
            xbuf.at[buf_slot, pl.ds(tau * n_chunks + k * part, part), :],
            sems.at[buf_slot]) for tau in range(CHUNK) for k in range(GATHER_SPLIT)]

    @pl.when(flat == 0)
    def _():
        for cp in gather(flat, slot):
            cp.start()
        wt_s[...] = w_ref[...].T.astype(BF16)

    @pl.when(flat + 1 < pl.num_programs(0) * n_s)
    def _():
        for cp in gather(flat + 1, 1 - slot):
            cp.start()

    for cp in gather(flat, slot):
        cp.wait()

    h = _rms_mod(xbuf[slot], g_ref[...], mod_ref[0, 1:2, :], mod_ref[0, 0:1, :])
    u_t = lax.dot_general(wt_s[...], h.astype(BF16), (((1,), (1,)), ((), ())),
                          preferred_element_type=F32)
    for tau in range(CHUNK):
        o_ref[0, tau] = u_t[:, tau * n_chunks:(tau + 1) * n_chunks].astype(o_ref.dtype)


def _ssm_in_call(x, mod, g, w_in, chunks_per_step):
    bsz, seq, _ = x.shape
    ts = chunks_per_step * CHUNK
    ssm_col_block = 2 * CONV_WIDTH // SSM_WIDTH
    x_chunked = x.reshape(bsz, seq // CHUNK, CHUNK, D_MODEL)
    return pl.pallas_call(
        _ssm_in_kernel,
        grid=(bsz, seq // ts),
        in_specs=[
            pl.BlockSpec(memory_space=pl.ANY),
            pl.BlockSpec((1, 6, D_MODEL), lambda b, s: (b, 0, 0)),
            pl.BlockSpec((1, D_MODEL), lambda b, s: (0, 0)),
            pl.BlockSpec((D_MODEL, SSM_WIDTH), lambda b, s: (0, ssm_col_block)),
        ],
        out_specs=pl.BlockSpec((1, CHUNK, SSM_WIDTH, chunks_per_step),
                               lambda b, s: (b, 0, 0, s)),
        out_shape=jax.ShapeDtypeStruct((bsz, CHUNK, SSM_WIDTH, seq // CHUNK), BF16),
        scratch_shapes=[pltpu.VMEM((2, ts, D_MODEL), F32),
                        pltpu.SemaphoreType.DMA((2,)),
                        pltpu.VMEM((SSM_WIDTH, D_MODEL), BF16)],
        compiler_params=pltpu.CompilerParams(
            dimension_semantics=("arbitrary", "arbitrary"),
            vmem_limit_bytes=VMEM_LIMIT),
        name="ssm_in",
    )(x_chunked, mod, g, w_in)


def _ssm_kernel(u_ref, kt_ref, bst_ref, cst_ref, lre_ref, lim_ref, *rest):
    n_w = (len(rest) - 7) // 2
    w_in_refs, o_ref, w_out_refs = rest[:n_w], rest[n_w], rest[n_w + 1:2 * n_w + 1]
    sre, sim, xre, xim, ys, yn = rest[2 * n_w + 1:]
    n_chunks = u_ref.shape[-1]
    pw = PAIR_ST

    _cast_rows(w_in_refs, w_out_refs)

    def load_pair(j):
        r = u_ref[0, :, j * PAIR_CH:(j + 1) * PAIR_CH, :]
        return r.reshape(PAIR_ROWS, n_chunks)

    for j in range(PAIRS_PER_STEP):
        s = lax.dot_general(load_pair(j), bst_ref[j], (((0,), (0,)), ((), ())),
                            preferred_element_type=F32)
        sre[:, j * pw:(j + 1) * pw] = s[:, :pw]
        sim[:, j * pw:(j + 1) * pw] = s[:, pw:]

    lre = lre_ref[0]
    lim = lim_ref[0]

    def body(c, carry):
        xr, xi = carry
        xre[pl.ds(c, 1), :] = xr
        xim[pl.ds(c, 1), :] = xi
        nr = lre * xr - lim * xi + sre[pl.ds(c, 1), :]
        ni = lre * xi + lim * xr + sim[pl.ds(c, 1), :]
        return nr, ni

    zero = jnp.zeros((1, PAIRS_PER_STEP * pw), F32)
    lax.fori_loop(0, n_chunks, body, (zero, zero), unroll=8)

    for j in range(PAIRS_PER_STEP):
        y_intra = jnp.dot(kt_ref[j], load_pair(j), preferred_element_type=F32)
        xp = jnp.concatenate([xre[:, j * pw:(j + 1) * pw],
                              xim[:, j * pw:(j + 1) * pw]], axis=1).astype(BF16)
        y_state = lax.dot_general(cst_ref[j], xp, (((1,), (1,)), ((), ())),
                                  preferred_element_type=F32)
        y = jax.nn.gelu(y_intra + y_state)
        ys[:, j * PAIR_CH:(j + 1) * PAIR_CH, :] = y.reshape(CHUNK, PAIR_CH, n_chunks)

    for k in range(yn.shape[0]):
        lanes = slice(k * LANES, (k + 1) * LANES)
        for t in range(CHUNK):
            yn[k, pl.ds(t, n_chunks, stride=CHUNK), :] = ys[t, lanes, :].T
        o_ref[0, 0, :, lanes] = yn[k].astype(o_ref.dtype)


def _ssm_call(u_t, kt, bst, cst, lam_re, lam_im, weights):
    bsz, _, _, n_chunks = u_t.shape
    seq = n_chunks * CHUNK
    pps = PAIRS_PER_STEP
    ch = pps * PAIR_CH
    state_w = pps * PAIR_ST
    steps = N_PAIRS // pps
    lam_re = lam_re.reshape(steps, 1, state_w)
    lam_im = lam_im.reshape(steps, 1, state_w)
    n_steps = steps * bsz
    w_specs = [pl.BlockSpec((w.shape[0] // n_steps, w.shape[1]),
                            lambda p, b: (p * bsz + b, 0)) for w in weights]
    outs = pl.pallas_call(
        _ssm_kernel,
        grid=(steps, bsz),
        in_specs=[
            pl.BlockSpec((1, CHUNK, ch, n_chunks), lambda p, b: (b, 0, p, 0)),
            pl.BlockSpec((pps, PAIR_ROWS, PAIR_ROWS), lambda p, b: (p, 0, 0)),
            pl.BlockSpec((pps, PAIR_ROWS, 2 * PAIR_ST), lambda p, b: (p, 0, 0)),
            pl.BlockSpec((pps, PAIR_ROWS, 2 * PAIR_ST), lambda p, b: (p, 0, 0)),
            pl.BlockSpec((1, 1, state_w), lambda p, b: (p, 0, 0)),
            pl.BlockSpec((1, 1, state_w), lambda p, b: (p, 0, 0)),
        ] + w_specs,
        out_specs=[pl.BlockSpec((1, 1, seq, ch), lambda p, b: (b, p, 0, 0))] + w_specs,
        out_shape=[jax.ShapeDtypeStruct((bsz, steps, seq, ch), BF16)] + [
            jax.ShapeDtypeStruct(w.shape, BF16) for w in weights],
        scratch_shapes=[pltpu.VMEM((n_chunks, state_w), F32)] * 4 + [
            pltpu.VMEM((CHUNK, ch, n_chunks), F32),
            pltpu.VMEM((ch // LANES, seq, LANES), F32)],
        compiler_params=pltpu.CompilerParams(
            dimension_semantics=("arbitrary", "arbitrary"),
            vmem_limit_bytes=VMEM_LIMIT),
        name="ssm",
    )(u_t, kt, bst, cst, lam_re, lam_im, *weights)
    return outs[0], outs[1:]


def _mixer_kernel(x_ref, y_ref, mod_ref, g_ref, wc_ref, wgc0_ref, wgc1_ref, wgs0_ref,
                  wgs1_ref, cw_ref, cb_ref, lng_ref, lnb_ref, proj_ref, glu_ref,
                  wout_ref, wf_ref, o_ref, wf_o_ref, ubuf, cbuf, sbuf):
    _cast_rows((wf_ref,), (wf_o_ref,))

    @pl.when(pl.program_id(1) == 0)
    def _():
        ubuf[:, 0:CONV_HALO, :] = jnp.zeros((CONV_WIDTH // LANES, CONV_HALO, LANES), F32)

    for r0 in range(0, x_ref.shape[1], MIX_ROWS):
        _mixer_rows(r0, x_ref, y_ref, mod_ref, g_ref, wc_ref, wgc0_ref, wgc1_ref,
                    wgs0_ref, wgs1_ref, cw_ref, cb_ref, lng_ref, lnb_ref, proj_ref,
                    glu_ref, wout_ref, o_ref, ubuf, cbuf, sbuf)


def _mixer_rows(r0, x_ref, y_ref, mod_ref, g_ref, wc_ref, wgc0_ref, wgc1_ref, wgs0_ref,
                wgs1_ref, cw_ref, cb_ref, lng_ref, lnb_ref, proj_ref, glu_ref,
                wout_ref, o_ref, ubuf, cbuf, sbuf):
    ts = MIX_ROWS
    rows = slice(r0, r0 + ts)
    halo = CONV_HALO
    half = D_MODEL // 2
    n_lt = CONV_WIDTH // LANES

    x = x_ref[0, rows, :]
    h = _rms_mod(x, g_ref[...], mod_ref[0, 1:2, :], mod_ref[0, 0:1, :]).astype(BF16)

    zc = jnp.dot(h, wc_ref[...], preferred_element_type=F32)
    u = zc[:, :CONV_WIDTH] * _sigmoid(zc[:, CONV_WIDTH:])
    for k in range(n_lt):
        ubuf[k, halo:halo + ts, :] = u[:, k * LANES:(k + 1) * LANES]

    y_in = jnp.concatenate([y_ref[0, k, rows, :] for k in range(y_ref.shape[1])], axis=1)

    def dot_h(w_ref):
        return jnp.dot(h, w_ref[...], preferred_element_type=F32)

    def dot_y(k):
        return jnp.dot(y_in, glu_ref[:, k * half:(k + 1) * half],
                       preferred_element_type=F32)

    gc0, gs0, za0, zb0, gc1, gs1, za1, zb1 = range(N_SIDE)

    def put(k, v):
        sbuf[k] = v

    dyn_zero = jnp.minimum(pl.program_id(1), 0)
    never = lax.broadcasted_iota(jnp.int32, (SUBLANES, LANES), 0) < dyn_zero

    def anchored(term, k):
        fence = sbuf[dyn_zero, k * SUBLANES:(k + 1) * SUBLANES, 0:LANES]
        head = jnp.where(never, fence, term[0:SUBLANES, :])
        return jnp.concatenate([head, term[SUBLANES:, :]], axis=0)

    side_jobs = [
        lambda: put(gc0, dot_h(wgc0_ref)), lambda: put(gs0, dot_h(wgs0_ref)),
        lambda: put(za0, dot_y(0)), lambda: put(zb0, dot_y(2)),
        lambda: put(gc1, dot_h(wgc1_ref)), lambda: put(gs1, dot_h(wgs1_ref)),
        lambda: put(za1, dot_y(1)), lambda: put(zb1, dot_y(3)),
    ]
    first = halo - (CONV_KERNEL - 1)
    n_rt = ts // CONV_ROWS
    for i in range(n_rt):
        for k in range(n_lt):
            lanes = slice(k * LANES, (k + 1) * LANES)
            acc = None
            for j in range(CONV_KERNEL):
                row0 = first + j + i * CONV_ROWS
                term = cw_ref[j:j + 1, lanes] * ubuf[k, row0:row0 + CONV_ROWS, :]
                if j == 0 and i > 0:
                    term = anchored(term, k)
                acc = term if acc is None else acc + term
            cbuf[i * CONV_ROWS:(i + 1) * CONV_ROWS, lanes] = acc + cb_ref[:, lanes]
        if (i + 1) * len(side_jobs) % n_rt == 0:
            side_jobs[(i + 1) * len(side_jobs) // n_rt - 1]()
    for k in range(n_lt):
        ubuf[k, 0:halo, :] = ubuf[k, ts:ts + halo, :]
    acc = cbuf[...]
    mu = jnp.mean(acc, axis=-1, keepdims=True)
    cen = acc - mu
    var = jnp.mean(cen * cen, axis=-1, keepdims=True)
    yn = cen * lax.rsqrt(var + EPS) * lng_ref[...] + lnb_ref[...]
    yn = yn * _sigmoid(yn)
    y_conv = jnp.dot(yn.astype(BF16), proj_ref[...], preferred_element_type=F32)

    def merge(conv_half, g_conv, g_ssm, z_a, z_b):
        y_ssm = sbuf[z_a] * _sigmoid(sbuf[z_b])
        return _sigmoid(sbuf[g_conv]) * conv_half + _sigmoid(sbuf[g_ssm]) * y_ssm

    merged = jnp.concatenate(
        [merge(y_conv[:, :half], gc0, gs0, za0, zb0),
         merge(y_conv[:, half:], gc1, gs1, za1, zb1)], axis=1)
    upd = jnp.dot(merged.astype(BF16), wout_ref[...], preferred_element_type=F32)
    o_ref[0, rows, :] = x + mod_ref[0, 2:3, :] * upd


def _const_spec(shape, index=None):
    index = (0,) * len(shape) if index is None else index
    return pl.BlockSpec(shape, lambda b, s: index, pipeline_mode=pl.Buffered(1))


def _mixer_call(x, y, mod, g, w_in_b, cw, cb, lng, lnb, proj, glu, wout, w_ffn_in, ts):
    bsz, seq, _ = x.shape
    n_s = seq // ts
    gate0 = (2 * CONV_WIDTH + SSM_WIDTH) // GATE_BLOCK
    gate_spec = lambda i: _const_spec((D_MODEL, GATE_BLOCK), (0, gate0 + i))
    wf_rows, wf_cols = w_ffn_in.shape
    wf_spec = pl.BlockSpec((wf_rows // (bsz * n_s), wf_cols), lambda b, s: (b * n_s + s, 0))
    return pl.pallas_call(
        _mixer_kernel,
        grid=(bsz, seq // ts),
        in_specs=[
            pl.BlockSpec((1, ts, D_MODEL), lambda b, s: (b, s, 0)),
            pl.BlockSpec((1, y.shape[1], ts, y.shape[3]), lambda b, s: (b, 0, s, 0)),
            pl.BlockSpec((1, 6, D_MODEL), lambda b, s: (b, 0, 0)),
            _const_spec((1, D_MODEL)),
            _const_spec((D_MODEL, 2 * CONV_WIDTH)),
            gate_spec(0), gate_spec(1), gate_spec(2), gate_spec(3),
            _const_spec((CONV_KERNEL, CONV_WIDTH)),
            _const_spec((1, CONV_WIDTH)),
            _const_spec((1, CONV_WIDTH)),
            _const_spec((1, CONV_WIDTH)),
            _const_spec((CONV_WIDTH, D_MODEL)),
            _const_spec((SSM_WIDTH, 2 * D_MODEL)),
            _const_spec((D_MODEL, D_MODEL)),
            wf_spec,
        ],
        out_specs=[pl.BlockSpec((1, ts, D_MODEL), lambda b, s: (b, s, 0)), wf_spec],
        out_shape=[jax.ShapeDtypeStruct(x.shape, F32),
                   jax.ShapeDtypeStruct(w_ffn_in.shape, BF16)],
        scratch_shapes=[
            pltpu.VMEM((CONV_WIDTH // LANES, MIX_ROWS + CONV_HALO, LANES), F32),
            pltpu.VMEM((MIX_ROWS, CONV_WIDTH), F32),
            pltpu.VMEM((N_SIDE, MIX_ROWS, D_MODEL // 2), F32)],
        compiler_params=pltpu.CompilerParams(
            dimension_semantics=("arbitrary", "arbitrary"),
            vmem_limit_bytes=VMEM_LIMIT),
        name="mixer",
    )(x, y, mod, g, w_in_b, w_in_b, w_in_b, w_in_b, w_in_b, cw, cb, lng, lnb, proj, glu,
      wout, w_ffn_in)


def _ffn_kernel(x_ref, mod_ref, g_ref, wg_ref, wu_ref, wo_ref, fg_ref, o_ref, act_s):
    x = x_ref[0]
    h = _rms_mod(x, g_ref[...], mod_ref[0, 4:5, :], mod_ref[0, 3:4, :]).astype(BF16)
    for c0 in range(0, FFN_HIDDEN, FFN_CHUNK):
        cols = slice(c0, c0 + FFN_CHUNK)
        fg = jnp.dot(h, wg_ref[:, cols], preferred_element_type=F32)
        fu = jnp.dot(h, wu_ref[:, cols], preferred_element_type=F32)
        act_s[:, cols] = (fg * _sigmoid(fg) * fu).astype(BF16)
    upd = jnp.dot(act_s[...], wo_ref[...], preferred_element_type=F32)
    x2 = x + mod_ref[0, 5:6, :] * upd
    ms = jnp.mean(x2 * x2, axis=-1, keepdims=True)
    o_ref[0] = x2 * lax.rsqrt(ms + EPS) * fg_ref[...]


def _ffn_call(x, mod, g, w_ffn_in_b, wo, final_g, ts):
    bsz, seq, _ = x.shape
    return pl.pallas_call(
        _ffn_kernel,
        grid=(bsz, seq // ts),
        in_specs=[
            pl.BlockSpec((1, ts, D_MODEL), lambda b, s: (b, s, 0)),
            pl.BlockSpec((1, 6, D_MODEL), lambda b, s: (b, 0, 0)),
            _const_spec((1, D_MODEL)),
            _const_spec((D_MODEL, FFN_HIDDEN), (0, 0)),
            _const_spec((D_MODEL, FFN_HIDDEN), (0, 1)),
            _const_spec((FFN_HIDDEN, D_MODEL)),
            _const_spec((1, D_MODEL)),
        ],
        out_specs=pl.BlockSpec((1, ts, D_MODEL), lambda b, s: (b, s, 0)),
        out_shape=jax.ShapeDtypeStruct(x.shape, F32),
        scratch_shapes=[pltpu.VMEM((ts, FFN_HIDDEN), BF16)],
        compiler_params=pltpu.CompilerParams(
            dimension_semantics=("arbitrary", "arbitrary"),
            vmem_limit_bytes=VMEM_LIMIT),
        name="ffn",
    )(x, mod, g, w_ffn_in_b, w_ffn_in_b, wo, final_g)


def kernel(x, c, w_ada, b_ada, norm1_g, w_in, conv_w, conv_b, conv_ln_g, conv_ln_b,
           conv_proj, ssm_a_re, ssm_a_im, ssm_b_re, ssm_b_im, ssm_c_re, ssm_c_im,
           ssm_d, ssm_log_dt, ssm_glu, w_out, norm2_g, w_ffn_in, w_ffn_out, final_g):
    bsz, seq, _ = x.shape
    assert w_ada.shape[0] == 1, "single layer only"

    mod, *ops = _prep_call(c, w_ada[0], b_ada[0], ssm_a_re[0], ssm_a_im[0], ssm_b_re[0],
                           ssm_b_im[0], ssm_c_re[0], ssm_c_im[0], ssm_d[0],
                           ssm_log_dt[0])
    mod = mod.reshape(bsz, -1, D_MODEL)

    g1 = norm1_g[0].reshape(1, -1)

    u_t = _ssm_in_call(x, mod, g1, w_in[0], 128)
    y, (w_in_b, proj_b, glu_b, w_out_b, w_ffn_out_b) = _ssm_call(
        u_t, *ops, (w_in[0], conv_proj[0], ssm_glu[0], w_out[0], w_ffn_out[0]))

    x1, w_ffn_in_b = _mixer_call(
        x, y, mod, g1, w_in_b, conv_w[0], conv_b[0].reshape(1, -1),
        conv_ln_g[0].reshape(1, -1), conv_ln_b[0].reshape(1, -1),
        proj_b, glu_b, w_out_b, w_ffn_in[0], 1024)

    return _ffn_call(x1, mod, norm2_g[0].reshape(1, -1), w_ffn_in_b, w_ffn_out_b,
                     final_g.reshape(1, -1), 1024)
```

```python
import jax
import jax.numpy as jnp
from jax import lax
from jax.experimental import pallas as pl
from jax.experimental.pallas import tpu as pltpu

F32 = jnp.float32
BF16 = jnp.bfloat16

D_MODEL = 1024
CONV_WIDTH = 512
CONV_KERNEL = 31
SSM_WIDTH = 512
SSM_GROUP = 16
SSM_GROUPS = 32
SSM_STATE = 64
FFN_HIDDEN = 2816
EPS = 1e-6

CHUNK = 16
PAIR_CH = 2 * SSM_GROUP
PAIR_ST = 2 * SSM_STATE
PAIR_ROWS = CHUNK * PAIR_CH
N_PAIRS = SSM_GROUPS // 2
PAIRS_PER_STEP = 8
PREP_PAIRS = 2
CONV_HALO = 32
GATE_BLOCK = 512
LANES = 128
SUBLANES = 8
CONV_ROWS = 64
FFN_CHUNK = 256
N_SIDE = 8
MIX_ROWS = 512

V7X_VMEM_BYTES = 64 * 1024 * 1024
VMEM_LIMIT = V7X_VMEM_BYTES * 7 // 8


def _rms_mod(x, g, scale, shift):
    ms = jnp.mean(x * x, axis=-1, keepdims=True)
    return (x * lax.rsqrt(ms + EPS)) * (g * (1.0 + scale)) + shift


def _sigmoid(v):
    return jax.nn.sigmoid(v)


def _split_bf16(v):
    hi = v.astype(BF16)
    return hi, (v - hi.astype(F32)).astype(BF16)


def _cast_rows(src_refs, dst_refs):
    for src, dst in zip(src_refs, dst_refs):
        dst[...] = src[...].astype(BF16)


def _prep_kernel(c_ref, wada_ref, bada_ref, ar_ref, ai_ref, ldt_ref, cre_ref, cim_ref,
                 bre_ref, bim_ref, dl_ref, mod_ref, kt_ref, bst_ref, cst_ref, lre_ref,
                 lim_ref, cl_s):
    c = c_ref[...]
    ca = (c * _sigmoid(c)).astype(BF16)
    mod_ref[...] = jnp.dot(ca, wada_ref[...].astype(BF16),
                           preferred_element_type=F32) + bada_ref[...]
    for j in range(ar_ref.shape[0]):
        _pair_operators(j, ar_ref, ai_ref, ldt_ref, cre_ref, cim_ref, bre_ref, bim_ref,
                        dl_ref, kt_ref, bst_ref, cst_ref, lre_ref, lim_ref, cl_s)


def _pair_operators(pair, ar_ref, ai_ref, ldt_ref, cre_ref, cim_ref, bre_ref, bim_ref,
                    dl_ref, kt_ref, bst_ref, cst_ref, lre_ref, lim_ref, cl_s):
    ar, ai = ar_ref[pair], ai_ref[pair]
    dt = jnp.exp(ldt_ref[pair])
    dar, dai = dt * ar, dt * ai
    mag, cos1, sin1 = jnp.exp(dar), jnp.cos(dai), jnp.sin(dai)
    n_re, n_im = mag * cos1 - 1.0, mag * sin1
    den = ar * ar + ai * ai
    q_re = (n_re * ar + n_im * ai) / den
    q_im = (n_im * ar - n_re * ai) / den

    row = lax.broadcasted_iota(jnp.int32, (PAIR_CH, PAIR_ST), 0)
    lane = lax.broadcasted_iota(jnp.int32, (PAIR_CH, PAIR_ST), 1)
    same_group = (row // SSM_GROUP) == (lane // SSM_STATE)

    def block_diag(ref):
        v = ref[pair]
        return jnp.where(same_group, jnp.concatenate([v, v], axis=1), 0.0)

    c_re, c_im = block_diag(cre_ref), block_diag(cim_ref)
    b_re, b_im = block_diag(bre_ref), block_diag(bim_ref)
    bb_re = q_re * b_re - q_im * b_im
    bb_im = q_re * b_im + q_im * b_re

    powers = {0: (jnp.ones_like(dar), jnp.zeros_like(dar)), 1: (mag * cos1, mag * sin1)}

    def power(k):
        if k not in powers:
            (p_re, p_im), (s_re, s_im) = power(k - 1), powers[1]
            powers[k] = (p_re * s_re - p_im * s_im, p_re * s_im + p_im * s_re)
        return powers[k]

    def cmul_cat(x_re, x_im, p, conj_out):
        p_re, p_im = p
        o_re = x_re * p_re - x_im * p_im
        o_im = x_re * p_im + x_im * p_re
        return jnp.concatenate([o_re, -o_im if conj_out else o_im], axis=1)

    for t in range(CHUNK):
        rs = slice(t * PAIR_CH, (t + 1) * PAIR_CH)
        cl_s[rs, :] = cmul_cat(c_re, c_im, power(t), True)
        cst_ref[pair, rs, :] =cmul_cat(c_re, c_im, power(t + 1), True).astype(BF16)
        bst_ref[pair, rs, :] = cmul_cat(bb_re, bb_im, power(CHUNK - 1 - t),
                                        False).astype(BF16)
    lre_ref[pair], lim_ref[pair] = power(CHUNK)

    cl_hi, cl_lo = _split_bf16(cl_s[...])
    bb = jnp.concatenate([bb_re, bb_im], axis=1)
    bb = jnp.concatenate([bb, jnp.zeros((LANES - PAIR_CH, 2 * PAIR_ST), F32)], axis=0)
    bb_hi, bb_lo = _split_bf16(bb)

    def dot_nt(a, b):
        return lax.dot_general(a, b, (((1,), (1,)), ((), ())), preferred_element_type=F32)

    kcol = dot_nt(cl_hi, bb_hi) + (dot_nt(cl_hi, bb_lo) + dot_nt(cl_lo, bb_hi))

    per_tile = LANES // PAIR_CH
    ri = lax.broadcasted_iota(jnp.int32, (PAIR_ROWS, LANES), 0)
    ci = lax.broadcasted_iota(jnp.int32, (PAIR_ROWS, LANES), 1)
    for tile in range(PAIR_ROWS // LANES):
        acc = jnp.where(ri == ci + tile * LANES,
                        dl_ref[pair, :, tile * LANES:(tile + 1) * LANES], 0.0)
        for q in range(per_tile):
            tau = tile * per_tile + q
            piece = kcol if q == 0 else pltpu.roll(kcol, q * PAIR_CH, axis=1)
            if tau:
                piece = jnp.concatenate(
                    [jnp.zeros((tau * PAIR_CH, LANES), F32),
                     piece[:PAIR_ROWS - tau * PAIR_CH, :]], axis=0)
            acc = acc + piece
        kt_ref[pair, :, tile * LANES:(tile + 1) * LANES] = acc.astype(BF16)


def _prep_call(c, w_ada, b_ada, a_re, a_im, b_re, b_im, c_re, c_im, d, log_dt):
    np_, pc, ps, st = N_PAIRS, PAIR_CH, PAIR_ST, SSM_STATE
    bsz = c.shape[0]
    mod_cols = w_ada.shape[1]
    steps = np_ // PREP_PAIRS
    mod_blk = mod_cols // steps
    row = lambda a: a.reshape(np_, 1, ps)
    ldt = jnp.broadcast_to(log_dt[:, None], (SSM_GROUPS, st))
    d_lane = jnp.tile(d.reshape(np_, 1, pc), (1, 1, CHUNK))
    bt = lambda a: jnp.transpose(a, (0, 2, 1)).reshape(np_, pc, st)
    spec = lambda *shape: pl.BlockSpec((PREP_PAIRS,) + shape, lambda p: (p, 0, 0))
    col_spec = lambda rows: pl.BlockSpec((rows, mod_blk), lambda p: (0, p))
    return pl.pallas_call(
        _prep_kernel,
        grid=(steps,),
        in_specs=[pl.BlockSpec((bsz, D_MODEL), lambda p: (0, 0)),
                  col_spec(D_MODEL), col_spec(1),
                  spec(1, ps), spec(1, ps), spec(1, ps), spec(pc, st), spec(pc, st),
                  spec(pc, st), spec(pc, st), spec(1, PAIR_ROWS)],
        out_specs=[col_spec(bsz),
                   spec(PAIR_ROWS, PAIR_ROWS), spec(PAIR_ROWS, 2 * ps),
                   spec(PAIR_ROWS, 2 * ps), spec(1, ps), spec(1, ps)],
        out_shape=[jax.ShapeDtypeStruct((bsz, mod_cols), F32),
                   jax.ShapeDtypeStruct((np_, PAIR_ROWS, PAIR_ROWS), BF16),
                   jax.ShapeDtypeStruct((np_, PAIR_ROWS, 2 * ps), BF16),
                   jax.ShapeDtypeStruct((np_, PAIR_ROWS, 2 * ps), BF16),
                   jax.ShapeDtypeStruct((np_, 1, ps), F32),
                   jax.ShapeDtypeStruct((np_, 1, ps), F32)],
        scratch_shapes=[pltpu.VMEM((PAIR_ROWS, 2 * ps), F32)],
        name="prep",
    )(c, w_ada, b_ada.reshape(1, -1), row(a_re), row(a_im), row(ldt),
      c_re.reshape(np_, pc, st), c_im.reshape(np_, pc, st), bt(b_re), bt(b_im), d_lane)


def _ssm_in_kernel(x_hbm, mod_ref, g_ref, w_ref, o_ref, xbuf, sems, wt_s):
    n_chunks = o_ref.shape[-1]
    n_s = pl.num_programs(1)
    flat = pl.program_id(0) * n_s + pl.program_id(1)
    slot = flat % 2

    def gather(block, buf_slot):
        bb, c0 = block // n_s, (block % n_s) * n_chunks
        return [pltpu.make_async_copy(
            x_hbm.at[bb, c0 + c], xbuf.at[buf_slot, :, c, :], sems.at[buf_slot])
            for c in range(n_chunks)]

    @pl.when(flat == 0)
    def _():
        for cp in gather(flat, slot):
            cp.start()
        wt_s[...] = w_ref[...].T.astype(BF16)

    @pl.when(flat + 1 < pl.num_programs(0) * n_s)
    def _():
        for cp in gather(flat + 1, 1 - slot):
            cp.start()

    for cp in gather(flat, slot):
        cp.wait()

    x = xbuf[slot].reshape(CHUNK * n_chunks, D_MODEL)
    h = _rms_mod(x, g_ref[...], mod_ref[0, 1:2, :], mod_ref[0, 0:1, :])
    u_t = lax.dot_general(wt_s[...], h.astype(BF16), (((1,), (1,)), ((), ())),
                          preferred_element_type=F32)
    for tau in range(CHUNK):
        o_ref[0, tau] = u_t[:, tau * n_chunks:(tau + 1) * n_chunks].astype(o_ref.dtype)


def _ssm_in_call(x, mod, g, w_in, chunks_per_step):
    bsz, seq, _ = x.shape
    ts = chunks_per_step * CHUNK
    ssm_col_block = 2 * CONV_WIDTH // SSM_WIDTH
    x_chunked = x.reshape(bsz, seq // CHUNK, CHUNK, D_MODEL)
    return pl.pallas_call(
        _ssm_in_kernel,
        grid=(bsz, seq // ts),
        in_specs=[
            pl.BlockSpec(memory_space=pl.ANY),
            pl.BlockSpec((1, 6, D_MODEL), lambda b, s: (b, 0, 0)),
            pl.BlockSpec((1, D_MODEL), lambda b, s: (0, 0)),
            pl.BlockSpec((D_MODEL, SSM_WIDTH), lambda b, s: (0, ssm_col_block)),
        ],
        out_specs=pl.BlockSpec((1, CHUNK, SSM_WIDTH, chunks_per_step),
                               lambda b, s: (b, 0, 0, s)),
        out_shape=jax.ShapeDtypeStruct((bsz, CHUNK, SSM_WIDTH, seq // CHUNK), BF16),
        scratch_shapes=[pltpu.VMEM((2, CHUNK, chunks_per_step, D_MODEL), F32),
                        pltpu.SemaphoreType.DMA((2,)),
                        pltpu.VMEM((SSM_WIDTH, D_MODEL), BF16)],
        compiler_params=pltpu.CompilerParams(
            dimension_semantics=("arbitrary", "arbitrary"),
            vmem_limit_bytes=VMEM_LIMIT),
        name="ssm_in",
    )(x_chunked, mod, g, w_in)


def _ssm_kernel(u_ref, kt_ref, bst_ref, cst_ref, lre_ref, lim_ref, *rest):
    n_w = (len(rest) - 7) // 2
    w_in_refs, o_ref, w_out_refs = rest[:n_w], rest[n_w], rest[n_w + 1:2 * n_w + 1]
    sre, sim, xre, xim, ys, yn = rest[2 * n_w + 1:]
    n_chunks = u_ref.shape[-1]
    pw = PAIR_ST

    _cast_rows(w_in_refs, w_out_refs)

    def load_pair(j):
        r = u_ref[0, :, j * PAIR_CH:(j + 1) * PAIR_CH, :]
        return r.reshape(PAIR_ROWS, n_chunks)

    for j in range(PAIRS_PER_STEP):
        s = lax.dot_general(load_pair(j), bst_ref[j], (((0,), (0,)), ((), ())),
                            preferred_element_type=F32)
        sre[:, j * pw:(j + 1) * pw] = s[:, :pw]
        sim[:, j * pw:(j + 1) * pw] = s[:, pw:]

    lre = lre_ref[0]
    lim = lim_ref[0]

    def body(c, carry):
        xr, xi = carry
        xre[pl.ds(c, 1), :] = xr
        xim[pl.ds(c, 1), :] = xi
        nr = lre * xr - lim * xi + sre[pl.ds(c, 1), :]
        ni = lre * xi + lim * xr + sim[pl.ds(c, 1), :]
        return nr, ni

    zero = jnp.zeros((1, PAIRS_PER_STEP * pw), F32)
    lax.fori_loop(0, n_chunks, body, (zero, zero), unroll=8)

    for j in range(PAIRS_PER_STEP):
        y_intra = jnp.dot(kt_ref[j], load_pair(j), preferred_element_type=F32)
        xp = jnp.concatenate([xre[:, j * pw:(j + 1) * pw],
                              xim[:, j * pw:(j + 1) * pw]], axis=1).astype(BF16)
        y_state = lax.dot_general(cst_ref[j], xp, (((1,), (1,)), ((), ())),
                                  preferred_element_type=F32)
        y = jax.nn.gelu(y_intra + y_state)
        ys[:, j * PAIR_CH:(j + 1) * PAIR_CH, :] = y.reshape(CHUNK, PAIR_CH, n_chunks)

    for k in range(yn.shape[0]):
        lanes = slice(k * LANES, (k + 1) * LANES)
        for t in range(CHUNK):
            yn[k, pl.ds(t, n_chunks, stride=CHUNK), :] = ys[t, lanes, :].T
        o_ref[0, 0, :, lanes] = yn[k].astype(o_ref.dtype)


def _ssm_call(u_t, kt, bst, cst, lam_re, lam_im, weights):
    bsz, _, _, n_chunks = u_t.shape
    seq = n_chunks * CHUNK
    pps = PAIRS_PER_STEP
    ch = pps * PAIR_CH
    state_w = pps * PAIR_ST
    steps = N_PAIRS // pps
    lam_re = lam_re.reshape(steps, 1, state_w)
    lam_im = lam_im.reshape(steps, 1, state_w)
    n_steps = steps * bsz
    w_specs = [pl.BlockSpec((w.shape[0] // n_steps, w.shape[1]),
                            lambda p, b: (p * bsz + b, 0)) for w in weights]
    outs = pl.pallas_call(
        _ssm_kernel,
        grid=(steps, bsz),
        in_specs=[
            pl.BlockSpec((1, CHUNK, ch, n_chunks), lambda p, b: (b, 0, p, 0)),
            pl.BlockSpec((pps, PAIR_ROWS, PAIR_ROWS), lambda p, b: (p, 0, 0)),
            pl.BlockSpec((pps, PAIR_ROWS, 2 * PAIR_ST), lambda p, b: (p, 0, 0)),
            pl.BlockSpec((pps, PAIR_ROWS, 2 * PAIR_ST), lambda p, b: (p, 0, 0)),
            pl.BlockSpec((1, 1, state_w), lambda p, b: (p, 0, 0)),
            pl.BlockSpec((1, 1, state_w), lambda p, b: (p, 0, 0)),
        ] + w_specs,
        out_specs=[pl.BlockSpec((1, 1, seq, ch), lambda p, b: (b, p, 0, 0))] + w_specs,
        out_shape=[jax.ShapeDtypeStruct((bsz, steps, seq, ch), BF16)] + [
            jax.ShapeDtypeStruct(w.shape, BF16) for w in weights],
        scratch_shapes=[pltpu.VMEM((n_chunks, state_w), F32)] * 4 + [
            pltpu.VMEM((CHUNK, ch, n_chunks), F32),
            pltpu.VMEM((ch // LANES, seq, LANES), F32)],
        compiler_params=pltpu.CompilerParams(
            dimension_semantics=("arbitrary", "arbitrary"),
            vmem_limit_bytes=VMEM_LIMIT),
        name="ssm",
    )(u_t, kt, bst, cst, lam_re, lam_im, *weights)
    return outs[0], outs[1:]


def _mixer_kernel(x_ref, y_ref, mod_ref, g_ref, wc_ref, wgc0_ref, wgc1_ref, wgs0_ref,
                  wgs1_ref, cw_ref, cb_ref, lng_ref, lnb_ref, proj_ref, glu_ref,
                  wout_ref, wf_ref, o_ref, wf_o_ref, ubuf, cbuf, sbuf):
    _cast_rows((wf_ref,), (wf_o_ref,))

    @pl.when(pl.program_id(1) == 0)
    def _():
        ubuf[:, 0:CONV_HALO, :] = jnp.zeros((CONV_WIDTH // LANES, CONV_HALO, LANES), F32)

    for r0 in range(0, x_ref.shape[1], MIX_ROWS):
        _mixer_rows(r0, x_ref, y_ref, mod_ref, g_ref, wc_ref, wgc0_ref, wgc1_ref,
                    wgs0_ref, wgs1_ref, cw_ref, cb_ref, lng_ref, lnb_ref, proj_ref,
                    glu_ref, wout_ref, o_ref, ubuf, cbuf, sbuf)


def _mixer_rows(r0, x_ref, y_ref, mod_ref, g_ref, wc_ref, wgc0_ref, wgc1_ref, wgs0_ref,
                wgs1_ref, cw_ref, cb_ref, lng_ref, lnb_ref, proj_ref, glu_ref,
                wout_ref, o_ref, ubuf, cbuf, sbuf):
    ts = MIX_ROWS
    rows = slice(r0, r0 + ts)
    halo = CONV_HALO
    half = D_MODEL // 2
    n_lt = CONV_WIDTH // LANES

    x = x_ref[0, rows, :]
    h = _rms_mod(x, g_ref[...], mod_ref[0, 1:2, :], mod_ref[0, 0:1, :]).astype(BF16)

    zc = jnp.dot(h, wc_ref[...], preferred_element_type=F32)
    u = zc[:, :CONV_WIDTH] * _sigmoid(zc[:, CONV_WIDTH:])
    for k in range(n_lt):
        ubuf[k, halo:halo + ts, :] = u[:, k * LANES:(k + 1) * LANES]

    y_in = jnp.concatenate([y_ref[0, k, rows, :] for k in range(y_ref.shape[1])], axis=1)

    def dot_h(w_ref):
        return jnp.dot(h, w_ref[...], preferred_element_type=F32)

    def dot_y(k):
        return jnp.dot(y_in, glu_ref[:, k * half:(k + 1) * half],
                       preferred_element_type=F32)

    gc0, gs0, za0, zb0, gc1, gs1, za1, zb1 = range(N_SIDE)

    def put(k, v):
        sbuf[k] = v

    dyn_zero = jnp.minimum(pl.program_id(1), 0)
    never = lax.broadcasted_iota(jnp.int32, (SUBLANES, LANES), 0) < dyn_zero

    def anchored(term, k):
        fence = sbuf[dyn_zero, k * SUBLANES:(k + 1) * SUBLANES, 0:LANES]
        head = jnp.where(never, fence, term[0:SUBLANES, :])
        return jnp.concatenate([head, term[SUBLANES:, :]], axis=0)

    side_jobs = [
        lambda: put(gc0, dot_h(wgc0_ref)), lambda: put(gs0, dot_h(wgs0_ref)),
        lambda: put(za0, dot_y(0)), lambda: put(zb0, dot_y(2)),
        lambda: put(gc1, dot_h(wgc1_ref)), lambda: put(gs1, dot_h(wgs1_ref)),
        lambda: put(za1, dot_y(1)), lambda: put(zb1, dot_y(3)),
    ]
    first = halo - (CONV_KERNEL - 1)
    n_rt = ts // CONV_ROWS
    for i in range(n_rt):
        for k in range(n_lt):
            lanes = slice(k * LANES, (k + 1) * LANES)
            acc = None
            for j in range(CONV_KERNEL):
                row0 = first + j + i * CONV_ROWS
                term = cw_ref[j:j + 1, lanes] * ubuf[k, row0:row0 + CONV_ROWS, :]
                if j == 0 and i > 0:
                    term = anchored(term, k)
                acc = term if acc is None else acc + term
            cbuf[i * CONV_ROWS:(i + 1) * CONV_ROWS, lanes] = acc + cb_ref[:, lanes]
        if (i + 1) * len(side_jobs) % n_rt == 0:
            side_jobs[(i + 1) * len(side_jobs) // n_rt - 1]()
    for k in range(n_lt):
        ubuf[k, 0:halo, :] = ubuf[k, ts:ts + halo, :]
    acc = cbuf[...]
    mu = jnp.mean(acc, axis=-1, keepdims=True)
    cen = acc - mu
    var = jnp.mean(cen * cen, axis=-1, keepdims=True)
    yn = cen * lax.rsqrt(var + EPS) * lng_ref[...] + lnb_ref[...]
    yn = yn * _sigmoid(yn)
    y_conv = jnp.dot(yn.astype(BF16), proj_ref[...], preferred_element_type=F32)

    def merge(conv_half, g_conv, g_ssm, z_a, z_b):
        y_ssm = sbuf[z_a] * _sigmoid(sbuf[z_b])
        return _sigmoid(sbuf[g_conv]) * conv_half + _sigmoid(sbuf[g_ssm]) * y_ssm

    merged = jnp.concatenate(
        [merge(y_conv[:, :half], gc0, gs0, za0, zb0),
         merge(y_conv[:, half:], gc1, gs1, za1, zb1)], axis=1)
    upd = jnp.dot(merged.astype(BF16), wout_ref[...], preferred_element_type=F32)
    o_ref[0, rows, :] = x + mod_ref[0, 2:3, :] * upd


def _const_spec(shape, index=None):
    index = (0,) * len(shape) if index is None else index
    return pl.BlockSpec(shape, lambda b, s: index, pipeline_mode=pl.Buffered(1))


def _mixer_call(x, y, mod, g, w_in_b, cw, cb, lng, lnb, proj, glu, wout, w_ffn_in, ts):
    bsz, seq, _ = x.shape
    n_s = seq // ts
    gate0 = (2 * CONV_WIDTH + SSM_WIDTH) // GATE_BLOCK
    gate_spec = lambda i: _const_spec((D_MODEL, GATE_BLOCK), (0, gate0 + i))
    wf_rows, wf_cols = w_ffn_in.shape
    wf_spec = pl.BlockSpec((wf_rows // (bsz * n_s), wf_cols), lambda b, s: (b * n_s + s, 0))
    return pl.pallas_call(
        _mixer_kernel,
        grid=(bsz, seq // ts),
        in_specs=[
            pl.BlockSpec((1, ts, D_MODEL), lambda b, s: (b, s, 0)),
            pl.BlockSpec((1, y.shape[1], ts, y.shape[3]), lambda b, s: (b, 0, s, 0)),
            pl.BlockSpec((1, 6, D_MODEL), lambda b, s: (b, 0, 0)),
            _const_spec((1, D_MODEL)),
            _const_spec((D_MODEL, 2 * CONV_WIDTH)),
            gate_spec(0), gate_spec(1), gate_spec(2), gate_spec(3),
            _const_spec((CONV_KERNEL, CONV_WIDTH)),
            _const_spec((1, CONV_WIDTH)),
            _const_spec((1, CONV_WIDTH)),
            _const_spec((1, CONV_WIDTH)),
            _const_spec((CONV_WIDTH, D_MODEL)),
            _const_spec((SSM_WIDTH, 2 * D_MODEL)),
            _const_spec((D_MODEL, D_MODEL)),
            wf_spec,
        ],
        out_specs=[pl.BlockSpec((1, ts, D_MODEL), lambda b, s: (b, s, 0)), wf_spec],
        out_shape=[jax.ShapeDtypeStruct(x.shape, F32),
                   jax.ShapeDtypeStruct(w_ffn_in.shape, BF16)],
        scratch_shapes=[
            pltpu.VMEM((CONV_WIDTH // LANES, MIX_ROWS + CONV_HALO, LANES), F32),
            pltpu.VMEM((MIX_ROWS, CONV_WIDTH), F32),
            pltpu.VMEM((N_SIDE, MIX_ROWS, D_MODEL // 2), F32)],
        compiler_params=pltpu.CompilerParams(
            dimension_semantics=("arbitrary", "arbitrary"),
            vmem_limit_bytes=VMEM_LIMIT),
        name="mixer",
    )(x, y, mod, g, w_in_b, w_in_b, w_in_b, w_in_b, w_in_b, cw, cb, lng, lnb, proj, glu,
      wout, w_ffn_in)


def _ffn_kernel(x_ref, mod_ref, g_ref, wg_ref, wu_ref, wo_ref, fg_ref, o_ref, act_s):
    x = x_ref[0]
    h = _rms_mod(x, g_ref[...], mod_ref[0, 4:5, :], mod_ref[0, 3:4, :]).astype(BF16)
    for c0 in range(0, FFN_HIDDEN, FFN_CHUNK):
        cols = slice(c0, c0 + FFN_CHUNK)
        fg = jnp.dot(h, wg_ref[:, cols], preferred_element_type=F32)
        fu = jnp.dot(h, wu_ref[:, cols], preferred_element_type=F32)
        act_s[:, cols] = (fg * _sigmoid(fg) * fu).astype(BF16)
    upd = jnp.dot(act_s[...], wo_ref[...], preferred_element_type=F32)
    x2 = x + mod_ref[0, 5:6, :] * upd
    ms = jnp.mean(x2 * x2, axis=-1, keepdims=True)
    o_ref[0] = x2 * lax.rsqrt(ms + EPS) * fg_ref[...]


def _ffn_call(x, mod, g, w_ffn_in_b, wo, final_g, ts):
    bsz, seq, _ = x.shape
    return pl.pallas_call(
        _ffn_kernel,
        grid=(bsz, seq // ts),
        in_specs=[
            pl.BlockSpec((1, ts, D_MODEL), lambda b, s: (b, s, 0)),
            pl.BlockSpec((1, 6, D_MODEL), lambda b, s: (b, 0, 0)),
            _const_spec((1, D_MODEL)),
            _const_spec((D_MODEL, FFN_HIDDEN), (0, 0)),
            _const_spec((D_MODEL, FFN_HIDDEN), (0, 1)),
            _const_spec((FFN_HIDDEN, D_MODEL)),
            _const_spec((1, D_MODEL)),
        ],
        out_specs=pl.BlockSpec((1, ts, D_MODEL), lambda b, s: (b, s, 0)),
        out_shape=jax.ShapeDtypeStruct(x.shape, F32),
        scratch_shapes=[pltpu.VMEM((ts, FFN_HIDDEN), BF16)],
        compiler_params=pltpu.CompilerParams(
            dimension_semantics=("arbitrary", "arbitrary"),
            vmem_limit_bytes=VMEM_LIMIT),
        name="ffn",
    )(x, mod, g, w_ffn_in_b, w_ffn_in_b, wo, final_g)


def kernel(x, c, w_ada, b_ada, norm1_g, w_in, conv_w, conv_b, conv_ln_g, conv_ln_b,
           conv_proj, ssm_a_re, ssm_a_im, ssm_b_re, ssm_b_im, ssm_c_re, ssm_c_im,
           ssm_d, ssm_log_dt, ssm_glu, w_out, norm2_g, w_ffn_in, w_ffn_out, final_g):
    bsz, seq, _ = x.shape
    assert w_ada.shape[0] == 1, "single layer only"

    mod, *ops = _prep_call(c, w_ada[0], b_ada[0], ssm_a_re[0], ssm_a_im[0], ssm_b_re[0],
                           ssm_b_im[0], ssm_c_re[0], ssm_c_im[0], ssm_d[0],
                           ssm_log_dt[0])
    mod = mod.reshape(bsz, -1, D_MODEL)

    g1 = norm1_g[0].reshape(1, -1)

    u_t = _ssm_in_call(x, mod, g1, w_in[0], 128)
    y, (w_in_b, proj_b, glu_b, w_out_b, w_ffn_out_b) = _ssm_call(
        u_t, *ops, (w_in[0], conv_proj[0], ssm_glu[0], w_out[0], w_ffn_out[0]))

    x1, w_ffn_in_b = _mixer_call(
        x, y, mod, g1, w_in_b, conv_w[0], conv_b[0].reshape(1, -1),
        conv_ln_g[0].reshape(1, -1), conv_ln_b[0].reshape(1, -1),
        proj_b, glu_b, w_out_b, w_ffn_in[0], 1024)

    return _ffn_call(x1, mod, norm2_g[0].reshape(1, -1), w_ffn_in_b, w_ffn_out_b,
                     final_g.reshape(1, -1), 1024)
```

```python
import jax
import jax.numpy as jnp
from jax import lax
from jax.experimental import pallas as pl
from jax.experimental.pallas import tpu as pltpu

F32 = jnp.float32
BF16 = jnp.bfloat16

D_MODEL = 1024
CONV_WIDTH = 512
CONV_KERNEL = 31
SSM_WIDTH = 512
SSM_GROUP = 16
SSM_GROUPS = 32
SSM_STATE = 64
FFN_HIDDEN = 2816
EPS = 1e-6

CHUNK = 16
PAIR_CH = 2 * SSM_GROUP
PAIR_ST = 2 * SSM_STATE
PAIR_ROWS = CHUNK * PAIR_CH
N_PAIRS = SSM_GROUPS // 2
PAIRS_PER_STEP = 8
PREP_PAIRS = 2
CONV_HALO = 32
GATE_BLOCK = 512
LANES = 128
SUBLANES = 8
CONV_ROWS = 64
FFN_CHUNK = 256
N_SIDE = 8
MIX_ROWS = 512

V7X_VMEM_BYTES = 64 * 1024 * 1024
VMEM_LIMIT = V7X_VMEM_BYTES * 7 // 8


def _rms_mod(x, g, scale, shift):
    ms = jnp.mean(x * x, axis=-1, keepdims=True)
    return (x * lax.rsqrt(ms + EPS)) * (g * (1.0 + scale)) + shift


def _sigmoid(v):
    return jax.nn.sigmoid(v)


def _split_bf16(v):
    hi = v.astype(BF16)
    return hi, (v - hi.astype(F32)).astype(BF16)


def _cast_rows(src_refs, dst_refs):
    for src, dst in zip(src_refs, dst_refs):
        dst[...] = src[...].astype(BF16)


def _prep_kernel(c_ref, wada_ref, bada_ref, ar_ref, ai_ref, ldt_ref, cre_ref, cim_ref,
                 bre_ref, bim_ref, dl_ref, mod_ref, kt_ref, bst_ref, cst_ref, lre_ref,
                 lim_ref, cl_s):
    c = c_ref[...]
    ca = (c * _sigmoid(c)).astype(BF16)
    mod_ref[...] = jnp.dot(ca, wada_ref[...].astype(BF16),
                           preferred_element_type=F32) + bada_ref[...]
    for j in range(ar_ref.shape[0]):
        _pair_operators(j, ar_ref, ai_ref, ldt_ref, cre_ref, cim_ref, bre_ref, bim_ref,
                        dl_ref, kt_ref, bst_ref, cst_ref, lre_ref, lim_ref, cl_s)


def _pair_operators(pair, ar_ref, ai_ref, ldt_ref, cre_ref, cim_ref, bre_ref, bim_ref,
                    dl_ref, kt_ref, bst_ref, cst_ref, lre_ref, lim_ref, cl_s):
    ar, ai = ar_ref[pair], ai_ref[pair]
    dt = jnp.exp(ldt_ref[pair])
    dar, dai = dt * ar, dt * ai
    mag, cos1, sin1 = jnp.exp(dar), jnp.cos(dai), jnp.sin(dai)
    n_re, n_im = mag * cos1 - 1.0, mag * sin1
    den = ar * ar + ai * ai
    q_re = (n_re * ar + n_im * ai) / den
    q_im = (n_im * ar - n_re * ai) / den

    row = lax.broadcasted_iota(jnp.int32, (PAIR_CH, PAIR_ST), 0)
    lane = lax.broadcasted_iota(jnp.int32, (PAIR_CH, PAIR_ST), 1)
    same_group = (row // SSM_GROUP) == (lane // SSM_STATE)

    def block_diag(ref):
        v = ref[pair]
        return jnp.where(same_group, jnp.concatenate([v, v], axis=1), 0.0)

    c_re, c_im = block_diag(cre_ref), block_diag(cim_ref)
    b_re, b_im = block_diag(bre_ref), block_diag(bim_ref)
    bb_re = q_re * b_re - q_im * b_im
    bb_im = q_re * b_im + q_im * b_re

    powers = {0: (jnp.ones_like(dar), jnp.zeros_like(dar)), 1: (mag * cos1, mag * sin1)}

    def power(k):
        if k not in powers:
            (p_re, p_im), (s_re, s_im) = power(k - 1), powers[1]
            powers[k] = (p_re * s_re - p_im * s_im, p_re * s_im + p_im * s_re)
        return powers[k]

    def cmul_cat(x_re, x_im, p, conj_out):
        p_re, p_im = p
        o_re = x_re * p_re - x_im * p_im
        o_im = x_re * p_im + x_im * p_re
        return jnp.concatenate([o_re, -o_im if conj_out else o_im], axis=1)

    for t in range(CHUNK):
        rs = slice(t * PAIR_CH, (t + 1) * PAIR_CH)
        cl_s[rs, :] = cmul_cat(c_re, c_im, power(t), True)
        cst_ref[pair, rs, :] =cmul_cat(c_re, c_im, power(t + 1), True).astype(BF16)
        bst_ref[pair, rs, :] = cmul_cat(bb_re, bb_im, power(CHUNK - 1 - t),
                                        False).astype(BF16)
    lre_ref[pair], lim_ref[pair] = power(CHUNK)

    cl_hi, cl_lo = _split_bf16(cl_s[...])
    bb = jnp.concatenate([bb_re, bb_im], axis=1)
    bb = jnp.concatenate([bb, jnp.zeros((LANES - PAIR_CH, 2 * PAIR_ST), F32)], axis=0)
    bb_hi, bb_lo = _split_bf16(bb)

    def dot_nt(a, b):
        return lax.dot_general(a, b, (((1,), (1,)), ((), ())), preferred_element_type=F32)

    kcol = dot_nt(cl_hi, bb_hi) + (dot_nt(cl_hi, bb_lo) + dot_nt(cl_lo, bb_hi))

    per_tile = LANES // PAIR_CH
    ri = lax.broadcasted_iota(jnp.int32, (PAIR_ROWS, LANES), 0)
    ci = lax.broadcasted_iota(jnp.int32, (PAIR_ROWS, LANES), 1)
    for tile in range(PAIR_ROWS // LANES):
        acc = jnp.where(ri == ci + tile * LANES,
                        dl_ref[pair, :, tile * LANES:(tile + 1) * LANES], 0.0)
        for q in range(per_tile):
            tau = tile * per_tile + q
            piece = kcol if q == 0 else pltpu.roll(kcol, q * PAIR_CH, axis=1)
            if tau:
                piece = jnp.concatenate(
                    [jnp.zeros((tau * PAIR_CH, LANES), F32),
                     piece[:PAIR_ROWS - tau * PAIR_CH, :]], axis=0)
            acc = acc + piece
        kt_ref[pair, :, tile * LANES:(tile + 1) * LANES] = acc.astype(BF16)


def _prep_call(c, w_ada, b_ada, a_re, a_im, b_re, b_im, c_re, c_im, d, log_dt):
    np_, pc, ps, st = N_PAIRS, PAIR_CH, PAIR_ST, SSM_STATE
    bsz = c.shape[0]
    mod_cols = w_ada.shape[1]
    steps = np_ // PREP_PAIRS
    mod_blk = mod_cols // steps
    row = lambda a: a.reshape(np_, 1, ps)
    ldt = jnp.broadcast_to(log_dt[:, None], (SSM_GROUPS, st))
    d_lane = jnp.tile(d.reshape(np_, 1, pc), (1, 1, CHUNK))
    bt = lambda a: jnp.transpose(a, (0, 2, 1)).reshape(np_, pc, st)
    spec = lambda *shape: pl.BlockSpec((PREP_PAIRS,) + shape, lambda p: (p, 0, 0))
    col_spec = lambda rows: pl.BlockSpec((rows, mod_blk), lambda p: (0, p))
    return pl.pallas_call(
        _prep_kernel,
        grid=(steps,),
        in_specs=[pl.BlockSpec((bsz, D_MODEL), lambda p: (0, 0)),
                  col_spec(D_MODEL), col_spec(1),
                  spec(1, ps), spec(1, ps), spec(1, ps), spec(pc, st), spec(pc, st),
                  spec(pc, st), spec(pc, st), spec(1, PAIR_ROWS)],
        out_specs=[col_spec(bsz),
                   spec(PAIR_ROWS, PAIR_ROWS), spec(PAIR_ROWS, 2 * ps),
                   spec(PAIR_ROWS, 2 * ps), spec(1, ps), spec(1, ps)],
        out_shape=[jax.ShapeDtypeStruct((bsz, mod_cols), F32),
                   jax.ShapeDtypeStruct((np_, PAIR_ROWS, PAIR_ROWS), BF16),
                   jax.ShapeDtypeStruct((np_, PAIR_ROWS, 2 * ps), BF16),
                   jax.ShapeDtypeStruct((np_, PAIR_ROWS, 2 * ps), BF16),
                   jax.ShapeDtypeStruct((np_, 1, ps), F32),
                   jax.ShapeDtypeStruct((np_, 1, ps), F32)],
        scratch_shapes=[pltpu.VMEM((PAIR_ROWS, 2 * ps), F32)],
        name="prep",
    )(c, w_ada, b_ada.reshape(1, -1), row(a_re), row(a_im), row(ldt),
      c_re.reshape(np_, pc, st), c_im.reshape(np_, pc, st), bt(b_re), bt(b_im), d_lane)


def _ssm_in_kernel(x_hbm, mod_ref, g_ref, w_ref, o_ref, xbuf, sems, wt_s):
    n_chunks = o_ref.shape[-1]
    n_s = pl.num_programs(1)
    flat = pl.program_id(0) * n_s + pl.program_id(1)
    slot = flat % 2

    def gather(block, buf_slot):
        bb, c0 = block // n_s, (block % n_s) * n_chunks
        return [pltpu.make_async_copy(
            x_hbm.at[bb, pl.ds(c0, n_chunks), tau, :],
            xbuf.at[buf_slot, pl.ds(tau * n_chunks, n_chunks), :],
            sems.at[buf_slot]) for tau in range(CHUNK)]

    @pl.when(flat == 0)
    def _():
        for cp in gather(flat, slot):
            cp.start()
        wt_s[...] = w_ref[...].T.astype(BF16)

    @pl.when(flat + 1 < pl.num_programs(0) * n_s)
    def _():
        for cp in gather(flat + 1, 1 - slot):
            cp.start()

    for cp in gather(flat, slot):
        cp.wait()

    h = _rms_mod(xbuf[slot], g_ref[...], mod_ref[0, 1:2, :], mod_ref[0, 0:1, :])
    u_t = lax.dot_general(wt_s[...], h.astype(BF16), (((1,), (1,)), ((), ())),
                          preferred_element_type=F32)
    for tau in range(CHUNK):
        o_ref[0, 0, tau] = u_t[:, tau * n_chunks:(tau + 1) * n_chunks].astype(o_ref.dtype)


def _ssm_in_call(x, mod, g, w_in, chunks_per_step):
    bsz, seq, _ = x.shape
    ts = chunks_per_step * CHUNK
    ssm_col_block = 2 * CONV_WIDTH // SSM_WIDTH
    x_chunked = x.reshape(bsz, seq // CHUNK, CHUNK, D_MODEL)
    return pl.pallas_call(
        _ssm_in_kernel,
        grid=(bsz, seq // ts),
        in_specs=[
            pl.BlockSpec(memory_space=pl.ANY),
            pl.BlockSpec((1, 6, D_MODEL), lambda b, s: (b, 0, 0)),
            pl.BlockSpec((1, D_MODEL), lambda b, s: (0, 0)),
            pl.BlockSpec((D_MODEL, SSM_WIDTH), lambda b, s: (0, ssm_col_block)),
        ],
        out_specs=pl.BlockSpec((1, 1, CHUNK, SSM_WIDTH, chunks_per_step),
                               lambda b, s: (b, s, 0, 0, 0)),
        out_shape=jax.ShapeDtypeStruct(
            (bsz, seq // ts, CHUNK, SSM_WIDTH, chunks_per_step), BF16),
        scratch_shapes=[pltpu.VMEM((2, ts, D_MODEL), F32),
                        pltpu.SemaphoreType.DMA((2,)),
                        pltpu.VMEM((SSM_WIDTH, D_MODEL), BF16)],
        compiler_params=pltpu.CompilerParams(
            dimension_semantics=("arbitrary", "arbitrary"),
            vmem_limit_bytes=VMEM_LIMIT),
        name="ssm_in",
    )(x_chunked, mod, g, w_in)


def _ssm_kernel(u_ref, kt_ref, bst_ref, cst_ref, lre_ref, lim_ref, *rest):
    n_w = (len(rest) - 7) // 2
    w_in_refs, o_ref, w_out_refs = rest[:n_w], rest[n_w], rest[n_w + 1:2 * n_w + 1]
    sre, sim, xre, xim, ys, yn = rest[2 * n_w + 1:]
    n_chunks = u_ref.shape[1] * u_ref.shape[-1]
    pw = PAIR_ST

    _cast_rows(w_in_refs, w_out_refs)

    def load_pair(j):
        r = jnp.concatenate([u_ref[0, k, :, j * PAIR_CH:(j + 1) * PAIR_CH, :]
                             for k in range(u_ref.shape[1])], axis=-1)
        return r.reshape(PAIR_ROWS, n_chunks)

    for j in range(PAIRS_PER_STEP):
        s = lax.dot_general(load_pair(j), bst_ref[j], (((0,), (0,)), ((), ())),
                            preferred_element_type=F32)
        sre[:, j * pw:(j + 1) * pw] = s[:, :pw]
        sim[:, j * pw:(j + 1) * pw] = s[:, pw:]

    lre = lre_ref[0]
    lim = lim_ref[0]

    def body(c, carry):
        xr, xi = carry
        xre[pl.ds(c, 1), :] = xr
        xim[pl.ds(c, 1), :] = xi
        nr = lre * xr - lim * xi + sre[pl.ds(c, 1), :]
        ni = lre * xi + lim * xr + sim[pl.ds(c, 1), :]
        return nr, ni

    zero = jnp.zeros((1, PAIRS_PER_STEP * pw), F32)
    lax.fori_loop(0, n_chunks, body, (zero, zero), unroll=8)

    for j in range(PAIRS_PER_STEP):
        y_intra = jnp.dot(kt_ref[j], load_pair(j), preferred_element_type=F32)
        xp = jnp.concatenate([xre[:, j * pw:(j + 1) * pw],
                              xim[:, j * pw:(j + 1) * pw]], axis=1).astype(BF16)
        y_state = lax.dot_general(cst_ref[j], xp, (((1,), (1,)), ((), ())),
                                  preferred_element_type=F32)
        y = jax.nn.gelu(y_intra + y_state)
        ys[:, j * PAIR_CH:(j + 1) * PAIR_CH, :] = y.reshape(CHUNK, PAIR_CH, n_chunks)

    for k in range(yn.shape[0]):
        lanes = slice(k * LANES, (k + 1) * LANES)
        for t in range(CHUNK):
            yn[k, pl.ds(t, n_chunks, stride=CHUNK), :] = ys[t, lanes, :].T
        o_ref[0, 0, :, lanes] = yn[k].astype(o_ref.dtype)


def _ssm_call(u_t, kt, bst, cst, lam_re, lam_im, weights):
    bsz, n_blk, _, _, blk_chunks = u_t.shape
    n_chunks = n_blk * blk_chunks
    seq = n_chunks * CHUNK
    pps = PAIRS_PER_STEP
    ch = pps * PAIR_CH
    state_w = pps * PAIR_ST
    steps = N_PAIRS // pps
    lam_re = lam_re.reshape(steps, 1, state_w)
    lam_im = lam_im.reshape(steps, 1, state_w)
    n_steps = steps * bsz
    w_specs = [pl.BlockSpec((w.shape[0] // n_steps, w.shape[1]),
                            lambda p, b: (p * bsz + b, 0)) for w in weights]
    outs = pl.pallas_call(
        _ssm_kernel,
        grid=(steps, bsz),
        in_specs=[
            pl.BlockSpec((1, n_blk, CHUNK, ch, blk_chunks), lambda p, b: (b, 0, 0, p, 0)),
            pl.BlockSpec((pps, PAIR_ROWS, PAIR_ROWS), lambda p, b: (p, 0, 0)),
            pl.BlockSpec((pps, PAIR_ROWS, 2 * PAIR_ST), lambda p, b: (p, 0, 0)),
            pl.BlockSpec((pps, PAIR_ROWS, 2 * PAIR_ST), lambda p, b: (p, 0, 0)),
            pl.BlockSpec((1, 1, state_w), lambda p, b: (p, 0, 0)),
            pl.BlockSpec((1, 1, state_w), lambda p, b: (p, 0, 0)),
        ] + w_specs,
        out_specs=[pl.BlockSpec((1, 1, seq, ch), lambda p, b: (b, p, 0, 0))] + w_specs,
        out_shape=[jax.ShapeDtypeStruct((bsz, steps, seq, ch), BF16)] + [
            jax.ShapeDtypeStruct(w.shape, BF16) for w in weights],
        scratch_shapes=[pltpu.VMEM((n_chunks, state_w), F32)] * 4 + [
            pltpu.VMEM((CHUNK, ch, n_chunks), F32),
            pltpu.VMEM((ch // LANES, seq, LANES), F32)],
        compiler_params=pltpu.CompilerParams(
            dimension_semantics=("arbitrary", "arbitrary"),
            vmem_limit_bytes=VMEM_LIMIT),
        name="ssm",
    )(u_t, kt, bst, cst, lam_re, lam_im, *weights)
    return outs[0], outs[1:]


def _mixer_kernel(x_ref, y_ref, mod_ref, g_ref, wc_ref, wgc0_ref, wgc1_ref, wgs0_ref,
                  wgs1_ref, cw_ref, cb_ref, lng_ref, lnb_ref, proj_ref, glu_ref,
                  wout_ref, wf_ref, o_ref, wf_o_ref, ubuf, cbuf, sbuf):
    _cast_rows((wf_ref,), (wf_o_ref,))

    @pl.when(pl.program_id(1) == 0)
    def _():
        ubuf[:, 0:CONV_HALO, :] = jnp.zeros((CONV_WIDTH // LANES, CONV_HALO, LANES), F32)

    for r0 in range(0, x_ref.shape[1], MIX_ROWS):
        _mixer_rows(r0, x_ref, y_ref, mod_ref, g_ref, wc_ref, wgc0_ref, wgc1_ref,
                    wgs0_ref, wgs1_ref, cw_ref, cb_ref, lng_ref, lnb_ref, proj_ref,
                    glu_ref, wout_ref, o_ref, ubuf, cbuf, sbuf)


def _mixer_rows(r0, x_ref, y_ref, mod_ref, g_ref, wc_ref, wgc0_ref, wgc1_ref, wgs0_ref,
                wgs1_ref, cw_ref, cb_ref, lng_ref, lnb_ref, proj_ref, glu_ref,
                wout_ref, o_ref, ubuf, cbuf, sbuf):
    ts = MIX_ROWS
    rows = slice(r0, r0 + ts)
    halo = CONV_HALO
    half = D_MODEL // 2
    n_lt = CONV_WIDTH // LANES

    x = x_ref[0, rows, :]
    h = _rms_mod(x, g_ref[...], mod_ref[0, 1:2, :], mod_ref[0, 0:1, :]).astype(BF16)

    zc = jnp.dot(h, wc_ref[...], preferred_element_type=F32)
    u = zc[:, :CONV_WIDTH] * _sigmoid(zc[:, CONV_WIDTH:])
    for k in range(n_lt):
        ubuf[k, halo:halo + ts, :] = u[:, k * LANES:(k + 1) * LANES]

    y_in = jnp.concatenate([y_ref[0, k, rows, :] for k in range(y_ref.shape[1])], axis=1)

    def dot_h(w_ref):
        return jnp.dot(h, w_ref[...], preferred_element_type=F32)

    def dot_y(k):
        return jnp.dot(y_in, glu_ref[:, k * half:(k + 1) * half],
                       preferred_element_type=F32)

    gc0, gs0, za0, zb0, gc1, gs1, za1, zb1 = range(N_SIDE)

    def put(k, v):
        sbuf[k] = v

    dyn_zero = jnp.minimum(pl.program_id(1), 0)
    never = lax.broadcasted_iota(jnp.int32, (SUBLANES, LANES), 0) < dyn_zero

    def anchored(term, k):
        fence = sbuf[dyn_zero, k * SUBLANES:(k + 1) * SUBLANES, 0:LANES]
        head = jnp.where(never, fence, term[0:SUBLANES, :])
        return jnp.concatenate([head, term[SUBLANES:, :]], axis=0)

    side_jobs = [
        lambda: put(gc0, dot_h(wgc0_ref)), lambda: put(gs0, dot_h(wgs0_ref)),
        lambda: put(za0, dot_y(0)), lambda: put(zb0, dot_y(2)),
        lambda: put(gc1, dot_h(wgc1_ref)), lambda: put(gs1, dot_h(wgs1_ref)),
        lambda: put(za1, dot_y(1)), lambda: put(zb1, dot_y(3)),
    ]
    first = halo - (CONV_KERNEL - 1)
    n_rt = ts // CONV_ROWS
    for i in range(n_rt):
        for k in range(n_lt):
            lanes = slice(k * LANES, (k + 1) * LANES)
            acc = None
            for j in range(CONV_KERNEL):
                row0 = first + j + i * CONV_ROWS
                term = cw_ref[j:j + 1, lanes] * ubuf[k, row0:row0 + CONV_ROWS, :]
                if j == 0 and i > 0:
                    term = anchored(term, k)
                acc = term if acc is None else acc + term
            cbuf[i * CONV_ROWS:(i + 1) * CONV_ROWS, lanes] = acc + cb_ref[:, lanes]
        if (i + 1) * len(side_jobs) % n_rt == 0:
            side_jobs[(i + 1) * len(side_jobs) // n_rt - 1]()
    for k in range(n_lt):
        ubuf[k, 0:halo, :] = ubuf[k, ts:ts + halo, :]
    acc = cbuf[...]
    mu = jnp.mean(acc, axis=-1, keepdims=True)
    cen = acc - mu
    var = jnp.mean(cen * cen, axis=-1, keepdims=True)
    yn = cen * lax.rsqrt(var + EPS) * lng_ref[...] + lnb_ref[...]
    yn = yn * _sigmoid(yn)
    y_conv = jnp.dot(yn.astype(BF16), proj_ref[...], preferred_element_type=F32)

    def merge(conv_half, g_conv, g_ssm, z_a, z_b):
        y_ssm = sbuf[z_a] * _sigmoid(sbuf[z_b])
        return _sigmoid(sbuf[g_conv]) * conv_half + _sigmoid(sbuf[g_ssm]) * y_ssm

    merged = jnp.concatenate(
        [merge(y_conv[:, :half], gc0, gs0, za0, zb0),
         merge(y_conv[:, half:], gc1, gs1, za1, zb1)], axis=1)
    upd = jnp.dot(merged.astype(BF16), wout_ref[...], preferred_element_type=F32)
    o_ref[0, rows, :] = x + mod_ref[0, 2:3, :] * upd


def _const_spec(shape, index=None):
    index = (0,) * len(shape) if index is None else index
    return pl.BlockSpec(shape, lambda b, s: index, pipeline_mode=pl.Buffered(1))


def _mixer_call(x, y, mod, g, w_in_b, cw, cb, lng, lnb, proj, glu, wout, w_ffn_in, ts):
    bsz, seq, _ = x.shape
    n_s = seq // ts
    gate0 = (2 * CONV_WIDTH + SSM_WIDTH) // GATE_BLOCK
    gate_spec = lambda i: _const_spec((D_MODEL, GATE_BLOCK), (0, gate0 + i))
    wf_rows, wf_cols = w_ffn_in.shape
    wf_spec = pl.BlockSpec((wf_rows // (bsz * n_s), wf_cols), lambda b, s: (b * n_s + s, 0))
    return pl.pallas_call(
        _mixer_kernel,
        grid=(bsz, seq // ts),
        in_specs=[
            pl.BlockSpec((1, ts, D_MODEL), lambda b, s: (b, s, 0)),
            pl.BlockSpec((1, y.shape[1], ts, y.shape[3]), lambda b, s: (b, 0, s, 0)),
            pl.BlockSpec((1, 6, D_MODEL), lambda b, s: (b, 0, 0)),
            _const_spec((1, D_MODEL)),
            _const_spec((D_MODEL, 2 * CONV_WIDTH)),
            gate_spec(0), gate_spec(1), gate_spec(2), gate_spec(3),
            _const_spec((CONV_KERNEL, CONV_WIDTH)),
            _const_spec((1, CONV_WIDTH)),
            _const_spec((1, CONV_WIDTH)),
            _const_spec((1, CONV_WIDTH)),
            _const_spec((CONV_WIDTH, D_MODEL)),
            _const_spec((SSM_WIDTH, 2 * D_MODEL)),
            _const_spec((D_MODEL, D_MODEL)),
            wf_spec,
        ],
        out_specs=[pl.BlockSpec((1, ts, D_MODEL), lambda b, s: (b, s, 0)), wf_spec],
        out_shape=[jax.ShapeDtypeStruct(x.shape, F32),
                   jax.ShapeDtypeStruct(w_ffn_in.shape, BF16)],
        scratch_shapes=[
            pltpu.VMEM((CONV_WIDTH // LANES, MIX_ROWS + CONV_HALO, LANES), F32),
            pltpu.VMEM((MIX_ROWS, CONV_WIDTH), F32),
            pltpu.VMEM((N_SIDE, MIX_ROWS, D_MODEL // 2), F32)],
        compiler_params=pltpu.CompilerParams(
            dimension_semantics=("arbitrary", "arbitrary"),
            vmem_limit_bytes=VMEM_LIMIT),
        name="mixer",
    )(x, y, mod, g, w_in_b, w_in_b, w_in_b, w_in_b, w_in_b, cw, cb, lng, lnb, proj, glu,
      wout, w_ffn_in)


def _ffn_kernel(x_ref, mod_ref, g_ref, wg_ref, wu_ref, wo_ref, fg_ref, o_ref, act_s):
    x = x_ref[0]
    h = _rms_mod(x, g_ref[...], mod_ref[0, 4:5, :], mod_ref[0, 3:4, :]).astype(BF16)
    for c0 in range(0, FFN_HIDDEN, FFN_CHUNK):
        cols = slice(c0, c0 + FFN_CHUNK)
        fg = jnp.dot(h, wg_ref[:, cols], preferred_element_type=F32)
        fu = jnp.dot(h, wu_ref[:, cols], preferred_element_type=F32)
        act_s[:, cols] = (fg * _sigmoid(fg) * fu).astype(BF16)
    upd = jnp.dot(act_s[...], wo_ref[...], preferred_element_type=F32)
    x2 = x + mod_ref[0, 5:6, :] * upd
    ms = jnp.mean(x2 * x2, axis=-1, keepdims=True)
    o_ref[0] = x2 * lax.rsqrt(ms + EPS) * fg_ref[...]


def _ffn_call(x, mod, g, w_ffn_in_b, wo, final_g, ts):
    bsz, seq, _ = x.shape
    return pl.pallas_call(
        _ffn_kernel,
        grid=(bsz, seq // ts),
        in_specs=[
            pl.BlockSpec((1, ts, D_MODEL), lambda b, s: (b, s, 0)),
            pl.BlockSpec((1, 6, D_MODEL), lambda b, s: (b, 0, 0)),
            _const_spec((1, D_MODEL)),
            _const_spec((D_MODEL, FFN_HIDDEN), (0, 0)),
            _const_spec((D_MODEL, FFN_HIDDEN), (0, 1)),
            _const_spec((FFN_HIDDEN, D_MODEL)),
            _const_spec((1, D_MODEL)),
        ],
        out_specs=pl.BlockSpec((1, ts, D_MODEL), lambda b, s: (b, s, 0)),
        out_shape=jax.ShapeDtypeStruct(x.shape, F32),
        scratch_shapes=[pltpu.VMEM((ts, FFN_HIDDEN), BF16)],
        compiler_params=pltpu.CompilerParams(
            dimension_semantics=("arbitrary", "arbitrary"),
            vmem_limit_bytes=VMEM_LIMIT),
        name="ffn",
    )(x, mod, g, w_ffn_in_b, w_ffn_in_b, wo, final_g)


def kernel(x, c, w_ada, b_ada, norm1_g, w_in, conv_w, conv_b, conv_ln_g, conv_ln_b,
           conv_proj, ssm_a_re, ssm_a_im, ssm_b_re, ssm_b_im, ssm_c_re, ssm_c_im,
           ssm_d, ssm_log_dt, ssm_glu, w_out, norm2_g, w_ffn_in, w_ffn_out, final_g):
    bsz, seq, _ = x.shape
    assert w_ada.shape[0] == 1, "single layer only"

    mod, *ops = _prep_call(c, w_ada[0], b_ada[0], ssm_a_re[0], ssm_a_im[0], ssm_b_re[0],
                           ssm_b_im[0], ssm_c_re[0], ssm_c_im[0], ssm_d[0],
                           ssm_log_dt[0])
    mod = mod.reshape(bsz, -1, D_MODEL)

    g1 = norm1_g[0].reshape(1, -1)

    u_t = _ssm_in_call(x, mod, g1, w_in[0], 128)
    y, (w_in_b, proj_b, glu_b, w_out_b, w_ffn_out_b) = _ssm_call(
        u_t, *ops, (w_in[0], conv_proj[0], ssm_glu[0], w_out[0], w_ffn_out[0]))

    x1, w_ffn_in_b = _mixer_call(
        x, y, mod, g1, w_in_b, conv_w[0], conv_b[0].reshape(1, -1),
        conv_ln_g[0].reshape(1, -1), conv_ln_b[0].reshape(1, -1),
        proj_b, glu_b, w_out_b, w_ffn_in[0], 1024)

    return _ffn_call(x1, mod, norm2_g[0].reshape(1, -1), w_ffn_in_b, w_ffn_out_b,
                     final_g.reshape(1, -1), 1024)
```

```python
import jax
import jax.numpy as jnp
from jax import lax
from jax.experimental import pallas as pl
from jax.experimental.pallas import tpu as pltpu

F32 = jnp.float32
BF16 = jnp.bfloat16

D_MODEL = 1024
CONV_WIDTH = 512
CONV_KERNEL = 31
SSM_WIDTH = 512
SSM_GROUP = 16
SSM_GROUPS = 32
SSM_STATE = 64
FFN_HIDDEN = 2816
EPS = 1e-6

CHUNK = 16
PAIR_CH = 2 * SSM_GROUP
PAIR_ST = 2 * SSM_STATE
PAIR_ROWS = CHUNK * PAIR_CH
N_PAIRS = SSM_GROUPS // 2
PAIRS_PER_STEP = 8
PREP_PAIRS = 2
CONV_HALO = 32
GATE_BLOCK = 512
LANES = 128
SUBLANES = 8
CONV_ROWS = 64
FFN_CHUNK = 256
N_SIDE = 8
MIX_ROWS = 512

V7X_VMEM_BYTES = 64 * 1024 * 1024
VMEM_LIMIT = V7X_VMEM_BYTES * 7 // 8


def _rms_mod(x, g, scale, shift):
    ms = jnp.mean(x * x, axis=-1, keepdims=True)
    return (x * lax.rsqrt(ms + EPS)) * (g * (1.0 + scale)) + shift


def _sigmoid(v):
    return jax.nn.sigmoid(v)


def _split_bf16(v):
    hi = v.astype(BF16)
    return hi, (v - hi.astype(F32)).astype(BF16)


def _cast_rows(src_refs, dst_refs):
    for src, dst in zip(src_refs, dst_refs):
        dst[...] = src[...].astype(BF16)


def _prep_kernel(c_ref, wada_ref, bada_ref, ar_ref, ai_ref, ldt_ref, cre_ref, cim_ref,
                 bre_ref, bim_ref, dl_ref, mod_ref, kt_ref, bst_ref, cst_ref, lre_ref,
                 lim_ref, cl_s):
    c = c_ref[...]
    ca = (c * _sigmoid(c)).astype(BF16)
    mod_ref[...] = jnp.dot(ca, wada_ref[...].astype(BF16),
                           preferred_element_type=F32) + bada_ref[...]
    for j in range(ar_ref.shape[0]):
        _pair_operators(j, ar_ref, ai_ref, ldt_ref, cre_ref, cim_ref, bre_ref, bim_ref,
                        dl_ref, kt_ref, bst_ref, cst_ref, lre_ref, lim_ref, cl_s)


def _pair_operators(pair, ar_ref, ai_ref, ldt_ref, cre_ref, cim_ref, bre_ref, bim_ref,
                    dl_ref, kt_ref, bst_ref, cst_ref, lre_ref, lim_ref, cl_s):
    ar, ai = ar_ref[pair], ai_ref[pair]
    dt = jnp.exp(ldt_ref[pair])
    dar, dai = dt * ar, dt * ai
    mag, cos1, sin1 = jnp.exp(dar), jnp.cos(dai), jnp.sin(dai)
    n_re, n_im = mag * cos1 - 1.0, mag * sin1
    den = ar * ar + ai * ai
    q_re = (n_re * ar + n_im * ai) / den
    q_im = (n_im * ar - n_re * ai) / den

    row = lax.broadcasted_iota(jnp.int32, (PAIR_CH, PAIR_ST), 0)
    lane = lax.broadcasted_iota(jnp.int32, (PAIR_CH, PAIR_ST), 1)
    same_group = (row // SSM_GROUP) == (lane // SSM_STATE)

    def block_diag(ref):
        v = ref[pair]
        return jnp.where(same_group, jnp.concatenate([v, v], axis=1), 0.0)

    c_re, c_im = block_diag(cre_ref), block_diag(cim_ref)
    b_re, b_im = block_diag(bre_ref), block_diag(bim_ref)
    bb_re = q_re * b_re - q_im * b_im
    bb_im = q_re * b_im + q_im * b_re

    powers = {0: (jnp.ones_like(dar), jnp.zeros_like(dar)), 1: (mag * cos1, mag * sin1)}

    def power(k):
        if k not in powers:
            (p_re, p_im), (s_re, s_im) = power(k - 1), powers[1]
            powers[k] = (p_re * s_re - p_im * s_im, p_re * s_im + p_im * s_re)
        return powers[k]

    def cmul_cat(x_re, x_im, p, conj_out):
        p_re, p_im = p
        o_re = x_re * p_re - x_im * p_im
        o_im = x_re * p_im + x_im * p_re
        return jnp.concatenate([o_re, -o_im if conj_out else o_im], axis=1)

    for t in range(CHUNK):
        rs = slice(t * PAIR_CH, (t + 1) * PAIR_CH)
        cl_s[rs, :] = cmul_cat(c_re, c_im, power(t), True)
        cst_ref[pair, rs, :] =cmul_cat(c_re, c_im, power(t + 1), True).astype(BF16)
        bst_ref[pair, rs, :] = cmul_cat(bb_re, bb_im, power(CHUNK - 1 - t),
                                        False).astype(BF16)
    lre_ref[pair], lim_ref[pair] = power(CHUNK)

    cl_hi, cl_lo = _split_bf16(cl_s[...])
    bb = jnp.concatenate([bb_re, bb_im], axis=1)
    bb = jnp.concatenate([bb, jnp.zeros((LANES - PAIR_CH, 2 * PAIR_ST), F32)], axis=0)
    bb_hi, bb_lo = _split_bf16(bb)

    def dot_nt(a, b):
        return lax.dot_general(a, b, (((1,), (1,)), ((), ())), preferred_element_type=F32)

    kcol = dot_nt(cl_hi, bb_hi) + (dot_nt(cl_hi, bb_lo) + dot_nt(cl_lo, bb_hi))

    per_tile = LANES // PAIR_CH
    ri = lax.broadcasted_iota(jnp.int32, (PAIR_ROWS, LANES), 0)
    ci = lax.broadcasted_iota(jnp.int32, (PAIR_ROWS, LANES), 1)
    for tile in range(PAIR_ROWS // LANES):
        acc = jnp.where(ri == ci + tile * LANES,
                        dl_ref[pair, :, tile * LANES:(tile + 1) * LANES], 0.0)
        for q in range(per_tile):
            tau = tile * per_tile + q
            piece = kcol if q == 0 else pltpu.roll(kcol, q * PAIR_CH, axis=1)
            if tau:
                piece = jnp.concatenate(
                    [jnp.zeros((tau * PAIR_CH, LANES), F32),
                     piece[:PAIR_ROWS - tau * PAIR_CH, :]], axis=0)
            acc = acc + piece
        kt_ref[pair, :, tile * LANES:(tile + 1) * LANES] = acc.astype(BF16)


def _prep_call(c, w_ada, b_ada, a_re, a_im, b_re, b_im, c_re, c_im, d, log_dt):
    np_, pc, ps, st = N_PAIRS, PAIR_CH, PAIR_ST, SSM_STATE
    bsz = c.shape[0]
    mod_cols = w_ada.shape[1]
    steps = np_ // PREP_PAIRS
    mod_blk = mod_cols // steps
    row = lambda a: a.reshape(np_, 1, ps)
    ldt = jnp.broadcast_to(log_dt[:, None], (SSM_GROUPS, st))
    d_lane = jnp.tile(d.reshape(np_, 1, pc), (1, 1, CHUNK))
    bt = lambda a: jnp.transpose(a, (0, 2, 1)).reshape(np_, pc, st)
    spec = lambda *shape: pl.BlockSpec((PREP_PAIRS,) + shape, lambda p: (p, 0, 0))
    col_spec = lambda rows: pl.BlockSpec((rows, mod_blk), lambda p: (0, p))
    return pl.pallas_call(
        _prep_kernel,
        grid=(steps,),
        in_specs=[pl.BlockSpec((bsz, D_MODEL), lambda p: (0, 0)),
                  col_spec(D_MODEL), col_spec(1),
                  spec(1, ps), spec(1, ps), spec(1, ps), spec(pc, st), spec(pc, st),
                  spec(pc, st), spec(pc, st), spec(1, PAIR_ROWS)],
        out_specs=[col_spec(bsz),
                   spec(PAIR_ROWS, PAIR_ROWS), spec(PAIR_ROWS, 2 * ps),
                   spec(PAIR_ROWS, 2 * ps), spec(1, ps), spec(1, ps)],
        out_shape=[jax.ShapeDtypeStruct((bsz, mod_cols), F32),
                   jax.ShapeDtypeStruct((np_, PAIR_ROWS, PAIR_ROWS), BF16),
                   jax.ShapeDtypeStruct((np_, PAIR_ROWS, 2 * ps), BF16),
                   jax.ShapeDtypeStruct((np_, PAIR_ROWS, 2 * ps), BF16),
                   jax.ShapeDtypeStruct((np_, 1, ps), F32),
                   jax.ShapeDtypeStruct((np_, 1, ps), F32)],
        scratch_shapes=[pltpu.VMEM((PAIR_ROWS, 2 * ps), F32)],
        name="prep",
    )(c, w_ada, b_ada.reshape(1, -1), row(a_re), row(a_im), row(ldt),
      c_re.reshape(np_, pc, st), c_im.reshape(np_, pc, st), bt(b_re), bt(b_im), d_lane)


def _ssm_in_kernel(x_hbm, mod_ref, g_ref, w_ref, *rest):
    n_w = (len(rest) - 4) // 2
    o_ref = rest[n_w]
    xbuf, sems, wt_s = rest[2 * n_w + 1:]
    _cast_rows(rest[:n_w], rest[n_w + 1:2 * n_w + 1])
    n_chunks = o_ref.shape[-1]
    n_s = pl.num_programs(1)
    flat = pl.program_id(0) * n_s + pl.program_id(1)
    slot = flat % 2

    def gather(block, buf_slot):
        bb, c0 = block // n_s, (block % n_s) * n_chunks
        return [pltpu.make_async_copy(
            x_hbm.at[bb, pl.ds(c0, n_chunks), tau, :],
            xbuf.at[buf_slot, pl.ds(tau * n_chunks, n_chunks), :],
            sems.at[buf_slot]) for tau in range(CHUNK)]

    @pl.when(flat == 0)
    def _():
        for cp in gather(flat, slot):
            cp.start()
        wt_s[...] = w_ref[...].T.astype(BF16)

    @pl.when(flat + 1 < pl.num_programs(0) * n_s)
    def _():
        for cp in gather(flat + 1, 1 - slot):
            cp.start()

    for cp in gather(flat, slot):
        cp.wait()

    h = _rms_mod(xbuf[slot], g_ref[...], mod_ref[0, 1:2, :], mod_ref[0, 0:1, :])
    u_t = lax.dot_general(wt_s[...], h.astype(BF16), (((1,), (1,)), ((), ())),
                          preferred_element_type=F32)
    for tau in range(CHUNK):
        o_ref[0, tau] = u_t[:, tau * n_chunks:(tau + 1) * n_chunks].astype(o_ref.dtype)


def _ssm_in_call(x, mod, g, w_in, chunks_per_step, weights):
    bsz, seq, _ = x.shape
    ts = chunks_per_step * CHUNK
    n_s = seq // ts
    ssm_col_block = 2 * CONV_WIDTH // SSM_WIDTH
    x_chunked = x.reshape(bsz, seq // CHUNK, CHUNK, D_MODEL)
    w_specs = [pl.BlockSpec((w.shape[0] // (bsz * n_s), w.shape[1]),
                            lambda b, s: (b * n_s + s, 0)) for w in weights]
    outs = pl.pallas_call(
        _ssm_in_kernel,
        grid=(bsz, seq // ts),
        in_specs=[
            pl.BlockSpec(memory_space=pl.ANY),
            pl.BlockSpec((1, 6, D_MODEL), lambda b, s: (b, 0, 0)),
            pl.BlockSpec((1, D_MODEL), lambda b, s: (0, 0)),
            pl.BlockSpec((D_MODEL, SSM_WIDTH), lambda b, s: (0, ssm_col_block)),
        ] + w_specs,
        out_specs=[pl.BlockSpec((1, CHUNK, SSM_WIDTH, chunks_per_step),
                                lambda b, s: (b, 0, 0, s))] + w_specs,
        out_shape=[jax.ShapeDtypeStruct((bsz, CHUNK, SSM_WIDTH, seq // CHUNK), BF16)] + [
            jax.ShapeDtypeStruct(w.shape, BF16) for w in weights],
        scratch_shapes=[pltpu.VMEM((2, ts, D_MODEL), F32),
                        pltpu.SemaphoreType.DMA((2,)),
                        pltpu.VMEM((SSM_WIDTH, D_MODEL), BF16)],
        compiler_params=pltpu.CompilerParams(
            dimension_semantics=("arbitrary", "arbitrary"),
            vmem_limit_bytes=VMEM_LIMIT),
        name="ssm_in",
    )(x_chunked, mod, g, w_in, *weights)
    return outs[0], outs[1:]


def _ssm_kernel(u_ref, kt_ref, bst_ref, cst_ref, lre_ref, lim_ref, *rest):
    n_w = (len(rest) - 7) // 2
    w_in_refs, o_ref, w_out_refs = rest[:n_w], rest[n_w], rest[n_w + 1:2 * n_w + 1]
    sre, sim, xre, xim, ys, yn = rest[2 * n_w + 1:]
    n_chunks = u_ref.shape[-1]
    pw = PAIR_ST

    _cast_rows(w_in_refs, w_out_refs)

    def load_pair(j):
        r = u_ref[0, :, j * PAIR_CH:(j + 1) * PAIR_CH, :]
        return r.reshape(PAIR_ROWS, n_chunks)

    for j in range(PAIRS_PER_STEP):
        s = lax.dot_general(load_pair(j), bst_ref[j], (((0,), (0,)), ((), ())),
                            preferred_element_type=F32)
        sre[:, j * pw:(j + 1) * pw] = s[:, :pw]
        sim[:, j * pw:(j + 1) * pw] = s[:, pw:]

    lre = lre_ref[0]
    lim = lim_ref[0]

    def body(c, carry):
        xr, xi = carry
        xre[pl.ds(c, 1), :] = xr
        xim[pl.ds(c, 1), :] = xi
        nr = lre * xr - lim * xi + sre[pl.ds(c, 1), :]
        ni = lre * xi + lim * xr + sim[pl.ds(c, 1), :]
        return nr, ni

    zero = jnp.zeros((1, PAIRS_PER_STEP * pw), F32)
    lax.fori_loop(0, n_chunks, body, (zero, zero), unroll=8)

    for j in range(PAIRS_PER_STEP):
        y_intra = jnp.dot(kt_ref[j], load_pair(j), preferred_element_type=F32)
        xp = jnp.concatenate([xre[:, j * pw:(j + 1) * pw],
                              xim[:, j * pw:(j + 1) * pw]], axis=1).astype(BF16)
        y_state = lax.dot_general(cst_ref[j], xp, (((1,), (1,)), ((), ())),
                                  preferred_element_type=F32)
        y = jax.nn.gelu(y_intra + y_state)
        ys[:, j * PAIR_CH:(j + 1) * PAIR_CH, :] = y.reshape(CHUNK, PAIR_CH, n_chunks)

    for k in range(yn.shape[0]):
        lanes = slice(k * LANES, (k + 1) * LANES)
        for t in range(CHUNK):
            yn[k, pl.ds(t, n_chunks, stride=CHUNK), :] = ys[t, lanes, :].T
        o_ref[0, 0, :, lanes] = yn[k].astype(o_ref.dtype)


def _ssm_call(u_t, kt, bst, cst, lam_re, lam_im, weights):
    bsz, _, _, n_chunks = u_t.shape
    seq = n_chunks * CHUNK
    pps = PAIRS_PER_STEP
    ch = pps * PAIR_CH
    state_w = pps * PAIR_ST
    steps = N_PAIRS // pps
    lam_re = lam_re.reshape(steps, 1, state_w)
    lam_im = lam_im.reshape(steps, 1, state_w)
    n_steps = steps * bsz
    w_specs = [pl.BlockSpec((w.shape[0] // n_steps, w.shape[1]),
                            lambda p, b: (p * bsz + b, 0)) for w in weights]
    outs = pl.pallas_call(
        _ssm_kernel,
        grid=(steps, bsz),
        in_specs=[
            pl.BlockSpec((1, CHUNK, ch, n_chunks), lambda p, b: (b, 0, p, 0)),
            pl.BlockSpec((pps, PAIR_ROWS, PAIR_ROWS), lambda p, b: (p, 0, 0)),
            pl.BlockSpec((pps, PAIR_ROWS, 2 * PAIR_ST), lambda p, b: (p, 0, 0)),
            pl.BlockSpec((pps, PAIR_ROWS, 2 * PAIR_ST), lambda p, b: (p, 0, 0)),
            pl.BlockSpec((1, 1, state_w), lambda p, b: (p, 0, 0)),
            pl.BlockSpec((1, 1, state_w), lambda p, b: (p, 0, 0)),
        ] + w_specs,
        out_specs=[pl.BlockSpec((1, 1, seq, ch), lambda p, b: (b, p, 0, 0))] + w_specs,
        out_shape=[jax.ShapeDtypeStruct((bsz, steps, seq, ch), BF16)] + [
            jax.ShapeDtypeStruct(w.shape, BF16) for w in weights],
        scratch_shapes=[pltpu.VMEM((n_chunks, state_w), F32)] * 4 + [
            pltpu.VMEM((CHUNK, ch, n_chunks), F32),
            pltpu.VMEM((ch // LANES, seq, LANES), F32)],
        compiler_params=pltpu.CompilerParams(
            dimension_semantics=("arbitrary", "arbitrary"),
            vmem_limit_bytes=VMEM_LIMIT),
        name="ssm",
    )(u_t, kt, bst, cst, lam_re, lam_im, *weights)
    return outs[0], outs[1:]


def _mixer_kernel(x_ref, y_ref, mod_ref, g_ref, wc_ref, wgc0_ref, wgc1_ref, wgs0_ref,
                  wgs1_ref, cw_ref, cb_ref, lng_ref, lnb_ref, proj_ref, glu_ref,
                  wout_ref, wf_ref, o_ref, wf_o_ref, ubuf, cbuf, sbuf):
    _cast_rows((wf_ref,), (wf_o_ref,))

    @pl.when(pl.program_id(1) == 0)
    def _():
        ubuf[:, 0:CONV_HALO, :] = jnp.zeros((CONV_WIDTH // LANES, CONV_HALO, LANES), F32)

    for r0 in range(0, x_ref.shape[1], MIX_ROWS):
        _mixer_rows(r0, x_ref, y_ref, mod_ref, g_ref, wc_ref, wgc0_ref, wgc1_ref,
                    wgs0_ref, wgs1_ref, cw_ref, cb_ref, lng_ref, lnb_ref, proj_ref,
                    glu_ref, wout_ref, o_ref, ubuf, cbuf, sbuf)


def _mixer_rows(r0, x_ref, y_ref, mod_ref, g_ref, wc_ref, wgc0_ref, wgc1_ref, wgs0_ref,
                wgs1_ref, cw_ref, cb_ref, lng_ref, lnb_ref, proj_ref, glu_ref,
                wout_ref, o_ref, ubuf, cbuf, sbuf):
    ts = MIX_ROWS
    rows = slice(r0, r0 + ts)
    halo = CONV_HALO
    half = D_MODEL // 2
    n_lt = CONV_WIDTH // LANES

    x = x_ref[0, rows, :]
    h = _rms_mod(x, g_ref[...], mod_ref[0, 1:2, :], mod_ref[0, 0:1, :]).astype(BF16)

    zc = jnp.dot(h, wc_ref[...], preferred_element_type=F32)
    u = zc[:, :CONV_WIDTH] * _sigmoid(zc[:, CONV_WIDTH:])
    for k in range(n_lt):
        ubuf[k, halo:halo + ts, :] = u[:, k * LANES:(k + 1) * LANES]

    y_in = jnp.concatenate([y_ref[0, k, rows, :] for k in range(y_ref.shape[1])], axis=1)

    def dot_h(w_ref):
        return jnp.dot(h, w_ref[...], preferred_element_type=F32)

    def dot_y(k):
        return jnp.dot(y_in, glu_ref[:, k * half:(k + 1) * half],
                       preferred_element_type=F32)

    gc0, gs0, za0, zb0, gc1, gs1, za1, zb1 = range(N_SIDE)

    def put(k, v):
        sbuf[k] = v

    dyn_zero = jnp.minimum(pl.program_id(1), 0)
    never = lax.broadcasted_iota(jnp.int32, (SUBLANES, LANES), 0) < dyn_zero

    def anchored(term, k):
        fence = sbuf[dyn_zero, k * SUBLANES:(k + 1) * SUBLANES, 0:LANES]
        head = jnp.where(never, fence, term[0:SUBLANES, :])
        return jnp.concatenate([head, term[SUBLANES:, :]], axis=0)

    side_jobs = [
        lambda: put(gc0, dot_h(wgc0_ref)), lambda: put(gs0, dot_h(wgs0_ref)),
        lambda: put(za0, dot_y(0)), lambda: put(zb0, dot_y(2)),
        lambda: put(gc1, dot_h(wgc1_ref)), lambda: put(gs1, dot_h(wgs1_ref)),
        lambda: put(za1, dot_y(1)), lambda: put(zb1, dot_y(3)),
    ]
    first = halo - (CONV_KERNEL - 1)
    n_rt = ts // CONV_ROWS
    for i in range(n_rt):
        for k in range(n_lt):
            lanes = slice(k * LANES, (k + 1) * LANES)
            acc = None
            for j in range(CONV_KERNEL):
                row0 = first + j + i * CONV_ROWS
                term = cw_ref[j:j + 1, lanes] * ubuf[k, row0:row0 + CONV_ROWS, :]
                if j == 0 and i > 0:
                    term = anchored(term, k)
                acc = term if acc is None else acc + term
            cbuf[i * CONV_ROWS:(i + 1) * CONV_ROWS, lanes] = acc + cb_ref[:, lanes]
        if (i + 1) * len(side_jobs) % n_rt == 0:
            side_jobs[(i + 1) * len(side_jobs) // n_rt - 1]()
    for k in range(n_lt):
        ubuf[k, 0:halo, :] = ubuf[k, ts:ts + halo, :]
    acc = cbuf[...]
    mu = jnp.mean(acc, axis=-1, keepdims=True)
    cen = acc - mu
    var = jnp.mean(cen * cen, axis=-1, keepdims=True)
    yn = cen * lax.rsqrt(var + EPS) * lng_ref[...] + lnb_ref[...]
    yn = yn * _sigmoid(yn)
    y_conv = jnp.dot(yn.astype(BF16), proj_ref[...], preferred_element_type=F32)

    def merge(conv_half, g_conv, g_ssm, z_a, z_b):
        y_ssm = sbuf[z_a] * _sigmoid(sbuf[z_b])
        return _sigmoid(sbuf[g_conv]) * conv_half + _sigmoid(sbuf[g_ssm]) * y_ssm

    merged = jnp.concatenate(
        [merge(y_conv[:, :half], gc0, gs0, za0, zb0),
         merge(y_conv[:, half:], gc1, gs1, za1, zb1)], axis=1)
    upd = jnp.dot(merged.astype(BF16), wout_ref[...], preferred_element_type=F32)
    o_ref[0, rows, :] = x + mod_ref[0, 2:3, :] * upd


def _const_spec(shape, index=None):
    index = (0,) * len(shape) if index is None else index
    return pl.BlockSpec(shape, lambda b, s: index, pipeline_mode=pl.Buffered(1))


def _mixer_call(x, y, mod, g, w_in_b, cw, cb, lng, lnb, proj, glu, wout, w_ffn_in, ts):
    bsz, seq, _ = x.shape
    n_s = seq // ts
    gate0 = (2 * CONV_WIDTH + SSM_WIDTH) // GATE_BLOCK
    gate_spec = lambda i: _const_spec((D_MODEL, GATE_BLOCK), (0, gate0 + i))
    wf_rows, wf_cols = w_ffn_in.shape
    wf_spec = pl.BlockSpec((wf_rows // (bsz * n_s), wf_cols), lambda b, s: (b * n_s + s, 0))
    return pl.pallas_call(
        _mixer_kernel,
        grid=(bsz, seq // ts),
        in_specs=[
            pl.BlockSpec((1, ts, D_MODEL), lambda b, s: (b, s, 0)),
            pl.BlockSpec((1, y.shape[1], ts, y.shape[3]), lambda b, s: (b, 0, s, 0)),
            pl.BlockSpec((1, 6, D_MODEL), lambda b, s: (b, 0, 0)),
            _const_spec((1, D_MODEL)),
            _const_spec((D_MODEL, 2 * CONV_WIDTH)),
            gate_spec(0), gate_spec(1), gate_spec(2), gate_spec(3),
            _const_spec((CONV_KERNEL, CONV_WIDTH)),
            _const_spec((1, CONV_WIDTH)),
            _const_spec((1, CONV_WIDTH)),
            _const_spec((1, CONV_WIDTH)),
            _const_spec((CONV_WIDTH, D_MODEL)),
            _const_spec((SSM_WIDTH, 2 * D_MODEL)),
            _const_spec((D_MODEL, D_MODEL)),
            wf_spec,
        ],
        out_specs=[pl.BlockSpec((1, ts, D_MODEL), lambda b, s: (b, s, 0)), wf_spec],
        out_shape=[jax.ShapeDtypeStruct(x.shape, F32),
                   jax.ShapeDtypeStruct(w_ffn_in.shape, BF16)],
        scratch_shapes=[
            pltpu.VMEM((CONV_WIDTH // LANES, MIX_ROWS + CONV_HALO, LANES), F32),
            pltpu.VMEM((MIX_ROWS, CONV_WIDTH), F32),
            pltpu.VMEM((N_SIDE, MIX_ROWS, D_MODEL // 2), F32)],
        compiler_params=pltpu.CompilerParams(
            dimension_semantics=("arbitrary", "arbitrary"),
            vmem_limit_bytes=VMEM_LIMIT),
        name="mixer",
    )(x, y, mod, g, w_in_b, w_in_b, w_in_b, w_in_b, w_in_b, cw, cb, lng, lnb, proj, glu,
      wout, w_ffn_in)


def _ffn_kernel(x_ref, mod_ref, g_ref, wg_ref, wu_ref, wo_ref, fg_ref, o_ref, act_s):
    x = x_ref[0]
    h = _rms_mod(x, g_ref[...], mod_ref[0, 4:5, :], mod_ref[0, 3:4, :]).astype(BF16)
    for c0 in range(0, FFN_HIDDEN, FFN_CHUNK):
        cols = slice(c0, c0 + FFN_CHUNK)
        fg = jnp.dot(h, wg_ref[:, cols], preferred_element_type=F32)
        fu = jnp.dot(h, wu_ref[:, cols], preferred_element_type=F32)
        act_s[:, cols] = (fg * _sigmoid(fg) * fu).astype(BF16)
    upd = jnp.dot(act_s[...], wo_ref[...], preferred_element_type=F32)
    x2 = x + mod_ref[0, 5:6, :] * upd
    ms = jnp.mean(x2 * x2, axis=-1, keepdims=True)
    o_ref[0] = x2 * lax.rsqrt(ms + EPS) * fg_ref[...]


def _ffn_call(x, mod, g, w_ffn_in_b, wo, final_g, ts):
    bsz, seq, _ = x.shape
    return pl.pallas_call(
        _ffn_kernel,
        grid=(bsz, seq // ts),
        in_specs=[
            pl.BlockSpec((1, ts, D_MODEL), lambda b, s: (b, s, 0)),
            pl.BlockSpec((1, 6, D_MODEL), lambda b, s: (b, 0, 0)),
            _const_spec((1, D_MODEL)),
            _const_spec((D_MODEL, FFN_HIDDEN), (0, 0)),
            _const_spec((D_MODEL, FFN_HIDDEN), (0, 1)),
            _const_spec((FFN_HIDDEN, D_MODEL)),
            _const_spec((1, D_MODEL)),
        ],
        out_specs=pl.BlockSpec((1, ts, D_MODEL), lambda b, s: (b, s, 0)),
        out_shape=jax.ShapeDtypeStruct(x.shape, F32),
        scratch_shapes=[pltpu.VMEM((ts, FFN_HIDDEN), BF16)],
        compiler_params=pltpu.CompilerParams(
            dimension_semantics=("arbitrary", "arbitrary"),
            vmem_limit_bytes=VMEM_LIMIT),
        name="ffn",
    )(x, mod, g, w_ffn_in_b, w_ffn_in_b, wo, final_g)


def kernel(x, c, w_ada, b_ada, norm1_g, w_in, conv_w, conv_b, conv_ln_g, conv_ln_b,
           conv_proj, ssm_a_re, ssm_a_im, ssm_b_re, ssm_b_im, ssm_c_re, ssm_c_im,
           ssm_d, ssm_log_dt, ssm_glu, w_out, norm2_g, w_ffn_in, w_ffn_out, final_g):
    bsz, seq, _ = x.shape
    assert w_ada.shape[0] == 1, "single layer only"

    mod, *ops = _prep_call(c, w_ada[0], b_ada[0], ssm_a_re[0], ssm_a_im[0], ssm_b_re[0],
                           ssm_b_im[0], ssm_c_re[0], ssm_c_im[0], ssm_d[0],
                           ssm_log_dt[0])
    mod = mod.reshape(bsz, -1, D_MODEL)

    g1 = norm1_g[0].reshape(1, -1)

    u_t, (w_in_b, proj_b, glu_b, w_out_b, w_ffn_out_b) = _ssm_in_call(
        x, mod, g1, w_in[0], 128,
        (w_in[0], conv_proj[0], ssm_glu[0], w_out[0], w_ffn_out[0]))
    y, _ = _ssm_call(u_t, *ops, ())

    x1, w_ffn_in_b = _mixer_call(
        x, y, mod, g1, w_in_b, conv_w[0], conv_b[0].reshape(1, -1),
        conv_ln_g[0].reshape(1, -1), conv_ln_b[0].reshape(1, -1),
        proj_b, glu_b, w_out_b, w_ffn_in[0], 1024)

    return _ffn_call(x1, mod, norm2_g[0].reshape(1, -1), w_ffn_in_b, w_ffn_out_b,
                     final_g.reshape(1, -1), 1024)
```

```python
import jax
import jax.numpy as jnp
from jax import lax
from jax.experimental import pallas as pl
from jax.experimental.pallas import tpu as pltpu

F32 = jnp.float32
BF16 = jnp.bfloat16

D_MODEL = 1024
CONV_WIDTH = 512
CONV_KERNEL = 31
SSM_WIDTH = 512
SSM_GROUP = 16
SSM_GROUPS = 32
SSM_STATE = 64
FFN_HIDDEN = 2816
EPS = 1e-6

CHUNK = 16
PAIR_CH = 2 * SSM_GROUP
PAIR_ST = 2 * SSM_STATE
PAIR_ROWS = CHUNK * PAIR_CH
N_PAIRS = SSM_GROUPS // 2
PAIRS_PER_STEP = 8
PREP_PAIRS = 2
CONV_HALO = 32
GATE_BLOCK = 512
LANES = 128
SUBLANES = 8
CONV_ROWS = 64
FFN_CHUNK = 256
N_SIDE = 8
MIX_ROWS = 512

V7X_VMEM_BYTES = 64 * 1024 * 1024
VMEM_LIMIT = V7X_VMEM_BYTES * 7 // 8


def _rms_mod(x, g, scale, shift):
    ms = jnp.mean(x * x, axis=-1, keepdims=True)
    return (x * lax.rsqrt(ms + EPS)) * (g * (1.0 + scale)) + shift


def _sigmoid(v):
    return jax.nn.sigmoid(v)


def _split_bf16(v):
    hi = v.astype(BF16)
    return hi, (v - hi.astype(F32)).astype(BF16)


def _cast_rows(src_refs, dst_refs):
    for src, dst in zip(src_refs, dst_refs):
        dst[...] = src[...].astype(BF16)


def _prep_kernel(c_ref, wada_ref, bada_ref, ar_ref, ai_ref, ldt_ref, cre_ref, cim_ref,
                 bre_ref, bim_ref, dl_ref, mod_ref, kt_ref, bst_ref, cst_ref, lre_ref,
                 lim_ref, cl_s):
    c = c_ref[...]
    ca = (c * _sigmoid(c)).astype(BF16)
    mod_ref[...] = jnp.dot(ca, wada_ref[...].astype(BF16),
                           preferred_element_type=F32) + bada_ref[...]
    for j in range(ar_ref.shape[0]):
        _pair_operators(j, ar_ref, ai_ref, ldt_ref, cre_ref, cim_ref, bre_ref, bim_ref,
                        dl_ref, kt_ref, bst_ref, cst_ref, lre_ref, lim_ref, cl_s)


def _pair_operators(pair, ar_ref, ai_ref, ldt_ref, cre_ref, cim_ref, bre_ref, bim_ref,
                    dl_ref, kt_ref, bst_ref, cst_ref, lre_ref, lim_ref, cl_s):
    ar, ai = ar_ref[pair], ai_ref[pair]
    dt = jnp.exp(ldt_ref[pair])
    dar, dai = dt * ar, dt * ai
    mag, cos1, sin1 = jnp.exp(dar), jnp.cos(dai), jnp.sin(dai)
    n_re, n_im = mag * cos1 - 1.0, mag * sin1
    den = ar * ar + ai * ai
    q_re = (n_re * ar + n_im * ai) / den
    q_im = (n_im * ar - n_re * ai) / den

    row = lax.broadcasted_iota(jnp.int32, (PAIR_CH, PAIR_ST), 0)
    lane = lax.broadcasted_iota(jnp.int32, (PAIR_CH, PAIR_ST), 1)
    same_group = (row // SSM_GROUP) == (lane // SSM_STATE)

    def block_diag(ref):
        v = ref[pair]
        return jnp.where(same_group, jnp.concatenate([v, v], axis=1), 0.0)

    c_re, c_im = block_diag(cre_ref), block_diag(cim_ref)
    b_re, b_im = block_diag(bre_ref), block_diag(bim_ref)
    bb_re = q_re * b_re - q_im * b_im
    bb_im = q_re * b_im + q_im * b_re

    powers = {0: (jnp.ones_like(dar), jnp.zeros_like(dar)), 1: (mag * cos1, mag * sin1)}

    def power(k):
        if k not in powers:
            (p_re, p_im), (s_re, s_im) = power(k - 1), powers[1]
            powers[k] = (p_re * s_re - p_im * s_im, p_re * s_im + p_im * s_re)
        return powers[k]

    def cmul_cat(x_re, x_im, p, conj_out):
        p_re, p_im = p
        o_re = x_re * p_re - x_im * p_im
        o_im = x_re * p_im + x_im * p_re
        return jnp.concatenate([o_re, -o_im if conj_out else o_im], axis=1)

    for t in range(CHUNK):
        rs = slice(t * PAIR_CH, (t + 1) * PAIR_CH)
        cl_s[rs, :] = cmul_cat(c_re, c_im, power(t), True)
        cst_ref[pair, rs, :] =cmul_cat(c_re, c_im, power(t + 1), True).astype(BF16)
        bst_ref[pair, rs, :] = cmul_cat(bb_re, bb_im, power(CHUNK - 1 - t),
                                        False).astype(BF16)
    lre_ref[pair], lim_ref[pair] = power(CHUNK)

    cl_hi, cl_lo = _split_bf16(cl_s[...])
    bb = jnp.concatenate([bb_re, bb_im], axis=1)
    bb = jnp.concatenate([bb, jnp.zeros((LANES - PAIR_CH, 2 * PAIR_ST), F32)], axis=0)
    bb_hi, bb_lo = _split_bf16(bb)

    def dot_nt(a, b):
        return lax.dot_general(a, b, (((1,), (1,)), ((), ())), preferred_element_type=F32)

    kcol = dot_nt(cl_hi, bb_hi) + (dot_nt(cl_hi, bb_lo) + dot_nt(cl_lo, bb_hi))

    per_tile = LANES // PAIR_CH
    ri = lax.broadcasted_iota(jnp.int32, (PAIR_ROWS, LANES), 0)
    ci = lax.broadcasted_iota(jnp.int32, (PAIR_ROWS, LANES), 1)
    for tile in range(PAIR_ROWS // LANES):
        acc = jnp.where(ri == ci + tile * LANES,
                        dl_ref[pair, :, tile * LANES:(tile + 1) * LANES], 0.0)
        for q in range(per_tile):
            tau = tile * per_tile + q
            piece = kcol if q == 0 else pltpu.roll(kcol, q * PAIR_CH, axis=1)
            if tau:
                piece = jnp.concatenate(
                    [jnp.zeros((tau * PAIR_CH, LANES), F32),
                     piece[:PAIR_ROWS - tau * PAIR_CH, :]], axis=0)
            acc = acc + piece
        kt_ref[pair, :, tile * LANES:(tile + 1) * LANES] = acc.astype(BF16)


def _prep_call(c, w_ada, b_ada, a_re, a_im, b_re, b_im, c_re, c_im, d, log_dt):
    np_, pc, ps, st = N_PAIRS, PAIR_CH, PAIR_ST, SSM_STATE
    bsz = c.shape[0]
    mod_cols = w_ada.shape[1]
    steps = np_ // PREP_PAIRS
    mod_blk = mod_cols // steps
    row = lambda a: a.reshape(np_, 1, ps)
    ldt = jnp.broadcast_to(log_dt[:, None], (SSM_GROUPS, st))
    d_lane = jnp.tile(d.reshape(np_, 1, pc), (1, 1, CHUNK))
    bt = lambda a: jnp.transpose(a, (0, 2, 1)).reshape(np_, pc, st)
    spec = lambda *shape: pl.BlockSpec((PREP_PAIRS,) + shape, lambda p: (p, 0, 0))
    col_spec = lambda rows: pl.BlockSpec((rows, mod_blk), lambda p: (0, p))
    return pl.pallas_call(
        _prep_kernel,
        grid=(steps,),
        in_specs=[pl.BlockSpec((bsz, D_MODEL), lambda p: (0, 0)),
                  col_spec(D_MODEL), col_spec(1),
                  spec(1, ps), spec(1, ps), spec(1, ps), spec(pc, st), spec(pc, st),
                  spec(pc, st), spec(pc, st), spec(1, PAIR_ROWS)],
        out_specs=[col_spec(bsz),
                   spec(PAIR_ROWS, PAIR_ROWS), spec(PAIR_ROWS, 2 * ps),
                   spec(PAIR_ROWS, 2 * ps), spec(1, ps), spec(1, ps)],
        out_shape=[jax.ShapeDtypeStruct((bsz, mod_cols), F32),
                   jax.ShapeDtypeStruct((np_, PAIR_ROWS, PAIR_ROWS), BF16),
                   jax.ShapeDtypeStruct((np_, PAIR_ROWS, 2 * ps), BF16),
                   jax.ShapeDtypeStruct((np_, PAIR_ROWS, 2 * ps), BF16),
                   jax.ShapeDtypeStruct((np_, 1, ps), F32),
                   jax.ShapeDtypeStruct((np_, 1, ps), F32)],
        scratch_shapes=[pltpu.VMEM((PAIR_ROWS, 2 * ps), F32)],
        name="prep",
    )(c, w_ada, b_ada.reshape(1, -1), row(a_re), row(a_im), row(ldt),
      c_re.reshape(np_, pc, st), c_im.reshape(np_, pc, st), bt(b_re), bt(b_im), d_lane)


def _ssm_in_kernel(x_hbm, mod_ref, g_ref, w_ref, o_ref, xbuf, sems, wt_s):
    n_chunks = o_ref.shape[-1]
    n_s = pl.num_programs(1)
    flat = pl.program_id(0) * n_s + pl.program_id(1)
    slot = flat % 2

    def gather(block, buf_slot):
        bb, c0 = block // n_s, (block % n_s) * n_chunks
        return [pltpu.make_async_copy(
            x_hbm.at[bb, pl.ds(c0, n_chunks), tau, :],
            xbuf.at[buf_slot, pl.ds(tau * n_chunks, n_chunks), :],
            sems.at[buf_slot]) for tau in range(CHUNK)]

    def start_all(copies):
        for i, cp in enumerate(copies):
            cp.start(priority=i % 2)

    @pl.when(flat == 0)
    def _():
        start_all(gather(flat, slot))
        wt_s[...] = w_ref[...].T.astype(BF16)

    @pl.when(flat + 1 < pl.num_programs(0) * n_s)
    def _():
        start_all(gather(flat + 1, 1 - slot))

    for cp in gather(flat, slot):
        cp.wait()

    h = _rms_mod(xbuf[slot], g_ref[...], mod_ref[0, 1:2, :], mod_ref[0, 0:1, :])
    u_t = lax.dot_general(wt_s[...], h.astype(BF16), (((1,), (1,)), ((), ())),
                          preferred_element_type=F32)
    for tau in range(CHUNK):
        o_ref[0, tau] = u_t[:, tau * n_chunks:(tau + 1) * n_chunks].astype(o_ref.dtype)


def _ssm_in_call(x, mod, g, w_in, chunks_per_step):
    bsz, seq, _ = x.shape
    ts = chunks_per_step * CHUNK
    ssm_col_block = 2 * CONV_WIDTH // SSM_WIDTH
    x_chunked = x.reshape(bsz, seq // CHUNK, CHUNK, D_MODEL)
    return pl.pallas_call(
        _ssm_in_kernel,
        grid=(bsz, seq // ts),
        in_specs=[
            pl.BlockSpec(memory_space=pl.ANY),
            pl.BlockSpec((1, 6, D_MODEL), lambda b, s: (b, 0, 0)),
            pl.BlockSpec((1, D_MODEL), lambda b, s: (0, 0)),
            pl.BlockSpec((D_MODEL, SSM_WIDTH), lambda b, s: (0, ssm_col_block)),
        ],
        out_specs=pl.BlockSpec((1, CHUNK, SSM_WIDTH, chunks_per_step),
                               lambda b, s: (b, 0, 0, s)),
        out_shape=jax.ShapeDtypeStruct((bsz, CHUNK, SSM_WIDTH, seq // CHUNK), BF16),
        scratch_shapes=[pltpu.VMEM((2, ts, D_MODEL), F32),
                        pltpu.SemaphoreType.DMA((2,)),
                        pltpu.VMEM((SSM_WIDTH, D_MODEL), BF16)],
        compiler_params=pltpu.CompilerParams(
            dimension_semantics=("arbitrary", "arbitrary"),
            vmem_limit_bytes=VMEM_LIMIT),
        name="ssm_in",
    )(x_chunked, mod, g, w_in)


def _ssm_kernel(u_ref, kt_ref, bst_ref, cst_ref, lre_ref, lim_ref, *rest):
    n_w = (len(rest) - 7) // 2
    w_in_refs, o_ref, w_out_refs = rest[:n_w], rest[n_w], rest[n_w + 1:2 * n_w + 1]
    sre, sim, xre, xim, ys, yn = rest[2 * n_w + 1:]
    n_chunks = u_ref.shape[-1]
    pw = PAIR_ST

    _cast_rows(w_in_refs, w_out_refs)

    def load_pair(j):
        r = u_ref[0, :, j * PAIR_CH:(j + 1) * PAIR_CH, :]
        return r.reshape(PAIR_ROWS, n_chunks)

    for j in range(PAIRS_PER_STEP):
        s = lax.dot_general(load_pair(j), bst_ref[j], (((0,), (0,)), ((), ())),
                            preferred_element_type=F32)
        sre[:, j * pw:(j + 1) * pw] = s[:, :pw]
        sim[:, j * pw:(j + 1) * pw] = s[:, pw:]

    lre = lre_ref[0]
    lim = lim_ref[0]

    def body(c, carry):
        xr, xi = carry
        xre[pl.ds(c, 1), :] = xr
        xim[pl.ds(c, 1), :] = xi
        nr = lre * xr - lim * xi + sre[pl.ds(c, 1), :]
        ni = lre * xi + lim * xr + sim[pl.ds(c, 1), :]
        return nr, ni

    zero = jnp.zeros((1, PAIRS_PER_STEP * pw), F32)
    lax.fori_loop(0, n_chunks, body, (zero, zero), unroll=8)

    for j in range(PAIRS_PER_STEP):
        y_intra = jnp.dot(kt_ref[j], load_pair(j), preferred_element_type=F32)
        xp = jnp.concatenate([xre[:, j * pw:(j + 1) * pw],
                              xim[:, j * pw:(j + 1) * pw]], axis=1).astype(BF16)
        y_state = lax.dot_general(cst_ref[j], xp, (((1,), (1,)), ((), ())),
                                  preferred_element_type=F32)
        y = jax.nn.gelu(y_intra + y_state)
        ys[:, j * PAIR_CH:(j + 1) * PAIR_CH, :] = y.reshape(CHUNK, PAIR_CH, n_chunks)

    for k in range(yn.shape[0]):
        lanes = slice(k * LANES, (k + 1) * LANES)
        for t in range(CHUNK):
            yn[k, pl.ds(t, n_chunks, stride=CHUNK), :] = ys[t, lanes, :].T
        o_ref[0, 0, :, lanes] = yn[k].astype(o_ref.dtype)


def _ssm_call(u_t, kt, bst, cst, lam_re, lam_im, weights):
    bsz, _, _, n_chunks = u_t.shape
    seq = n_chunks * CHUNK
    pps = PAIRS_PER_STEP
    ch = pps * PAIR_CH
    state_w = pps * PAIR_ST
    steps = N_PAIRS // pps
    lam_re = lam_re.reshape(steps, 1, state_w)
    lam_im = lam_im.reshape(steps, 1, state_w)
    n_steps = steps * bsz
    w_specs = [pl.BlockSpec((w.shape[0] // n_steps, w.shape[1]),
                            lambda p, b: (p * bsz + b, 0)) for w in weights]
    outs = pl.pallas_call(
        _ssm_kernel,
        grid=(steps, bsz),
        in_specs=[
            pl.BlockSpec((1, CHUNK, ch, n_chunks), lambda p, b: (b, 0, p, 0)),
            pl.BlockSpec((pps, PAIR_ROWS, PAIR_ROWS), lambda p, b: (p, 0, 0)),
            pl.BlockSpec((pps, PAIR_ROWS, 2 * PAIR_ST), lambda p, b: (p, 0, 0)),
            pl.BlockSpec((pps, PAIR_ROWS, 2 * PAIR_ST), lambda p, b: (p, 0, 0)),
            pl.BlockSpec((1, 1, state_w), lambda p, b: (p, 0, 0)),
            pl.BlockSpec((1, 1, state_w), lambda p, b: (p, 0, 0)),
        ] + w_specs,
        out_specs=[pl.BlockSpec((1, 1, seq, ch), lambda p, b: (b, p, 0, 0))] + w_specs,
        out_shape=[jax.ShapeDtypeStruct((bsz, steps, seq, ch), BF16)] + [
            jax.ShapeDtypeStruct(w.shape, BF16) for w in weights],
        scratch_shapes=[pltpu.VMEM((n_chunks, state_w), F32)] * 4 + [
            pltpu.VMEM((CHUNK, ch, n_chunks), F32),
            pltpu.VMEM((ch // LANES, seq, LANES), F32)],
        compiler_params=pltpu.CompilerParams(
            dimension_semantics=("arbitrary", "arbitrary"),
            vmem_limit_bytes=VMEM_LIMIT),
        name="ssm",
    )(u_t, kt, bst, cst, lam_re, lam_im, *weights)
    return outs[0], outs[1:]


def _mixer_kernel(x_ref, y_ref, mod_ref, g_ref, wc_ref, wgc0_ref, wgc1_ref, wgs0_ref,
                  wgs1_ref, cw_ref, cb_ref, lng_ref, lnb_ref, proj_ref, glu_ref,
                  wout_ref, wf_ref, o_ref, wf_o_ref, ubuf, cbuf, sbuf):
    _cast_rows((wf_ref,), (wf_o_ref,))

    @pl.when(pl.program_id(1) == 0)
    def _():
        ubuf[:, 0:CONV_HALO, :] = jnp.zeros((CONV_WIDTH // LANES, CONV_HALO, LANES), F32)

    for r0 in range(0, x_ref.shape[1], MIX_ROWS):
        _mixer_rows(r0, x_ref, y_ref, mod_ref, g_ref, wc_ref, wgc0_ref, wgc1_ref,
                    wgs0_ref, wgs1_ref, cw_ref, cb_ref, lng_ref, lnb_ref, proj_ref,
                    glu_ref, wout_ref, o_ref, ubuf, cbuf, sbuf)


def _mixer_rows(r0, x_ref, y_ref, mod_ref, g_ref, wc_ref, wgc0_ref, wgc1_ref, wgs0_ref,
                wgs1_ref, cw_ref, cb_ref, lng_ref, lnb_ref, proj_ref, glu_ref,
                wout_ref, o_ref, ubuf, cbuf, sbuf):
    ts = MIX_ROWS
    rows = slice(r0, r0 + ts)
    halo = CONV_HALO
    half = D_MODEL // 2
    n_lt = CONV_WIDTH // LANES

    x = x_ref[0, rows, :]
    h = _rms_mod(x, g_ref[...], mod_ref[0, 1:2, :], mod_ref[0, 0:1, :]).astype(BF16)

    zc = jnp.dot(h, wc_ref[...], preferred_element_type=F32)
    u = zc[:, :CONV_WIDTH] * _sigmoid(zc[:, CONV_WIDTH:])
    for k in range(n_lt):
        ubuf[k, halo:halo + ts, :] = u[:, k * LANES:(k + 1) * LANES]

    y_in = jnp.concatenate([y_ref[0, k, rows, :] for k in range(y_ref.shape[1])], axis=1)

    def dot_h(w_ref):
        return jnp.dot(h, w_ref[...], preferred_element_type=F32)

    def dot_y(k):
        return jnp.dot(y_in, glu_ref[:, k * half:(k + 1) * half],
                       preferred_element_type=F32)

    gc0, gs0, za0, zb0, gc1, gs1, za1, zb1 = range(N_SIDE)

    def put(k, v):
        sbuf[k] = v

    dyn_zero = jnp.minimum(pl.program_id(1), 0)
    never = lax.broadcasted_iota(jnp.int32, (SUBLANES, LANES), 0) < dyn_zero

    def anchored(term, k):
        fence = sbuf[dyn_zero, k * SUBLANES:(k + 1) * SUBLANES, 0:LANES]
        head = jnp.where(never, fence, term[0:SUBLANES, :])
        return jnp.concatenate([head, term[SUBLANES:, :]], axis=0)

    side_jobs = [
        lambda: put(gc0, dot_h(wgc0_ref)), lambda: put(gs0, dot_h(wgs0_ref)),
        lambda: put(za0, dot_y(0)), lambda: put(zb0, dot_y(2)),
        lambda: put(gc1, dot_h(wgc1_ref)), lambda: put(gs1, dot_h(wgs1_ref)),
        lambda: put(za1, dot_y(1)), lambda: put(zb1, dot_y(3)),
    ]
    first = halo - (CONV_KERNEL - 1)
    n_rt = ts // CONV_ROWS
    for i in range(n_rt):
        for k in range(n_lt):
            lanes = slice(k * LANES, (k + 1) * LANES)
            acc = None
            for j in range(CONV_KERNEL):
                row0 = first + j + i * CONV_ROWS
                term = cw_ref[j:j + 1, lanes] * ubuf[k, row0:row0 + CONV_ROWS, :]
                if j == 0 and i > 0:
                    term = anchored(term, k)
                acc = term if acc is None else acc + term
            cbuf[i * CONV_ROWS:(i + 1) * CONV_ROWS, lanes] = acc + cb_ref[:, lanes]
        if (i + 1) * len(side_jobs) % n_rt == 0:
            side_jobs[(i + 1) * len(side_jobs) // n_rt - 1]()
    for k in range(n_lt):
        ubuf[k, 0:halo, :] = ubuf[k, ts:ts + halo, :]
    acc = cbuf[...]
    mu = jnp.mean(acc, axis=-1, keepdims=True)
    cen = acc - mu
    var = jnp.mean(cen * cen, axis=-1, keepdims=True)
    yn = cen * lax.rsqrt(var + EPS) * lng_ref[...] + lnb_ref[...]
    yn = yn * _sigmoid(yn)
    y_conv = jnp.dot(yn.astype(BF16), proj_ref[...], preferred_element_type=F32)

    def merge(conv_half, g_conv, g_ssm, z_a, z_b):
        y_ssm = sbuf[z_a] * _sigmoid(sbuf[z_b])
        return _sigmoid(sbuf[g_conv]) * conv_half + _sigmoid(sbuf[g_ssm]) * y_ssm

    merged = jnp.concatenate(
        [merge(y_conv[:, :half], gc0, gs0, za0, zb0),
         merge(y_conv[:, half:], gc1, gs1, za1, zb1)], axis=1)
    upd = jnp.dot(merged.astype(BF16), wout_ref[...], preferred_element_type=F32)
    o_ref[0, rows, :] = x + mod_ref[0, 2:3, :] * upd


def _const_spec(shape, index=None):
    index = (0,) * len(shape) if index is None else index
    return pl.BlockSpec(shape, lambda b, s: index, pipeline_mode=pl.Buffered(1))


def _mixer_call(x, y, mod, g, w_in_b, cw, cb, lng, lnb, proj, glu, wout, w_ffn_in, ts):
    bsz, seq, _ = x.shape
    n_s = seq // ts
    gate0 = (2 * CONV_WIDTH + SSM_WIDTH) // GATE_BLOCK
    gate_spec = lambda i: _const_spec((D_MODEL, GATE_BLOCK), (0, gate0 + i))
    wf_rows, wf_cols = w_ffn_in.shape
    wf_spec = pl.BlockSpec((wf_rows // (bsz * n_s), wf_cols), lambda b, s: (b * n_s + s, 0))
    return pl.pallas_call(
        _mixer_kernel,
        grid=(bsz, seq // ts),
        in_specs=[
            pl.BlockSpec((1, ts, D_MODEL), lambda b, s: (b, s, 0)),
            pl.BlockSpec((1, y.shape[1], ts, y.shape[3]), lambda b, s: (b, 0, s, 0)),
            pl.BlockSpec((1, 6, D_MODEL), lambda b, s: (b, 0, 0)),
            _const_spec((1, D_MODEL)),
            _const_spec((D_MODEL, 2 * CONV_WIDTH)),
            gate_spec(0), gate_spec(1), gate_spec(2), gate_spec(3),
            _const_spec((CONV_KERNEL, CONV_WIDTH)),
            _const_spec((1, CONV_WIDTH)),
            _const_spec((1, CONV_WIDTH)),
            _const_spec((1, CONV_WIDTH)),
            _const_spec((CONV_WIDTH, D_MODEL)),
            _const_spec((SSM_WIDTH, 2 * D_MODEL)),
            _const_spec((D_MODEL, D_MODEL)),
            wf_spec,
        ],
        out_specs=[pl.BlockSpec((1, ts, D_MODEL), lambda b, s: (b, s, 0)), wf_spec],
        out_shape=[jax.ShapeDtypeStruct(x.shape, F32),
                   jax.ShapeDtypeStruct(w_ffn_in.shape, BF16)],
        scratch_shapes=[
            pltpu.VMEM((CONV_WIDTH // LANES, MIX_ROWS + CONV_HALO, LANES), F32),
            pltpu.VMEM((MIX_ROWS, CONV_WIDTH), F32),
            pltpu.VMEM((N_SIDE, MIX_ROWS, D_MODEL // 2), F32)],
        compiler_params=pltpu.CompilerParams(
            dimension_semantics=("arbitrary", "arbitrary"),
            vmem_limit_bytes=VMEM_LIMIT),
        name="mixer",
    )(x, y, mod, g, w_in_b, w_in_b, w_in_b, w_in_b, w_in_b, cw, cb, lng, lnb, proj, glu,
      wout, w_ffn_in)


def _ffn_kernel(x_ref, mod_ref, g_ref, wg_ref, wu_ref, wo_ref, fg_ref, o_ref, act_s):
    x = x_ref[0]
    h = _rms_mod(x, g_ref[...], mod_ref[0, 4:5, :], mod_ref[0, 3:4, :]).astype(BF16)
    for c0 in range(0, FFN_HIDDEN, FFN_CHUNK):
        cols = slice(c0, c0 + FFN_CHUNK)
        fg = jnp.dot(h, wg_ref[:, cols], preferred_element_type=F32)
        fu = jnp.dot(h, wu_ref[:, cols], preferred_element_type=F32)
        act_s[:, cols] = (fg * _sigmoid(fg) * fu).astype(BF16)
    upd = jnp.dot(act_s[...], wo_ref[...], preferred_element_type=F32)
    x2 = x + mod_ref[0, 5:6, :] * upd
    ms = jnp.mean(x2 * x2, axis=-1, keepdims=True)
    o_ref[0] = x2 * lax.rsqrt(ms + EPS) * fg_ref[...]


def _ffn_call(x, mod, g, w_ffn_in_b, wo, final_g, ts):
    bsz, seq, _ = x.shape
    return pl.pallas_call(
        _ffn_kernel,
        grid=(bsz, seq // ts),
        in_specs=[
            pl.BlockSpec((1, ts, D_MODEL), lambda b, s: (b, s, 0)),
            pl.BlockSpec((1, 6, D_MODEL), lambda b, s: (b, 0, 0)),
            _const_spec((1, D_MODEL)),
            _const_spec((D_MODEL, FFN_HIDDEN), (0, 0)),
            _const_spec((D_MODEL, FFN_HIDDEN), (0, 1)),
            _const_spec((FFN_HIDDEN, D_MODEL)),
            _const_spec((1, D_MODEL)),
        ],
        out_specs=pl.BlockSpec((1, ts, D_MODEL), lambda b, s: (b, s, 0)),
        out_shape=jax.ShapeDtypeStruct(x.shape, F32),
        scratch_shapes=[pltpu.VMEM((ts, FFN_HIDDEN), BF16)],
        compiler_params=pltpu.CompilerParams(
            dimension_semantics=("arbitrary", "arbitrary"),
            vmem_limit_bytes=VMEM_LIMIT),
        name="ffn",
    )(x, mod, g, w_ffn_in_b, w_ffn_in_b, wo, final_g)


def kernel(x, c, w_ada, b_ada, norm1_g, w_in, conv_w, conv_b, conv_ln_g, conv_ln_b,
           conv_proj, ssm_a_re, ssm_a_im, ssm_b_re, ssm_b_im, ssm_c_re, ssm_c_im,
           ssm_d, ssm_log_dt, ssm_glu, w_out, norm2_g, w_ffn_in, w_ffn_out, final_g):
    bsz, seq, _ = x.shape
    assert w_ada.shape[0] == 1, "single layer only"

    mod, *ops = _prep_call(c, w_ada[0], b_ada[0], ssm_a_re[0], ssm_a_im[0], ssm_b_re[0],
                           ssm_b_im[0], ssm_c_re[0], ssm_c_im[0], ssm_d[0],
                           ssm_log_dt[0])
    mod = mod.reshape(bsz, -1, D_MODEL)

    g1 = norm1_g[0].reshape(1, -1)

    u_t = _ssm_in_call(x, mod, g1, w_in[0], 128)
    y, (w_in_b, proj_b, glu_b, w_out_b, w_ffn_out_b) = _ssm_call(
        u_t, *ops, (w_in[0], conv_proj[0], ssm_glu[0], w_out[0], w_ffn_out[0]))

    x1, w_ffn_in_b = _mixer_call(
        x, y, mod, g1, w_in_b, conv_w[0], conv_b[0].reshape(1, -1),
        conv_ln_g[0].reshape(1, -1), conv_ln_b[0].reshape(1, -1),
        proj_b, glu_b, w_out_b, w_ffn_in[0], 1024)

    return _ffn_call(x1, mod, norm2_g[0].reshape(1, -1), w_ffn_in_b, w_ffn_out_b,
                     final_g.reshape(1, -1), 1024)
```
